```python
import math
import jax, jax.numpy as jnp
from jax import lax
import numpy as np

D_MODEL = 2048
BATCH = 4
SEQ = 4096
DEPTH = 4
DEC_BATCH = 16
DEC_SEQ = 16
PAST_LEN = 1024

CHUNK = 64
N_MIXERS = 2
N_A = (DEPTH + 1) // 2
N_B = DEPTH // 2
PLE_DIM = 256
EPS = 1e-6
Q_BLOCK = 128

MLA_HEADS = 16
MLA_Q_LORA = 512
MLA_KV_LORA = 512
MLA_NOPE = 128
MLA_ROPE = 64
MLA_V = 128
MLA_THETA = 10000.0
MLA_SCALE = (MLA_NOPE + MLA_ROPE) ** -0.5

SWA_Q_HEADS = 32
SWA_KV_HEADS = 4
SWA_GROUP = SWA_Q_HEADS // SWA_KV_HEADS
SWA_HEAD_DIM = 64
SWA_ROT = SWA_HEAD_DIM // 4
SWA_SCALE = SWA_HEAD_DIM ** -0.5
WINDOW = 128
WINDOW_CHUNKS = WINDOW // CHUNK
BAND = (WINDOW_CHUNKS + 1) * CHUNK
ROPE_THETA = 500000.0

D_FF = ((8 * D_MODEL + 3 * 256 - 1) // (3 * 256)) * 256

kernel_name = 'hybrid_mla_swa_streaming_step'


def rmsnorm(x, g):
    xf = x.astype(jnp.float32)
    y = xf * lax.rsqrt(jnp.mean(xf * xf, axis=-1, keepdims=True) + EPS)
    return (y * g.astype(jnp.float32)).astype(x.dtype)


def rope(x, pos, theta):
    r = x.shape[-1]
    half = r // 2
    inv = jnp.power(jnp.float32(theta), -jnp.arange(0, r, 2, dtype=jnp.float32) / r)
    ang = pos.astype(jnp.float32)[:, None] * inv[None, :]
    shape = (ang.shape[0],) + (1,) * (x.ndim - 3) + (half,)
    cos = jnp.cos(ang).reshape(shape)
    sin = jnp.sin(ang).reshape(shape)
    xf = x.astype(jnp.float32)
    x1, x2 = xf[..., :half], xf[..., half:]
    return jnp.concatenate([x1 * cos - x2 * sin, x2 * cos + x1 * sin], axis=-1).astype(x.dtype)


def partial_rope(x, pos):
    return jnp.concatenate([rope(x[..., :SWA_ROT], pos, ROPE_THETA), x[..., SWA_ROT:]], axis=-1)


def mla_project(hn, pos, w, j):
    B, S, _ = hn.shape
    a = hn @ w['w_mla_in'][j]
    c_q, c_kv, k_r = jnp.split(a, [MLA_Q_LORA, MLA_Q_LORA + MLA_KV_LORA], axis=-1)
    q = (rmsnorm(c_q, w['g_mla_q_a'][j]) @ w['w_mla_q_up'][j]).reshape(B, S, MLA_HEADS, MLA_NOPE + MLA_ROPE)
    q_nope = rmsnorm(q[..., :MLA_NOPE], w['g_mla_q_nope'][j])
    q_rope = rope(rmsnorm(q[..., MLA_NOPE:], w['g_mla_q_rope'][j]), pos, MLA_THETA)
    c_kv = rmsnorm(c_kv, w['g_mla_kv_a'][j])
    k_rope = rope(rmsnorm(k_r, w['g_mla_k_rope'][j]), pos, MLA_THETA)
    return q_nope, q_rope, c_kv, k_rope


def mla_expand(c_kv, w, j):
    B, S, _ = c_kv.shape
    kv = (c_kv @ w['w_mla_kv_up'][j]).reshape(B, S, MLA_HEADS, MLA_NOPE + MLA_V)
    return rmsnorm(kv[..., :MLA_NOPE], w['g_mla_k_nope'][j]), kv[..., MLA_NOPE:]


def mla_attend(q_nope, q_rope, k_nope, k_rope, v, q_pos, k_pos):
    s = (jnp.einsum('bqhn,bkhn->bhqk', q_nope, k_nope)
         + jnp.einsum('bqhr,bkr->bhqk', q_rope, k_rope)).astype(jnp.float32) * MLA_SCALE
    vis = (k_pos // CHUNK)[None, :] <= (q_pos // CHUNK)[:, None]
    s = jnp.where(vis[None, None], s, -jnp.inf)
    p = jax.nn.softmax(s, axis=-1).astype(v.dtype)
    return jnp.einsum('bhqk,bkhv->bqhv', p, v)


def mla_mixer(hn, pos, w, j, cache):
    B, S, _ = hn.shape
    q_nope, q_rope, c_kv, k_rope = mla_project(hn, pos, w, j)
    if cache is None:
        k_nope, v = mla_expand(c_kv, w, j)
        nb = S // Q_BLOCK

        def blockify(t):
            return jnp.swapaxes(t.reshape((B, nb, Q_BLOCK) + t.shape[2:]), 0, 1)

        o = lax.map(lambda a: mla_attend(a[0], a[1], k_nope, k_rope, v, a[2], pos),
                    (blockify(q_nope), blockify(q_rope), pos.reshape(nb, Q_BLOCK)))
        o = jnp.swapaxes(o, 0, 1).reshape(B, S, MLA_HEADS * MLA_V)
    else:
        lat_c, kr_c = cache
        lat_all = jnp.concatenate([lat_c, c_kv], axis=1)
        kr_all = jnp.concatenate([kr_c, k_rope], axis=1)
        k_nope, v = mla_expand(lat_all, w, j)
        k_pos = jnp.arange(lat_all.shape[1], dtype=jnp.int32)
        o = mla_attend(q_nope, q_rope, k_nope, kr_all, v, pos, k_pos).reshape(B, S, MLA_HEADS * MLA_V)
    return o @ w['w_mla_out'][j], c_kv, k_rope


def swa_project(hn, pos, w, j):
    B, S, _ = hn.shape
    qkv = hn @ w['w_swa_qkv'][j]
    nq = SWA_Q_HEADS * SWA_HEAD_DIM
    nk = SWA_KV_HEADS * SWA_HEAD_DIM
    q, k, v = jnp.split(qkv, [nq, nq + nk], axis=-1)
    q = partial_rope(rmsnorm(q.reshape(B, S, SWA_Q_HEADS, SWA_HEAD_DIM), w['g_swa_q'][j]), pos)
    q = q.reshape(B, S, SWA_KV_HEADS, SWA_GROUP, SWA_HEAD_DIM)
    k = partial_rope(rmsnorm(k.reshape(B, S, SWA_KV_HEADS, SWA_HEAD_DIM), w['g_swa_k'][j]), pos)
    v = v.reshape(B, S, SWA_KV_HEADS, SWA_HEAD_DIM)
    return q, k, v


def swa_attend(q, k, v, sinks, valid):
    s = jnp.einsum('bcqkgd,bcjkd->bckgqj', q, k).astype(jnp.float32) * SWA_SCALE
    s = jnp.where(valid[None, :, None, None, None, :], s, -jnp.inf)
    sk = sinks.astype(jnp.float32).reshape(SWA_KV_HEADS, SWA_GROUP)[None, None, :, :, None, None]
    m = jnp.maximum(jnp.max(s, axis=-1, keepdims=True), sk)
    e = jnp.exp(s - m)
    p = e / (jnp.sum(e, axis=-1, keepdims=True) + jnp.exp(sk - m))
    return jnp.einsum('bckgqj,bcjkd->bcqkgd', p.astype(v.dtype), v)


def swa_mixer(hn, pos, w, j, cache):
    B, S, _ = hn.shape
    q, k, v = swa_project(hn, pos, w, j)
    sinks = w['swa_sinks'][j]
    if cache is None:
        C = S // CHUNK
        qc = q.reshape(B, C, CHUNK, SWA_KV_HEADS, SWA_GROUP, SWA_HEAD_DIM)
        pad = ((0, 0), (WINDOW, 0), (0, 0), (0, 0))

        def band(t):
            tc = jnp.pad(t, pad).reshape(B, C + WINDOW_CHUNKS, CHUNK, SWA_KV_HEADS, SWA_HEAD_DIM)
            return jnp.concatenate([tc[:, o_:o_ + C] for o_ in range(WINDOW_CHUNKS + 1)], axis=2)

        key_pos = jnp.arange(C)[:, None] * CHUNK - WINDOW + jnp.arange(BAND)[None, :]
        o = swa_attend(qc, band(k), band(v), sinks, key_pos >= 0)
        keep = min(WINDOW, S)
        new_k, new_v = k[:, S - keep:], v[:, S - keep:]
    else:
        ck, cv = cache
        k_all = jnp.concatenate([ck, k], axis=1)
        v_all = jnp.concatenate([cv, v], axis=1)
        valid = jnp.ones((1, k_all.shape[1]), dtype=bool)
        o = swa_attend(q[:, None], k_all[:, None], v_all[:, None], sinks, valid)
        keep = ck.shape[1]
        new_k, new_v = k_all[:, -keep:], v_all[:, -keep:]
    o = o.reshape(B, S, SWA_Q_HEADS * SWA_HEAD_DIM)
    return o @ w['w_swa_out'][j], new_k, new_v


def run_trunk(x, p, pos, caches, w):
    h = x
    lats, krs, sks, svs = [], [], [], []
    for i in range(DEPTH):
        j = i // N_MIXERS
        hn = rmsnorm(h, w['g_attn_norm'][i])
        if i % N_MIXERS == 0:
            c = None if caches is None else (caches[0][j], caches[1][j])
            mix, lat, kr = mla_mixer(hn, pos, w, j, c)
            lats.append(lat)
            krs.append(kr)
        else:
            c = None if caches is None else (caches[2][j], caches[3][j])
            mix, sk, sv = swa_mixer(hn, pos, w, j, c)
            sks.append(sk)
            svs.append(sv)
        h = h + mix
        hn = rmsnorm(h, w['g_ffn_norm'][i])
        h = h + (jax.nn.silu(hn @ w['w_ffn_gate'][i]) * (hn @ w['w_ffn_up'][i])) @ w['w_ffn_down'][i]
        gate = jax.nn.sigmoid(rmsnorm(h, w['g_ple_norm'][i]) @ w['w_ple_gate'][i])
        h = h + gate * (p[i] @ w['w_ple_proj'][i])
    y = rmsnorm(h, w['g_final'])
    return y, jnp.stack(lats), jnp.stack(krs), jnp.stack(sks), jnp.stack(svs)


def setup_inputs(seed: int = 0) -> dict:
    key = jax.random.key(seed)
    ks = iter(jax.random.split(key, 48))

    def nrm(shape, scale=1.0):
        return scale * jax.random.normal(next(ks), shape, jnp.float32)

    def lin(shape):
        return nrm(shape, shape[-2] ** -0.5)

    def gain(shape):
        return 1.0 + nrm(shape, 0.02)

    keep = min(WINDOW, PAST_LEN)
    return {
        'x_prompt': nrm((BATCH, SEQ, D_MODEL)),
        'x_sample': nrm((DEC_BATCH, DEC_SEQ, D_MODEL)),
        'p_prompt': nrm((DEPTH, BATCH, SEQ, PLE_DIM)),
        'p_sample': nrm((DEPTH, DEC_BATCH, DEC_SEQ, PLE_DIM)),
        'cache_mla_latent': nrm((N_A, DEC_BATCH, PAST_LEN, MLA_KV_LORA)),
        'cache_mla_krope': nrm((N_A, DEC_BATCH, PAST_LEN, MLA_ROPE)),
        'state_swa_k': nrm((N_B, DEC_BATCH, keep, SWA_KV_HEADS, SWA_HEAD_DIM)),
        'state_swa_v': nrm((N_B, DEC_BATCH, keep, SWA_KV_HEADS, SWA_HEAD_DIM)),
        'g_attn_norm': gain((DEPTH, D_MODEL)),
        'w_mla_in': lin((N_A, D_MODEL, MLA_Q_LORA + MLA_KV_LORA + MLA_ROPE)),
        'g_mla_q_a': gain((N_A, MLA_Q_LORA)),
        'w_mla_q_up': lin((N_A, MLA_Q_LORA, MLA_HEADS * (MLA_NOPE + MLA_ROPE))),
        'g_mla_kv_a': gain((N_A, MLA_KV_LORA)),
        'w_mla_kv_up': lin((N_A, MLA_KV_LORA, MLA_HEADS * (MLA_NOPE + MLA_V))),
        'g_mla_q_nope': gain((N_A, MLA_NOPE)),
        'g_mla_q_rope': gain((N_A, MLA_ROPE)),
        'g_mla_k_nope': gain((N_A, MLA_NOPE)),
        'g_mla_k_rope': gain((N_A, MLA_ROPE)),
        'w_mla_out': lin((N_A, MLA_HEADS * MLA_V, D_MODEL)),
        'w_swa_qkv': lin((N_B, D_MODEL, (SWA_Q_HEADS + 2 * SWA_KV_HEADS) * SWA_HEAD_DIM)),
        'g_swa_q': gain((N_B, SWA_HEAD_DIM)),
        'g_swa_k': gain((N_B, SWA_HEAD_DIM)),
        'swa_sinks': nrm((N_B, SWA_Q_HEADS), 0.5),
        'w_swa_out': lin((N_B, SWA_Q_HEADS * SWA_HEAD_DIM, D_MODEL)),
        'g_ffn_norm': gain((DEPTH, D_MODEL)),
        'w_ffn_gate': lin((DEPTH, D_MODEL, D_FF)),
        'w_ffn_up': lin((DEPTH, D_MODEL, D_FF)),
        'w_ffn_down': lin((DEPTH, D_FF, D_MODEL)),
        'g_ple_norm': gain((DEPTH, D_MODEL)),
        'w_ple_gate': lin((DEPTH, D_MODEL, D_MODEL)),
        'w_ple_proj': lin((DEPTH, PLE_DIM, D_MODEL)),
        'g_final': gain((D_MODEL,)),
    }


def reference(x_prompt, x_sample, p_prompt, p_sample, cache_mla_latent, cache_mla_krope,
              state_swa_k, state_swa_v, g_attn_norm, w_mla_in, g_mla_q_a, w_mla_q_up,
              g_mla_kv_a, w_mla_kv_up, g_mla_q_nope, g_mla_q_rope, g_mla_k_nope, g_mla_k_rope,
              w_mla_out, w_swa_qkv, g_swa_q, g_swa_k, swa_sinks, w_swa_out, g_ffn_norm,
              w_ffn_gate, w_ffn_up, w_ffn_down, g_ple_norm, w_ple_gate, w_ple_proj, g_final):
    w = dict(g_attn_norm=g_attn_norm, w_mla_in=w_mla_in, g_mla_q_a=g_mla_q_a, w_mla_q_up=w_mla_q_up,
             g_mla_kv_a=g_mla_kv_a, w_mla_kv_up=w_mla_kv_up, g_mla_q_nope=g_mla_q_nope,
             g_mla_q_rope=g_mla_q_rope, g_mla_k_nope=g_mla_k_nope, g_mla_k_rope=g_mla_k_rope,
             w_mla_out=w_mla_out, w_swa_qkv=w_swa_qkv, g_swa_q=g_swa_q, g_swa_k=g_swa_k,
             swa_sinks=swa_sinks, w_swa_out=w_swa_out, g_ffn_norm=g_ffn_norm, w_ffn_gate=w_ffn_gate,
             w_ffn_up=w_ffn_up, w_ffn_down=w_ffn_down, g_ple_norm=g_ple_norm, w_ple_gate=w_ple_gate,
             w_ple_proj=w_ple_proj, g_final=g_final)
    S = x_prompt.shape[1]
    T = x_sample.shape[1]
    past = cache_mla_latent.shape[2]
    pos_p = jnp.arange(S, dtype=jnp.int32)
    pos_s = past + jnp.arange(T, dtype=jnp.int32)
    y_prompt, lat_p, kr_p, sk_p, sv_p = run_trunk(x_prompt, p_prompt, pos_p, None, w)
    y_sample, lat_s, kr_s, sk_s, sv_s = run_trunk(
        x_sample, p_sample, pos_s, (cache_mla_latent, cache_mla_krope, state_swa_k, state_swa_v), w)
    return (y_prompt, y_sample, lat_p, kr_p, sk_p, sv_p, lat_s, kr_s, sk_s, sv_s)
```

```python
import functools

import numpy as np
import jax
import jax.numpy as jnp
from jax import lax
from jax.experimental import pallas as pl
from jax.experimental.pallas import tpu as pltpu

F32 = jnp.float32
BF16 = jnp.bfloat16

EPS = 1e-6
CHUNK = 64
WINDOW = 128
MLA_THETA = 10000.0
SWA_THETA = 500000.0
MLA_NOPE = 128
MLA_ROPE = 64
MLA_V = 128
SWA_HEAD_DIM = 64
SWA_ROT = SWA_HEAD_DIM // 4

LANES = 128
VMEM_LIMIT = 56 * 1024 * 1024
NEG = -1e30


def _params(*sem):
    return pltpu.CompilerParams(dimension_semantics=sem, vmem_limit_bytes=VMEM_LIMIT)


def _resident(shape):
    zeros = (0,) * len(shape)
    return pl.BlockSpec(shape, lambda *_: zeros, pipeline_mode=pl.Buffered(1))


def _rms(x, g):
    return x * lax.rsqrt(jnp.mean(x * x, axis=-1, keepdims=True) + EPS) * g


def _dot(a, b):
    return jnp.dot(a, b, preferred_element_type=F32)


def _dot_t(a, b):
    return lax.dot_general(a, b, (((1,), (1,)), ((), ())), preferred_element_type=F32)


def _rope_dup(x, g, tab):
    ss = jnp.sum(x * x, axis=-1, keepdims=True)
    y = x * lax.rsqrt(ss * (1.0 / LANES) + EPS) * g
    t = y * tab
    return t + pltpu.roll(t, LANES // 2, axis=1)


def _mla_expand(latb, kr, wkn_ref, wv_ref, gkn, k_ref, v_ref, heads):
    krb = kr[:, :MLA_ROPE].astype(BF16)
    for p in range(heads // 2):
        kn2 = _dot(latb, wkn_ref[:, p * 256:(p + 1) * 256])
        v2 = _dot(latb, wv_ref[:, p * 256:(p + 1) * 256])
        for u in range(2):
            h = 2 * p + u
            k_ref[h, :, 0:MLA_NOPE] = _rms(kn2[:, u * 128:(u + 1) * 128], gkn).astype(BF16)
            k_ref[h, :, MLA_NOPE:MLA_NOPE + MLA_ROPE] = krb
            v_ref[h] = v2[:, u * 128:(u + 1) * 128].astype(BF16)


def _mla_proj_kernel(h_ref, gattn_ref, win_ref, gqa_ref, gkva_ref, gkr_ref, tab_ref, wq_ref,
                     gqn_ref, gqr_ref, wkn_ref, wv_ref, gkn_ref,
                     lat_ref, kr_ref, q_ref, k_ref, v_ref, *, heads, q_lora, kv_lora, scale):
    hn = _rms(h_ref[...], gattn_ref[...]).astype(BF16)
    a = _dot(hn, win_ref[...])
    c_q = _rms(a[:, :q_lora], gqa_ref[...]).astype(BF16)
    lat = _rms(a[:, q_lora:q_lora + kv_lora], gkva_ref[...])
    lat_ref[...] = lat
    tab = tab_ref[...]
    kr = _rope_dup(a[:, q_lora + kv_lora:], gkr_ref[...], tab)
    kr_ref[...] = kr[:, :MLA_ROPE]
    _mla_expand(lat.astype(BF16), kr, wkn_ref, wv_ref, gkn_ref[...], k_ref, v_ref, heads)
    gqn = gqn_ref[...] * scale
    gqr = gqr_ref[...] * scale
    for h in range(heads):
        q = _dot(c_q, wq_ref[:, h * 256:(h + 1) * 256])
        q_ref[h, :, 0:MLA_NOPE] = _rms(q[:, :128], gqn).astype(BF16)
        qr = _rope_dup(q[:, 128:], gqr, tab)
        q_ref[h, :, MLA_NOPE:MLA_NOPE + MLA_ROPE] = qr[:, :MLA_ROPE].astype(BF16)


def _mla_expand_kernel(lat_ref, kr_ref, wkn_ref, wv_ref, gkn_ref, k_ref, v_ref, *, heads):
    _mla_expand(lat_ref[...].astype(BF16), kr_ref[...], wkn_ref, wv_ref, gkn_ref[...],
                k_ref, v_ref, heads)


def _mla_proj(h, w, tab, tm):
    n, d = h.shape
    heads = w['heads']
    q_lora, kv_lora = w['g_q_a'].shape[1], w['g_kv_a'].shape[1]
    qk = MLA_NOPE + MLA_ROPE
    nt = tab.shape[0] // tm
    row = lambda i: (i, 0)
    hrow = lambda i: (0, i, 0)
    kern = functools.partial(_mla_proj_kernel, heads=heads, q_lora=q_lora, kv_lora=kv_lora,
                             scale=qk ** -0.5)
    return pl.pallas_call(
        kern,
        grid=(n // tm,),
        in_specs=[
            pl.BlockSpec((tm, d), row),
            _resident(w['g_attn'].shape), _resident(w['w_in'].shape),
            _resident(w['g_q_a'].shape), _resident(w['g_kv_a'].shape), _resident(w['g_k_rope'].shape),
            pl.BlockSpec((tm, LANES), lambda i: (i % nt, 0)),
            _resident(w['w_q'].shape), _resident(w['g_q_nope'].shape), _resident(w['g_q_rope'].shape),
            _resident(w['w_kn'].shape), _resident(w['w_v'].shape), _resident(w['g_k_nope'].shape),
        ],
        out_specs=[
            pl.BlockSpec((tm, kv_lora), row),
            pl.BlockSpec((tm, MLA_ROPE), row),
            pl.BlockSpec((heads, tm, qk), hrow),
            pl.BlockSpec((heads, tm, qk), hrow),
            pl.BlockSpec((heads, tm, MLA_V), hrow),
        ],
        out_shape=[
            jax.ShapeDtypeStruct((n, kv_lora), F32),
            jax.ShapeDtypeStruct((n, MLA_ROPE), F32),
            jax.ShapeDtypeStruct((heads, n, qk), BF16),
            jax.ShapeDtypeStruct((heads, n, qk), BF16),
            jax.ShapeDtypeStruct((heads, n, MLA_V), BF16),
        ],
        compiler_params=_params("parallel"),
        name="mla_proj",
    )(h, w['g_attn'], w['w_in'], w['g_q_a'], w['g_kv_a'], w['g_k_rope'], tab, w['w_q'],
      w['g_q_nope'], w['g_q_rope'], w['w_kn'], w['w_v'], w['g_k_nope'])


def _mla_expand_call(lat, kr, w, tm):
    n, kv_lora = lat.shape
    heads = w['heads']
    qk = MLA_NOPE + MLA_ROPE
    row = lambda i: (i, 0)
    hrow = lambda i: (0, i, 0)
    return pl.pallas_call(
        functools.partial(_mla_expand_kernel, heads=heads),
        grid=(n // tm,),
        in_specs=[pl.BlockSpec((tm, kv_lora), row), pl.BlockSpec((tm, MLA_ROPE), row),
                  _resident(w['w_kn'].shape), _resident(w['w_v'].shape),
                  _resident(w['g_k_nope'].shape)],
        out_specs=[pl.BlockSpec((heads, tm, qk), hrow), pl.BlockSpec((heads, tm, MLA_V), hrow)],
        out_shape=[jax.ShapeDtypeStruct((heads, n, qk), BF16),
                   jax.ShapeDtypeStruct((heads, n, MLA_V), BF16)],
        compiler_params=_params("parallel"),
        name="mla_expand",
    )(lat, kr, w['w_kn'], w['w_v'], w['g_k_nope'])


def _mla_attn_kernel(q_ref, k_ref, v_ref, o_ref, *, hb, tq):
    i = pl.program_id(2)
    rq = lax.broadcasted_iota(jnp.int32, (tq, tq), 0) // CHUNK
    ck = lax.broadcasted_iota(jnp.int32, (tq, tq), 1) // CHUNK
    diag_bias = jnp.where(ck <= rq, 0.0, NEG).astype(F32)

    for h in range(hb):
        q = q_ref[h]

        def step(j, carry, bias=None):
            m, l, acc = carry
            start = pl.multiple_of(j * tq, tq)
            s = _dot_t(q, k_ref[h, pl.ds(start, tq), :])
            if bias is not None:
                s = s + bias
            m_new = jnp.maximum(m, jnp.max(s, axis=-1, keepdims=True))
            alpha = jnp.exp(m - m_new)
            p = jnp.exp(s - m_new)
            l = alpha * l + jnp.sum(p, axis=-1, keepdims=True)
            acc = alpha * acc + _dot(p.astype(BF16), v_ref[h, pl.ds(start, tq), :])
            return m_new, l, acc

        init = (jnp.full((tq, 1), NEG, F32), jnp.zeros((tq, 1), F32), jnp.zeros((tq, MLA_V), F32))
        carry = lax.fori_loop(0, i, step, init)
        _, l, acc = step(i, carry, diag_bias)
        o_ref[:, h * MLA_V:(h + 1) * MLA_V] = (acc / l).astype(BF16)


def _mla_attn(q, k, v, batch, seq, tq, hb):
    heads, n, qk = q.shape
    nq = seq // tq
    kv_map = lambda b, g, i: (g, b, 0)
    return pl.pallas_call(
        functools.partial(_mla_attn_kernel, hb=hb, tq=tq),
        grid=(batch, heads // hb, nq),
        in_specs=[pl.BlockSpec((hb, tq, qk), lambda b, g, i: (g, b * nq + i, 0)),
                  pl.BlockSpec((hb, seq, qk), kv_map),
                  pl.BlockSpec((hb, seq, MLA_V), kv_map)],
        out_specs=pl.BlockSpec((tq, hb * MLA_V), lambda b, g, i: (b * nq + i, g)),
        out_shape=jax.ShapeDtypeStruct((n, heads * MLA_V), BF16),
        compiler_params=_params("parallel", "parallel", "arbitrary"),
        name="mla_attn",
    )(q, k, v)


def _mla_attn_sample_kernel(q_ref, kc_ref, kn_ref, vc_ref, vn_ref, o_ref):
    q = q_ref[0]
    s_c = _dot_t(q, kc_ref[0])
    s_n = _dot_t(q, kn_ref[0])
    m = jnp.maximum(jnp.max(s_c, axis=-1, keepdims=True), jnp.max(s_n, axis=-1, keepdims=True))
    p_c = jnp.exp(s_c - m)
    p_n = jnp.exp(s_n - m)
    l = jnp.sum(p_c, axis=-1, keepdims=True) + jnp.sum(p_n, axis=-1, keepdims=True)
    acc = _dot(p_c.astype(BF16), vc_ref[0]) + _dot(p_n.astype(BF16), vn_ref[0])
    o_ref[...] = (acc / l).astype(BF16)


def _mla_attn_sample(q, k_cache, k_new, v_cache, v_new, batch, past, t):
    heads, n, qk = q.shape
    q_pos = past + np.arange(t)
    k_pos = np.arange(past + t)
    assert np.all((k_pos // CHUNK)[None, :] <= (q_pos // CHUNK)[:, None])
    m3 = lambda b, h: (h, b, 0)
    return pl.pallas_call(
        _mla_attn_sample_kernel,
        grid=(batch, heads),
        in_specs=[pl.BlockSpec((1, t, qk), m3), pl.BlockSpec((1, past, qk), m3),
                  pl.BlockSpec((1, t, qk), m3), pl.BlockSpec((1, past, MLA_V), m3),
                  pl.BlockSpec((1, t, MLA_V), m3)],
        out_specs=pl.BlockSpec((t, MLA_V), lambda b, h: (b, h)),
        out_shape=jax.ShapeDtypeStruct((n, heads * MLA_V), BF16),
        compiler_params=_params("parallel", "parallel"),
        name="mla_attn_sample",
    )(q, k_cache, k_new, v_cache, v_new)


def _swa_proj_kernel(h_ref, gattn_ref, w_ref, gq_ref, gk_ref, tc_ref, ts1_ref, ts2_ref, gmat_ref,
                     q_ref, kd_ref, vd_ref, kf_ref, vf_ref, *, q_cols, kv_heads, scale):
    hn = _rms(h_ref[...], gattn_ref[...]).astype(BF16)
    tc, ts1, ts2 = tc_ref[...], ts1_ref[...], ts2_ref[...]
    gmat = gmat_ref[...]

    def norm_rope(x, g):
        ss = _dot((x * x).astype(BF16), gmat)
        y = x * lax.rsqrt(ss * (1.0 / SWA_HEAD_DIM) + EPS) * g
        return (y * tc + pltpu.roll(y, LANES - SWA_ROT // 2, axis=1) * ts1
                + pltpu.roll(y, SWA_ROT // 2, axis=1) * ts2)

    gq = gq_ref[...] * scale
    for c in range(q_cols // 2):
        x2 = _dot(hn, w_ref[:, c * 256:(c + 1) * 256])
        for u in range(2):
            col = 2 * c + u
            q_ref[:, col * LANES:(col + 1) * LANES] = norm_rope(
                x2[:, u * LANES:(u + 1) * LANES], gq).astype(BF16)
    base = q_cols * LANES
    gk = gk_ref[...]
    for kh in range(kv_heads):
        x2 = _dot(hn, w_ref[:, base + kh * 256:base + (kh + 1) * 256])
        kd = norm_rope(x2[:, :LANES], gk)
        kd_ref[kh] = kd.astype(BF16)
        kf_ref[kh] = kd[:, :SWA_HEAD_DIM]
        vd_ref[kh] = x2[:, LANES:].astype(BF16)
        vf_ref[kh] = x2[:, LANES:LANES + SWA_HEAD_DIM]


def _swa_proj(h, w, tabs, tm):
    n, d = h.shape
    q_heads, kv_heads = w['q_heads'], w['kv_heads']
    q_cols = q_heads * SWA_HEAD_DIM // LANES
    nt = tabs[0].shape[0] // tm
    row = lambda i: (i, 0)
    hrow = lambda i: (0, i, 0)
    tspec = pl.BlockSpec((tm, LANES), lambda i: (i % nt, 0))
    kern = functools.partial(_swa_proj_kernel, q_cols=q_cols, kv_heads=kv_heads,
                             scale=SWA_HEAD_DIM ** -0.5)
    return pl.pallas_call(
        kern,
        grid=(n // tm,),
        in_specs=[pl.BlockSpec((tm, d), row), _resident(w['g_attn'].shape),
                  _resident(w['w_qkv'].shape), _resident(w['g_q'].shape), _resident(w['g_k'].shape),
                  tspec, tspec, tspec, _resident(w['gmat'].shape)],
        out_specs=[pl.BlockSpec((tm, q_cols * LANES), row),
                   pl.BlockSpec((kv_heads, tm, LANES), hrow),
                   pl.BlockSpec((kv_heads, tm, LANES), hrow),
                   pl.BlockSpec((kv_heads, tm, SWA_HEAD_DIM), hrow),
                   pl.BlockSpec((kv_heads, tm, SWA_HEAD_DIM), hrow)],
        out_shape=[jax.ShapeDtypeStruct((n, q_cols * LANES), BF16),
                   jax.ShapeDtypeStruct((kv_heads, n, LANES), BF16),
                   jax.ShapeDtypeStruct((kv_heads, n, LANES), BF16),
                   jax.ShapeDtypeStruct((kv_heads, n, SWA_HEAD_DIM), F32),
                   jax.ShapeDtypeStruct((kv_heads, n, SWA_HEAD_DIM), F32)],
        compiler_params=_params("parallel"),
        name="swa_proj",
    )(h, w['g_attn'], w['w_qkv'], w['g_q'], w['g_k'], *tabs, w['gmat'])


def _swa_attn_kernel(sink_ref, q_ref, kp_ref, kc_ref, vp_ref, vc_ref, o_ref, *,
                     kv_heads, group, banded):
    tq = q_ref.shape[0]
    n_p, n_c = kp_ref.shape[1], kc_ref.shape[1]
    lane = lax.broadcasted_iota(jnp.int32, (1, LANES), 1)
    lo = lane < SWA_HEAD_DIM
    bias_p = bias_c = None
    if banded:
        win = WINDOW // CHUNK
        rq = lax.broadcasted_iota(jnp.int32, (tq, n_p), 0) // CHUNK
        cp = lax.broadcasted_iota(jnp.int32, (tq, n_p), 1) // CHUNK - n_p // CHUNK
        ok_p = (cp >= rq - win) & (pl.program_id(1) > 0)
        bias_p = jnp.where(ok_p, 0.0, NEG).astype(F32)
        rq = lax.broadcasted_iota(jnp.int32, (tq, n_c), 0) // CHUNK
        cc = lax.broadcasted_iota(jnp.int32, (tq, n_c), 1) // CHUNK
        bias_c = jnp.where((cc <= rq) & (cc >= rq - win), 0.0, NEG).astype(F32)
    zero = jnp.zeros((), BF16)
    for kh in range(kv_heads):
        kp, kc = kp_ref[kh], kc_ref[kh]
        vp, vc = vp_ref[kh], vc_ref[kh]
        vps = (jnp.where(lo, vp, zero), jnp.where(lo, zero, vp))
        vcs = (jnp.where(lo, vc, zero), jnp.where(lo, zero, vc))
        for c in range(group // 2):
            col = kh * (group // 2) + c
            qcol = q_ref[:, col * LANES:(col + 1) * LANES]
            out = None
            for u in range(2):
                qm = jnp.where(lo, qcol, zero) if u == 0 else jnp.where(lo, zero, qcol)
                s_p = _dot_t(qm, kp)
                s_c = _dot_t(qm, kc)
                if banded:
                    s_p = s_p + bias_p
                    s_c = s_c + bias_c
                sink = sink_ref[2 * col + u]
                m = jnp.maximum(jnp.maximum(jnp.max(s_p, axis=-1, keepdims=True),
                                            jnp.max(s_c, axis=-1, keepdims=True)), sink)
                e_p = jnp.exp(s_p - m)
                e_c = jnp.exp(s_c - m)
                den = (jnp.sum(e_p, axis=-1, keepdims=True) + jnp.sum(e_c, axis=-1, keepdims=True)
                       + jnp.exp(sink - m))
                o = (_dot(e_p.astype(BF16), vps[u]) + _dot(e_c.astype(BF16), vcs[u])) / den
                out = o if out is None else out + o
            o_ref[:, col * LANES:(col + 1) * LANES] = out.astype(BF16)


def _swa_attn(sinks, q, kd, vd, batch, seq, tq):
    n, dq = q.shape
    kv_heads = kd.shape[0]
    group = (dq // SWA_HEAD_DIM) // kv_heads
    nq = seq // tq
    per = tq // WINDOW
    prev = lambda b, i: (0, jnp.maximum(i * per - 1, 0) + b * nq * per, 0)
    cur = lambda b, i: (0, b * nq + i, 0)
    return pl.pallas_call(
        functools.partial(_swa_attn_kernel, kv_heads=kv_heads, group=group, banded=True),
        grid=(batch, nq),
        in_specs=[pl.BlockSpec(memory_space=pltpu.SMEM),
                  pl.BlockSpec((tq, dq), lambda b, i: (b * nq + i, 0)),
                  pl.BlockSpec((kv_heads, WINDOW, LANES), prev),
                  pl.BlockSpec((kv_heads, tq, LANES), cur),
                  pl.BlockSpec((kv_heads, WINDOW, LANES), prev),
                  pl.BlockSpec((kv_heads, tq, LANES), cur)],
        out_specs=pl.BlockSpec((tq, dq), lambda b, i: (b * nq + i, 0)),
        out_shape=jax.ShapeDtypeStruct((n, dq), BF16),
        compiler_params=_params("parallel", "arbitrary"),
        name="swa_attn",
    )(sinks, q, kd, kd, vd, vd)


def _swa_attn_sample(sinks, q, kd_cache, kd_new, vd_cache, vd_new, batch, keep, t):
    n, dq = q.shape
    kv_heads = kd_new.shape[0]
    group = (dq // SWA_HEAD_DIM) // kv_heads
    blk = lambda b: (0, b, 0)
    return pl.pallas_call(
        functools.partial(_swa_attn_kernel, kv_heads=kv_heads, group=group, banded=False),
        grid=(batch,),
        in_specs=[pl.BlockSpec(memory_space=pltpu.SMEM),
                  pl.BlockSpec((t, dq), lambda b: (b, 0)),
                  pl.BlockSpec((kv_heads, keep, LANES), blk),
                  pl.BlockSpec((kv_heads, t, LANES), blk),
                  pl.BlockSpec((kv_heads, keep, LANES), blk),
                  pl.BlockSpec((kv_heads, t, LANES), blk)],
        out_specs=pl.BlockSpec((t, dq), lambda b: (b, 0)),
        out_shape=jax.ShapeDtypeStruct((n, dq), BF16),
        compiler_params=_params("parallel"),
        name="swa_attn_sample",
    )(sinks, q, kd_cache, kd_new, vd_cache, vd_new)


def _oproj_kernel(h_ref, o_ref, w_ref, out_ref):
    out_ref[...] = h_ref[...] + _dot(o_ref[...], w_ref[...])


def _oproj(h, o, w, tm):
    n, d = h.shape
    row = lambda i: (i, 0)
    return pl.pallas_call(
        _oproj_kernel,
        grid=(n // tm,),
        in_specs=[pl.BlockSpec((tm, d), row), pl.BlockSpec((tm, o.shape[1]), row), _resident(w.shape)],
        out_specs=pl.BlockSpec((tm, d), row),
        out_shape=jax.ShapeDtypeStruct((n, d), F32),
        compiler_params=_params("parallel"),
        name="attn_out_proj",
    )(h, o, w)


def _ffn_kernel(h_ref, g_ref, wg_ref, wu_ref, wd_ref, out_ref, hn_ref):
    @pl.when(pl.program_id(1) == 0)
    def _():
        x = h_ref[...]
        hn_ref[...] = _rms(x, g_ref[...]).astype(BF16)
        out_ref[...] = x

    hn = hn_ref[...]
    a = _dot(hn, wg_ref[...])
    b = _dot(hn, wu_ref[...])
    act = (a * jax.nn.sigmoid(a) * b).astype(BF16)
    out_ref[...] += _dot(act, wd_ref[...])


def _ffn(h, g, wg, wu, wd, tm, tf):
    n, d = h.shape
    f = wg.shape[1]
    row = lambda i, j: (i, 0)
    return pl.pallas_call(
        _ffn_kernel,
        grid=(n // tm, f // tf),
        in_specs=[pl.BlockSpec((tm, d), row), pl.BlockSpec((1, d), lambda i, j: (0, 0)),
                  pl.BlockSpec((d, tf), lambda i, j: (0, j)),
                  pl.BlockSpec((d, tf), lambda i, j: (0, j)),
                  pl.BlockSpec((tf, d), lambda i, j: (j, 0))],
        out_specs=pl.BlockSpec((tm, d), row),
        out_shape=jax.ShapeDtypeStruct((n, d), F32),
        scratch_shapes=[pltpu.VMEM((tm, d), BF16)],
        compiler_params=_params("parallel", "arbitrary"),
        name="swiglu_ffn",
    )(h, g, wg, wu, wd)


def _ple_kernel(h_ref, g_ref, wgate_ref, p_ref, wproj_ref, gfin_ref, out_ref, *, final):
    x = h_ref[...]
    gate = jax.nn.sigmoid(_dot(_rms(x, g_ref[...]).astype(BF16), wgate_ref[...]))
    y = x + gate * _dot(p_ref[...].astype(BF16), wproj_ref[...])
    if final:
        y = _rms(y, gfin_ref[...])
    out_ref[...] = y


def _ple(h, g, wgate, p, wproj, gfin, final, tm):
    n, d = h.shape
    row = lambda i: (i, 0)
    return pl.pallas_call(
        functools.partial(_ple_kernel, final=final),
        grid=(n // tm,),
        in_specs=[pl.BlockSpec((tm, d), row), _resident(g.shape), _resident(wgate.shape),
                  pl.BlockSpec((tm, p.shape[1]), row), _resident(wproj.shape), _resident(gfin.shape)],
        out_specs=pl.BlockSpec((tm, d), row),
        out_shape=jax.ShapeDtypeStruct((n, d), F32),
        compiler_params=_params("parallel"),
        name="ple_embed",
    )(h, g, wgate, p, wproj, gfin)


def _angles(pos, r, theta):
    inv = jnp.power(jnp.float32(theta), -jnp.arange(0, r, 2, dtype=jnp.float32) / r)
    ang = pos.astype(jnp.float32)[:, None] * inv[None, :]
    return jnp.cos(ang), jnp.sin(ang)


def _mla_table(pos):
    cos, sin = _angles(pos, MLA_ROPE, MLA_THETA)
    return jnp.concatenate([cos, cos, -sin, sin], axis=-1)


def _swa_tables(pos):
    cos, sin = _angles(pos, SWA_ROT, SWA_THETA)
    s = cos.shape[0]
    rest = SWA_HEAD_DIM - SWA_ROT
    one = jnp.ones((s, rest), F32)
    zero = jnp.zeros((s, rest), F32)
    zh = jnp.zeros_like(sin)
    tc = jnp.concatenate([cos, cos, one], axis=-1)
    ts1 = jnp.concatenate([-sin, zh, zero], axis=-1)
    ts2 = jnp.concatenate([zh, sin, zero], axis=-1)
    return tuple(jnp.concatenate([t, t], axis=-1) for t in (tc, ts1, ts2))


def _tile_rows(tab, rows):
    return tab if tab.shape[0] >= rows else jnp.tile(tab, (rows // tab.shape[0], 1))


def _swap_halves(x):
    half = x.shape[-1] // 2
    return jnp.concatenate([x[..., half:], x[..., :half]], axis=-1)


def _prep_mla(j, g_attn_norm_i, w_mla_in, g_mla_q_a, w_mla_q_up, g_mla_kv_a, w_mla_kv_up, g_mla_q_nope,
              g_mla_q_rope, g_mla_k_nope, g_mla_k_rope, w_mla_out):
    q_lora, kv_lora = g_mla_q_a.shape[1], g_mla_kv_a.shape[1]
    qk = MLA_NOPE + MLA_ROPE
    heads = w_mla_q_up.shape[2] // qk
    w_in = w_mla_in[j]
    k_r = w_in[:, q_lora + kv_lora:]
    w_in = jnp.concatenate([w_in, _swap_halves(k_r)], axis=-1)
    wq = w_mla_q_up[j].reshape(q_lora, heads, qk)
    wq = jnp.concatenate([wq, _swap_halves(wq[:, :, MLA_NOPE:])], axis=-1).reshape(q_lora, heads * 256)
    wkv = w_mla_kv_up[j].reshape(kv_lora, heads, MLA_NOPE + MLA_V)
    dup = lambda g: jnp.concatenate([g, _swap_halves(g)], axis=-1)[None, :]
    return dict(
        heads=heads,
        g_attn=g_attn_norm_i[None, :],
        w_in=w_in.astype(BF16),
        g_q_a=g_mla_q_a[j][None, :], g_kv_a=g_mla_kv_a[j][None, :],
        g_k_rope=dup(g_mla_k_rope[j]), g_q_rope=dup(g_mla_q_rope[j]),
        g_q_nope=g_mla_q_nope[j][None, :], g_k_nope=g_mla_k_nope[j][None, :],
        w_q=wq.astype(BF16),
        w_kn=wkv[:, :, :MLA_NOPE].reshape(kv_lora, heads * MLA_NOPE).astype(BF16),
        w_v=wkv[:, :, MLA_NOPE:].reshape(kv_lora, heads * MLA_V).astype(BF16),
        w_out=w_mla_out[j].astype(BF16),
    )


def _prep_swa(j, g_attn_norm_i, w_swa_qkv, g_swa_q, g_swa_k, swa_sinks, w_swa_out):
    q_heads = swa_sinks.shape[1]
    w = w_swa_qkv[j]
    d = w.shape[0]
    nq = q_heads * SWA_HEAD_DIM
    kv_heads = (w.shape[1] - nq) // (2 * SWA_HEAD_DIM)
    nk = kv_heads * SWA_HEAD_DIM
    wk = w[:, nq:nq + nk].reshape(d, kv_heads, SWA_HEAD_DIM)
    wv = w[:, nq + nk:].reshape(d, kv_heads, SWA_HEAD_DIM)
    wkv = jnp.concatenate([wk, wk, wv, wv], axis=-1).reshape(d, kv_heads * 4 * SWA_HEAD_DIM)
    gmat = np.kron(np.eye(LANES // SWA_HEAD_DIM), np.ones((SWA_HEAD_DIM, SWA_HEAD_DIM)))
    dup = lambda g: jnp.concatenate([g, g], axis=-1)[None, :]
    return dict(
        q_heads=q_heads, kv_heads=kv_heads,
        g_attn=g_attn_norm_i[None, :],
        w_qkv=jnp.concatenate([w[:, :nq], wkv], axis=-1).astype(BF16),
        g_q=dup(g_swa_q[j]), g_k=dup(g_swa_k[j]),
        gmat=jnp.asarray(gmat, BF16),
        sinks=swa_sinks[j],
        w_out=w_swa_out[j].astype(BF16),
    )


def _kv_dup(t):
    b, l, kv, d = t.shape
    t = jnp.transpose(t, (2, 0, 1, 3)).reshape(kv, b * l, d)
    return jnp.concatenate([t, t], axis=-1).astype(BF16)


def _trunk(x, p, pos, caches, layers, ffn_w, ple_w, g_final, tm, tf):
    batch, seq, d = x.shape
    n = batch * seq
    h = x.reshape(n, d)
    rows = max(tm, seq)
    mla_tab = _tile_rows(_mla_table(pos), rows)
    swa_tabs = tuple(_tile_rows(t, rows) for t in _swa_tables(pos))
    depth = len(layers)
    lats, krs, sks, svs = [], [], [], []
    for i, (kind, w) in enumerate(layers):
        j = i // 2
        if kind == 'mla':
            lat, kr, q, k, v = _mla_proj(h, w, mla_tab, tm)
            heads = w['heads']
            if caches is None:
                o = _mla_attn(q, k, v, batch, seq, tq=256, hb=2)
            else:
                lat_c, kr_c = caches[0][j], caches[1][j]
                past = lat_c.shape[1]
                k_c, v_c = _mla_expand_call(lat_c.reshape(batch * past, -1),
                                            kr_c.reshape(batch * past, -1), w, tm=512)
                o = _mla_attn_sample(q, k_c, k, v_c, v, batch, past, seq)
            lats.append(lat.reshape(batch, seq, -1))
            krs.append(kr.reshape(batch, seq, -1))
        else:
            q, kd, vd, kf, vf = _swa_proj(h, w, swa_tabs, tm)
            kv_heads = w['kv_heads']
            unhead = lambda t: jnp.transpose(t.reshape(kv_heads, batch, seq, SWA_HEAD_DIM), (1, 2, 0, 3))
            if caches is None:
                o = _swa_attn(w['sinks'], q, kd, vd, batch, seq, tq=256)
                keep = min(WINDOW, seq)
                sks.append(unhead(kf)[:, seq - keep:])
                svs.append(unhead(vf)[:, seq - keep:])
            else:
                ck, cv = caches[2][j], caches[3][j]
                keep = ck.shape[1]
                o = _swa_attn_sample(w['sinks'], q, _kv_dup(ck), kd, _kv_dup(cv), vd, batch, keep, seq)
                sks.append(jnp.concatenate([ck, unhead(kf)], axis=1)[:, -keep:])
                svs.append(jnp.concatenate([cv, unhead(vf)], axis=1)[:, -keep:])
        h = _oproj(h, o, w['w_out'], tm)
        g_ffn, wg, wu, wd = ffn_w[i]
        h = _ffn(h, g_ffn, wg, wu, wd, tm, tf)
        g_ple, wgate, wproj = ple_w[i]
        h = _ple(h, g_ple, wgate, p[i].reshape(n, -1), wproj, g_final, i == depth - 1, tm)
    return (h.reshape(batch, seq, d), jnp.stack(lats), jnp.stack(krs), jnp.stack(sks), jnp.stack(svs))


def kernel(x_prompt, x_sample, p_prompt, p_sample, cache_mla_latent, cache_mla_krope, state_swa_k, state_swa_v, g_attn_norm, w_mla_in, g_mla_q_a, w_mla_q_up, g_mla_kv_a, w_mla_kv_up, g_mla_q_nope, g_mla_q_rope, g_mla_k_nope, g_mla_k_rope, w_mla_out, w_swa_qkv, g_swa_q, g_swa_k, swa_sinks, w_swa_out, g_ffn_norm, w_ffn_gate, w_ffn_up, w_ffn_down, g_ple_norm, w_ple_gate, w_ple_proj, g_final):
    depth = g_attn_norm.shape[0]
    layers = []
    for i in range(depth):
        j = i // 2
        if i % 2 == 0:
            layers.append(('mla', _prep_mla(j, g_attn_norm[i], w_mla_in, g_mla_q_a, w_mla_q_up,
                                            g_mla_kv_a, w_mla_kv_up, g_mla_q_nope, g_mla_q_rope,
                                            g_mla_k_nope, g_mla_k_rope, w_mla_out)))
        else:
            layers.append(('swa', _prep_swa(j, g_attn_norm[i], w_swa_qkv, g_swa_q, g_swa_k,
                                            swa_sinks, w_swa_out)))
    ffn_w = [(g_ffn_norm[i][None, :], w_ffn_gate[i].astype(BF16), w_ffn_up[i].astype(BF16),
              w_ffn_down[i].astype(BF16)) for i in range(depth)]
    ple_w = [(g_ple_norm[i][None, :], w_ple_gate[i].astype(BF16), w_ple_proj[i].astype(BF16))
             for i in range(depth)]
    gfin = g_final[None, :]

    seq = x_prompt.shape[1]
    t = x_sample.shape[1]
    past = cache_mla_latent.shape[2]
    pos_p = jnp.arange(seq, dtype=jnp.int32)
    pos_s = past + jnp.arange(t, dtype=jnp.int32)
    y_p, lat_p, kr_p, sk_p, sv_p = _trunk(x_prompt, p_prompt, pos_p, None, layers, ffn_w, ple_w,
                                          gfin, tm=512, tf=512)
    caches = (cache_mla_latent, cache_mla_krope, state_swa_k, state_swa_v)
    n_s = x_sample.shape[0] * t
    y_s, lat_s, kr_s, sk_s, sv_s = _trunk(x_sample, p_sample, pos_s, caches, layers, ffn_w, ple_w,
                                          gfin, tm=n_s, tf=512)
    return (y_p, y_s, lat_p, kr_p, sk_p, sv_p, lat_s, kr_s, sk_s, sv_s)
```

```python
import functools

import numpy as np
import jax
import jax.numpy as jnp
from jax import lax
from jax.experimental import pallas as pl
from jax.experimental.pallas import tpu as pltpu

F32 = jnp.float32
BF16 = jnp.bfloat16

EPS = 1e-6
CHUNK = 64
WINDOW = 128
MLA_THETA = 10000.0
SWA_THETA = 500000.0
MLA_NOPE = 128
MLA_ROPE = 64
MLA_V = 128
SWA_HEAD_DIM = 64
SWA_ROT = SWA_HEAD_DIM // 4

LANES = 128
VMEM_LIMIT = 56 * 1024 * 1024
NEG = -1e30
LOG2E = 1.4426950408889634


def _params(*sem):
    return pltpu.CompilerParams(dimension_semantics=sem, vmem_limit_bytes=VMEM_LIMIT)


def _resident(shape):
    zeros = (0,) * len(shape)
    return pl.BlockSpec(shape, lambda *_: zeros, pipeline_mode=pl.Buffered(1))


def _rms(x, g):
    return x * lax.rsqrt(jnp.mean(x * x, axis=-1, keepdims=True) + EPS) * g


def _dot(a, b):
    return jnp.dot(a, b, preferred_element_type=F32)


def _dot_t(a, b):
    return lax.dot_general(a, b, (((1,), (1,)), ((), ())), preferred_element_type=F32)


def _rope_dup(x, g, tab):
    ss = jnp.sum(x * x, axis=-1, keepdims=True)
    y = x * lax.rsqrt(ss * (1.0 / LANES) + EPS) * g
    t = y * tab
    return t + pltpu.roll(t, LANES // 2, axis=1)


def _mla_expand(latb, kr, wkn_ref, wv_ref, gkn, k_ref, v_ref, heads):
    krb = kr[:, :MLA_ROPE].astype(BF16)
    for p in range(heads // 2):
        kn2 = _dot(latb, wkn_ref[:, p * 256:(p + 1) * 256])
        v2 = _dot(latb, wv_ref[:, p * 256:(p + 1) * 256])
        for u in range(2):
            h = 2 * p + u
            k_ref[h, :, 0:MLA_NOPE] = _rms(kn2[:, u * 128:(u + 1) * 128], gkn).astype(BF16)
            k_ref[h, :, MLA_NOPE:MLA_NOPE + MLA_ROPE] = krb
            v_ref[h] = v2[:, u * 128:(u + 1) * 128].astype(BF16)


def _mla_proj_kernel(h_ref, gattn_ref, win_ref, gqa_ref, gkva_ref, gkr_ref, tab_ref, wq_ref,
                     gqn_ref, gqr_ref, wkn_ref, wv_ref, gkn_ref,
                     lat_ref, kr_ref, q_ref, k_ref, v_ref, *, heads, q_lora, kv_lora, scale):
    hn = _rms(h_ref[...], gattn_ref[...]).astype(BF16)
    a = _dot(hn, win_ref[...])
    c_q = _rms(a[:, :q_lora], gqa_ref[...]).astype(BF16)
    lat = _rms(a[:, q_lora:q_lora + kv_lora], gkva_ref[...])
    lat_ref[...] = lat
    tab = tab_ref[...]
    kr = _rope_dup(a[:, q_lora + kv_lora:], gkr_ref[...], tab)
    kr_ref[...] = kr[:, :MLA_ROPE]
    _mla_expand(lat.astype(BF16), kr, wkn_ref, wv_ref, gkn_ref[...], k_ref, v_ref, heads)
    gqn = gqn_ref[...] * scale
    gqr = gqr_ref[...] * scale
    for h in range(heads):
        q = _dot(c_q, wq_ref[:, h * 256:(h + 1) * 256])
        q_ref[h, :, 0:MLA_NOPE] = _rms(q[:, :128], gqn).astype(BF16)
        qr = _rope_dup(q[:, 128:], gqr, tab)
        q_ref[h, :, MLA_NOPE:MLA_NOPE + MLA_ROPE] = qr[:, :MLA_ROPE].astype(BF16)


def _mla_expand_kernel(lat_ref, kr_ref, wkn_ref, wv_ref, gkn_ref, k_ref, v_ref, *, heads):
    _mla_expand(lat_ref[...].astype(BF16), kr_ref[...], wkn_ref, wv_ref, gkn_ref[...],
                k_ref, v_ref, heads)


def _mla_proj(h, w, tab, tm):
    n, d = h.shape
    heads = w['heads']
    q_lora, kv_lora = w['g_q_a'].shape[1], w['g_kv_a'].shape[1]
    qk = MLA_NOPE + MLA_ROPE
    nt = tab.shape[0] // tm
    row = lambda i: (i, 0)
    hrow = lambda i: (0, i, 0)
    kern = functools.partial(_mla_proj_kernel, heads=heads, q_lora=q_lora, kv_lora=kv_lora,
                             scale=qk ** -0.5 * LOG2E)
    return pl.pallas_call(
        kern,
        grid=(n // tm,),
        in_specs=[
            pl.BlockSpec((tm, d), row),
            _resident(w['g_attn'].shape), _resident(w['w_in'].shape),
            _resident(w['g_q_a'].shape), _resident(w['g_kv_a'].shape), _resident(w['g_k_rope'].shape),
            pl.BlockSpec((tm, LANES), lambda i: (i % nt, 0)),
            _resident(w['w_q'].shape), _resident(w['g_q_nope'].shape), _resident(w['g_q_rope'].shape),
            _resident(w['w_kn'].shape), _resident(w['w_v'].shape), _resident(w['g_k_nope'].shape),
        ],
        out_specs=[
            pl.BlockSpec((tm, kv_lora), row),
            pl.BlockSpec((tm, MLA_ROPE), row),
            pl.BlockSpec((heads, tm, qk), hrow),
            pl.BlockSpec((heads, tm, qk), hrow),
            pl.BlockSpec((heads, tm, MLA_V), hrow),
        ],
        out_shape=[
            jax.ShapeDtypeStruct((n, kv_lora), F32),
            jax.ShapeDtypeStruct((n, MLA_ROPE), F32),
            jax.ShapeDtypeStruct((heads, n, qk), BF16),
            jax.ShapeDtypeStruct((heads, n, qk), BF16),
            jax.ShapeDtypeStruct((heads, n, MLA_V), BF16),
        ],
        compiler_params=_params("parallel"),
        name="mla_proj",
    )(h, w['g_attn'], w['w_in'], w['g_q_a'], w['g_kv_a'], w['g_k_rope'], tab, w['w_q'],
      w['g_q_nope'], w['g_q_rope'], w['w_kn'], w['w_v'], w['g_k_nope'])


def _mla_expand_call(lat, kr, w, tm):
    n, kv_lora = lat.shape
    heads = w['heads']
    qk = MLA_NOPE + MLA_ROPE
    row = lambda i: (i, 0)
    hrow = lambda i: (0, i, 0)
    return pl.pallas_call(
        functools.partial(_mla_expand_kernel, heads=heads),
        grid=(n // tm,),
        in_specs=[pl.BlockSpec((tm, kv_lora), row), pl.BlockSpec((tm, MLA_ROPE), row),
                  _resident(w['w_kn'].shape), _resident(w['w_v'].shape),
                  _resident(w['g_k_nope'].shape)],
        out_specs=[pl.BlockSpec((heads, tm, qk), hrow), pl.BlockSpec((heads, tm, MLA_V), hrow)],
        out_shape=[jax.ShapeDtypeStruct((heads, n, qk), BF16),
                   jax.ShapeDtypeStruct((heads, n, MLA_V), BF16)],
        compiler_params=_params("parallel"),
        name="mla_expand",
    )(lat, kr, w['w_kn'], w['w_v'], w['g_k_nope'])


def _mla_attn_kernel(q_ref, k_ref, v_ref, o_ref, *, hb, tq):
    i = pl.program_id(2)
    kc = lax.broadcasted_iota(jnp.int32, (tq, tq), 0) // CHUNK
    qc = lax.broadcasted_iota(jnp.int32, (tq, tq), 1) // CHUNK
    diag_bias = jnp.where(kc <= qc, 0.0, NEG).astype(F32)
    qs = [q_ref[h] for h in range(hb)]

    def step(j, carry, bias=None):
        start = pl.multiple_of(j * tq, tq)
        ss = [_dot_t(k_ref[h, pl.ds(start, tq), :], qs[h]) for h in range(hb)]
        ps, stats = [], []
        for h in range(hb):
            m, l, _ = carry[h]
            s = ss[h] if bias is None else ss[h] + bias
            m_new = jnp.maximum(m, jnp.max(s, axis=0, keepdims=True))
            alpha = jnp.exp2(m - m_new)
            p = jnp.exp2(s - m_new)
            stats.append((m_new, alpha * l + jnp.sum(p, axis=0, keepdims=True), alpha))
            ps.append(p.astype(BF16))
        out = []
        for h in range(hb):
            pv = lax.dot_general(v_ref[h, pl.ds(start, tq), :], ps[h],
                                 (((0,), (0,)), ((), ())), preferred_element_type=F32)
            m_new, l, alpha = stats[h]
            out.append((m_new, l, alpha * carry[h][2] + pv))
        return tuple(out)

    init = tuple((jnp.full((1, tq), NEG, F32), jnp.zeros((1, tq), F32), jnp.zeros((MLA_V, tq), F32))
                 for _ in range(hb))
    carry = lax.fori_loop(0, i, step, init)
    carry = step(i, carry, diag_bias)
    for h in range(hb):
        _, l, acc = carry[h]
        o_ref[:, h * MLA_V:(h + 1) * MLA_V] = (acc / l).T.astype(BF16)


def _mla_attn(q, k, v, batch, seq, tq, hb):
    heads, n, qk = q.shape
    nq = seq // tq
    kv_map = lambda b, g, i: (g, b, 0)
    return pl.pallas_call(
        functools.partial(_mla_attn_kernel, hb=hb, tq=tq),
        grid=(batch, heads // hb, nq),
        in_specs=[pl.BlockSpec((hb, tq, qk), lambda b, g, i: (g, b * nq + i, 0)),
                  pl.BlockSpec((hb, seq, qk), kv_map),
                  pl.BlockSpec((hb, seq, MLA_V), kv_map)],
        out_specs=pl.BlockSpec((tq, hb * MLA_V), lambda b, g, i: (b * nq + i, g)),
        out_shape=jax.ShapeDtypeStruct((n, heads * MLA_V), BF16),
        compiler_params=_params("parallel", "parallel", "arbitrary"),
        name="mla_attn",
    )(q, k, v)


def _mla_attn_sample_kernel(q_ref, kc_ref, kn_ref, vc_ref, vn_ref, o_ref):
    q = q_ref[0]
    s_c = _dot_t(q, kc_ref[0])
    s_n = _dot_t(q, kn_ref[0])
    m = jnp.maximum(jnp.max(s_c, axis=-1, keepdims=True), jnp.max(s_n, axis=-1, keepdims=True))
    p_c = jnp.exp2(s_c - m)
    p_n = jnp.exp2(s_n - m)
    l = jnp.sum(p_c, axis=-1, keepdims=True) + jnp.sum(p_n, axis=-1, keepdims=True)
    acc = _dot(p_c.astype(BF16), vc_ref[0]) + _dot(p_n.astype(BF16), vn_ref[0])
    o_ref[...] = (acc / l).astype(BF16)


def _mla_attn_sample(q, k_cache, k_new, v_cache, v_new, batch, past, t):
    heads, n, qk = q.shape
    q_pos = past + np.arange(t)
    k_pos = np.arange(past + t)
    assert np.all((k_pos // CHUNK)[None, :] <= (q_pos // CHUNK)[:, None])
    m3 = lambda b, h: (h, b, 0)
    return pl.pallas_call(
        _mla_attn_sample_kernel,
        grid=(batch, heads),
        in_specs=[pl.BlockSpec((1, t, qk), m3), pl.BlockSpec((1, past, qk), m3),
                  pl.BlockSpec((1, t, qk), m3), pl.BlockSpec((1, past, MLA_V), m3),
                  pl.BlockSpec((1, t, MLA_V), m3)],
        out_specs=pl.BlockSpec((t, MLA_V), lambda b, h: (b, h)),
        out_shape=jax.ShapeDtypeStruct((n, heads * MLA_V), BF16),
        compiler_params=_params("parallel", "parallel"),
        name="mla_attn_sample",
    )(q, k_cache, k_new, v_cache, v_new)


def _swa_proj_kernel(h_ref, gattn_ref, w_ref, gq_ref, gk_ref, tc_ref, ts1_ref, ts2_ref, gmat_ref,
                     q_ref, kd_ref, vd_ref, kf_ref, vf_ref, *, q_cols, kv_heads, scale):
    hn = _rms(h_ref[...], gattn_ref[...]).astype(BF16)
    tc, ts1, ts2 = tc_ref[...], ts1_ref[...], ts2_ref[...]
    gmat = gmat_ref[...]

    def norm_rope(x, g):
        ss = _dot((x * x).astype(BF16), gmat)
        y = x * lax.rsqrt(ss * (1.0 / SWA_HEAD_DIM) + EPS) * g
        return (y * tc + pltpu.roll(y, LANES - SWA_ROT // 2, axis=1) * ts1
                + pltpu.roll(y, SWA_ROT // 2, axis=1) * ts2)

    gq = gq_ref[...] * scale
    for c in range(q_cols // 2):
        x2 = _dot(hn, w_ref[:, c * 256:(c + 1) * 256])
        for u in range(2):
            col = 2 * c + u
            q_ref[:, col * LANES:(col + 1) * LANES] = norm_rope(
                x2[:, u * LANES:(u + 1) * LANES], gq).astype(BF16)
    base = q_cols * LANES
    gk = gk_ref[...]
    for kh in range(kv_heads):
        x2 = _dot(hn, w_ref[:, base + kh * 256:base + (kh + 1) * 256])
        kd = norm_rope(x2[:, :LANES], gk)
        kd_ref[kh] = kd.astype(BF16)
        kf_ref[kh] = kd[:, :SWA_HEAD_DIM]
        vd_ref[kh] = x2[:, LANES:].astype(BF16)
        vf_ref[kh] = x2[:, LANES:LANES + SWA_HEAD_DIM]


def _swa_proj(h, w, tabs, tm):
    n, d = h.shape
    q_heads, kv_heads = w['q_heads'], w['kv_heads']
    q_cols = q_heads * SWA_HEAD_DIM // LANES
    nt = tabs[0].shape[0] // tm
    row = lambda i: (i, 0)
    hrow = lambda i: (0, i, 0)
    tspec = pl.BlockSpec((tm, LANES), lambda i: (i % nt, 0))
    kern = functools.partial(_swa_proj_kernel, q_cols=q_cols, kv_heads=kv_heads,
                             scale=SWA_HEAD_DIM ** -0.5)
    return pl.pallas_call(
        kern,
        grid=(n // tm,),
        in_specs=[pl.BlockSpec((tm, d), row), _resident(w['g_attn'].shape),
                  _resident(w['w_qkv'].shape), _resident(w['g_q'].shape), _resident(w['g_k'].shape),
                  tspec, tspec, tspec, _resident(w['gmat'].shape)],
        out_specs=[pl.BlockSpec((tm, q_cols * LANES), row),
                   pl.BlockSpec((kv_heads, tm, LANES), hrow),
                   pl.BlockSpec((kv_heads, tm, LANES), hrow),
                   pl.BlockSpec((kv_heads, tm, SWA_HEAD_DIM), hrow),
                   pl.BlockSpec((kv_heads, tm, SWA_HEAD_DIM), hrow)],
        out_shape=[jax.ShapeDtypeStruct((n, q_cols * LANES), BF16),
                   jax.ShapeDtypeStruct((kv_heads, n, LANES), BF16),
                   jax.ShapeDtypeStruct((kv_heads, n, LANES), BF16),
                   jax.ShapeDtypeStruct((kv_heads, n, SWA_HEAD_DIM), F32),
                   jax.ShapeDtypeStruct((kv_heads, n, SWA_HEAD_DIM), F32)],
        compiler_params=_params("parallel"),
        name="swa_proj",
    )(h, w['g_attn'], w['w_qkv'], w['g_q'], w['g_k'], *tabs, w['gmat'])


def _swa_attn_kernel(sink_ref, q_ref, kp_ref, kc_ref, vp_ref, vc_ref, o_ref, *,
                     kv_heads, group, banded):
    tq = q_ref.shape[0]
    n_p, n_c = kp_ref.shape[1], kc_ref.shape[1]
    lane = lax.broadcasted_iota(jnp.int32, (1, LANES), 1)
    lo = lane < SWA_HEAD_DIM
    bias_p = bias_c = None
    if banded:
        win = WINDOW // CHUNK
        rq = lax.broadcasted_iota(jnp.int32, (tq, n_p), 0) // CHUNK
        cp = lax.broadcasted_iota(jnp.int32, (tq, n_p), 1) // CHUNK - n_p // CHUNK
        ok_p = (cp >= rq - win) & (pl.program_id(1) > 0)
        bias_p = jnp.where(ok_p, 0.0, NEG).astype(F32)
        rq = lax.broadcasted_iota(jnp.int32, (tq, n_c), 0) // CHUNK
        cc = lax.broadcasted_iota(jnp.int32, (tq, n_c), 1) // CHUNK
        bias_c = jnp.where((cc <= rq) & (cc >= rq - win), 0.0, NEG).astype(F32)
    zero = jnp.zeros((), BF16)
    for kh in range(kv_heads):
        kp, kc = kp_ref[kh], kc_ref[kh]
        vp, vc = vp_ref[kh], vc_ref[kh]
        vps = (jnp.where(lo, vp, zero), jnp.where(lo, zero, vp))
        vcs = (jnp.where(lo, vc, zero), jnp.where(lo, zero, vc))
        for c in range(group // 2):
            col = kh * (group // 2) + c
            qcol = q_ref[:, col * LANES:(col + 1) * LANES]
            out = None
            for u in range(2):
                qm = jnp.where(lo, qcol, zero) if u == 0 else jnp.where(lo, zero, qcol)
                s_p = _dot_t(qm, kp)
                s_c = _dot_t(qm, kc)
                if banded:
                    s_p = s_p + bias_p
                    s_c = s_c + bias_c
                sink = sink_ref[2 * col + u]
                m = jnp.maximum(jnp.maximum(jnp.max(s_p, axis=-1, keepdims=True),
                                            jnp.max(s_c, axis=-1, keepdims=True)), sink)
                e_p = jnp.exp(s_p - m)
                e_c = jnp.exp(s_c - m)
                den = (jnp.sum(e_p, axis=-1, keepdims=True) + jnp.sum(e_c, axis=-1, keepdims=True)
                       + jnp.exp(sink - m))
                o = (_dot(e_p.astype(BF16), vps[u]) + _dot(e_c.astype(BF16), vcs[u])) / den
                out = o if out is None else out + o
            o_ref[:, col * LANES:(col + 1) * LANES] = out.astype(BF16)


def _swa_attn(sinks, q, kd, vd, batch, seq, tq):
    n, dq = q.shape
    kv_heads = kd.shape[0]
    group = (dq // SWA_HEAD_DIM) // kv_heads
    nq = seq // tq
    per = tq // WINDOW
    prev = lambda b, i: (0, jnp.maximum(i * per - 1, 0) + b * nq * per, 0)
    cur = lambda b, i: (0, b * nq + i, 0)
    return pl.pallas_call(
        functools.partial(_swa_attn_kernel, kv_heads=kv_heads, group=group, banded=True),
        grid=(batch, nq),
        in_specs=[pl.BlockSpec(memory_space=pltpu.SMEM),
                  pl.BlockSpec((tq, dq), lambda b, i: (b * nq + i, 0)),
                  pl.BlockSpec((kv_heads, WINDOW, LANES), prev),
                  pl.BlockSpec((kv_heads, tq, LANES), cur),
                  pl.BlockSpec((kv_heads, WINDOW, LANES), prev),
                  pl.BlockSpec((kv_heads, tq, LANES), cur)],
        out_specs=pl.BlockSpec((tq, dq), lambda b, i: (b * nq + i, 0)),
        out_shape=jax.ShapeDtypeStruct((n, dq), BF16),
        compiler_params=_params("parallel", "arbitrary"),
        name="swa_attn",
    )(sinks, q, kd, kd, vd, vd)


def _swa_attn_sample(sinks, q, kd_cache, kd_new, vd_cache, vd_new, batch, keep, t):
    n, dq = q.shape
    kv_heads = kd_new.shape[0]
    group = (dq // SWA_HEAD_DIM) // kv_heads
    blk = lambda b: (0, b, 0)
    return pl.pallas_call(
        functools.partial(_swa_attn_kernel, kv_heads=kv_heads, group=group, banded=False),
        grid=(batch,),
        in_specs=[pl.BlockSpec(memory_space=pltpu.SMEM),
                  pl.BlockSpec((t, dq), lambda b: (b, 0)),
                  pl.BlockSpec((kv_heads, keep, LANES), blk),
                  pl.BlockSpec((kv_heads, t, LANES), blk),
                  pl.BlockSpec((kv_heads, keep, LANES), blk),
                  pl.BlockSpec((kv_heads, t, LANES), blk)],
        out_specs=pl.BlockSpec((t, dq), lambda b: (b, 0)),
        out_shape=jax.ShapeDtypeStruct((n, dq), BF16),
        compiler_params=_params("parallel"),
        name="swa_attn_sample",
    )(sinks, q, kd_cache, kd_new, vd_cache, vd_new)


def _oproj_kernel(h_ref, o_ref, w_ref, out_ref):
    out_ref[...] = h_ref[...] + _dot(o_ref[...], w_ref[...])


def _oproj(h, o, w, tm):
    n, d = h.shape
    row = lambda i: (i, 0)
    return pl.pallas_call(
        _oproj_kernel,
        grid=(n // tm,),
        in_specs=[pl.BlockSpec((tm, d), row), pl.BlockSpec((tm, o.shape[1]), row), _resident(w.shape)],
        out_specs=pl.BlockSpec((tm, d), row),
        out_shape=jax.ShapeDtypeStruct((n, d), F32),
        compiler_params=_params("parallel"),
        name="attn_out_proj",
    )(h, o, w)


def _ffn_kernel(h_ref, g_ref, wg_ref, wu_ref, wd_ref, out_ref, hn_ref):
    @pl.when(pl.program_id(1) == 0)
    def _():
        x = h_ref[...]
        hn_ref[...] = _rms(x, g_ref[...]).astype(BF16)
        out_ref[...] = x

    hn = hn_ref[...]
    a = _dot(hn, wg_ref[...])
    b = _dot(hn, wu_ref[...])
    act = (a * jax.nn.sigmoid(a) * b).astype(BF16)
    out_ref[...] += _dot(act, wd_ref[...])


def _ffn(h, g, wg, wu, wd, tm, tf):
    n, d = h.shape
    f = wg.shape[1]
    row = lambda i, j: (i, 0)
    return pl.pallas_call(
        _ffn_kernel,
        grid=(n // tm, f // tf),
        in_specs=[pl.BlockSpec((tm, d), row), pl.BlockSpec((1, d), lambda i, j: (0, 0)),
                  pl.BlockSpec((d, tf), lambda i, j: (0, j)),
                  pl.BlockSpec((d, tf), lambda i, j: (0, j)),
                  pl.BlockSpec((tf, d), lambda i, j: (j, 0))],
        out_specs=pl.BlockSpec((tm, d), row),
        out_shape=jax.ShapeDtypeStruct((n, d), F32),
        scratch_shapes=[pltpu.VMEM((tm, d), BF16)],
        compiler_params=_params("parallel", "arbitrary"),
        name="swiglu_ffn",
    )(h, g, wg, wu, wd)


def _ple_kernel(h_ref, g_ref, wgate_ref, p_ref, wproj_ref, gfin_ref, out_ref, *, final):
    x = h_ref[...]
    gate = jax.nn.sigmoid(_dot(_rms(x, g_ref[...]).astype(BF16), wgate_ref[...]))
    y = x + gate * _dot(p_ref[...].astype(BF16), wproj_ref[...])
    if final:
        y = _rms(y, gfin_ref[...])
    out_ref[...] = y


def _ple(h, g, wgate, p, wproj, gfin, final, tm):
    n, d = h.shape
    row = lambda i: (i, 0)
    return pl.pallas_call(
        functools.partial(_ple_kernel, final=final),
        grid=(n // tm,),
        in_specs=[pl.BlockSpec((tm, d), row), _resident(g.shape), _resident(wgate.shape),
                  pl.BlockSpec((tm, p.shape[1]), row), _resident(wproj.shape), _resident(gfin.shape)],
        out_specs=pl.BlockSpec((tm, d), row),
        out_shape=jax.ShapeDtypeStruct((n, d), F32),
        compiler_params=_params("parallel"),
        name="ple_embed",
    )(h, g, wgate, p, wproj, gfin)


def _angles(pos, r, theta):
    inv = jnp.power(jnp.float32(theta), -jnp.arange(0, r, 2, dtype=jnp.float32) / r)
    ang = pos.astype(jnp.float32)[:, None] * inv[None, :]
    return jnp.cos(ang), jnp.sin(ang)


def _mla_table(pos):
    cos, sin = _angles(pos, MLA_ROPE, MLA_THETA)
    return jnp.concatenate([cos, cos, -sin, sin], axis=-1)


def _swa_tables(pos):
    cos, sin = _angles(pos, SWA_ROT, SWA_THETA)
    s = cos.shape[0]
    rest = SWA_HEAD_DIM - SWA_ROT
    one = jnp.ones((s, rest), F32)
    zero = jnp.zeros((s, rest), F32)
    zh = jnp.zeros_like(sin)
    tc = jnp.concatenate([cos, cos, one], axis=-1)
    ts1 = jnp.concatenate([-sin, zh, zero], axis=-1)
    ts2 = jnp.concatenate([zh, sin, zero], axis=-1)
    return tuple(jnp.concatenate([t, t], axis=-1) for t in (tc, ts1, ts2))


def _tile_rows(tab, rows):
    return tab if tab.shape[0] >= rows else jnp.tile(tab, (rows // tab.shape[0], 1))


def _swap_halves(x):
    half = x.shape[-1] // 2
    return jnp.concatenate([x[..., half:], x[..., :half]], axis=-1)


def _prep_mla(j, g_attn_norm_i, w_mla_in, g_mla_q_a, w_mla_q_up, g_mla_kv_a, w_mla_kv_up, g_mla_q_nope,
              g_mla_q_rope, g_mla_k_nope, g_mla_k_rope, w_mla_out):
    q_lora, kv_lora = g_mla_q_a.shape[1], g_mla_kv_a.shape[1]
    qk = MLA_NOPE + MLA_ROPE
    heads = w_mla_q_up.shape[2] // qk
    w_in = w_mla_in[j]
    k_r = w_in[:, q_lora + kv_lora:]
    w_in = jnp.concatenate([w_in, _swap_halves(k_r)], axis=-1)
    wq = w_mla_q_up[j].reshape(q_lora, heads, qk)
    wq = jnp.concatenate([wq, _swap_halves(wq[:, :, MLA_NOPE:])], axis=-1).reshape(q_lora, heads * 256)
    wkv = w_mla_kv_up[j].reshape(kv_lora, heads, MLA_NOPE + MLA_V)
    dup = lambda g: jnp.concatenate([g, _swap_halves(g)], axis=-1)[None, :]
    return dict(
        heads=heads,
        g_attn=g_attn_norm_i[None, :],
        w_in=w_in.astype(BF16),
        g_q_a=g_mla_q_a[j][None, :], g_kv_a=g_mla_kv_a[j][None, :],
        g_k_rope=dup(g_mla_k_rope[j]), g_q_rope=dup(g_mla_q_rope[j]),
        g_q_nope=g_mla_q_nope[j][None, :], g_k_nope=g_mla_k_nope[j][None, :],
        w_q=wq.astype(BF16),
        w_kn=wkv[:, :, :MLA_NOPE].reshape(kv_lora, heads * MLA_NOPE).astype(BF16),
        w_v=wkv[:, :, MLA_NOPE:].reshape(kv_lora, heads * MLA_V).astype(BF16),
        w_out=w_mla_out[j].astype(BF16),
    )


def _prep_swa(j, g_attn_norm_i, w_swa_qkv, g_swa_q, g_swa_k, swa_sinks, w_swa_out):
    q_heads = swa_sinks.shape[1]
    w = w_swa_qkv[j]
    d = w.shape[0]
    nq = q_heads * SWA_HEAD_DIM
    kv_heads = (w.shape[1] - nq) // (2 * SWA_HEAD_DIM)
    nk = kv_heads * SWA_HEAD_DIM
    wk = w[:, nq:nq + nk].reshape(d, kv_heads, SWA_HEAD_DIM)
    wv = w[:, nq + nk:].reshape(d, kv_heads, SWA_HEAD_DIM)
    wkv = jnp.concatenate([wk, wk, wv, wv], axis=-1).reshape(d, kv_heads * 4 * SWA_HEAD_DIM)
    gmat = np.kron(np.eye(LANES // SWA_HEAD_DIM), np.ones((SWA_HEAD_DIM, SWA_HEAD_DIM)))
    dup = lambda g: jnp.concatenate([g, g], axis=-1)[None, :]
    return dict(
        q_heads=q_heads, kv_heads=kv_heads,
        g_attn=g_attn_norm_i[None, :],
        w_qkv=jnp.concatenate([w[:, :nq], wkv], axis=-1).astype(BF16),
        g_q=dup(g_swa_q[j]), g_k=dup(g_swa_k[j]),
        gmat=jnp.asarray(gmat, BF16),
        sinks=swa_sinks[j],
        w_out=w_swa_out[j].astype(BF16),
    )


def _kv_dup(t):
    b, l, kv, d = t.shape
    t = jnp.transpose(t, (2, 0, 1, 3)).reshape(kv, b * l, d)
    return jnp.concatenate([t, t], axis=-1).astype(BF16)


def _trunk(x, p, pos, caches, layers, ffn_w, ple_w, g_final, tm, tf):
    batch, seq, d = x.shape
    n = batch * seq
    h = x.reshape(n, d)
    rows = max(tm, seq)
    mla_tab = _tile_rows(_mla_table(pos), rows)
    swa_tabs = tuple(_tile_rows(t, rows) for t in _swa_tables(pos))
    depth = len(layers)
    lats, krs, sks, svs = [], [], [], []
    for i, (kind, w) in enumerate(layers):
        j = i // 2
        if kind == 'mla':
            lat, kr, q, k, v = _mla_proj(h, w, mla_tab, tm)
            heads = w['heads']
            if caches is None:
                o = _mla_attn(q, k, v, batch, seq, tq=512, hb=4)
            else:
                lat_c, kr_c = caches[0][j], caches[1][j]
                past = lat_c.shape[1]
                k_c, v_c = _mla_expand_call(lat_c.reshape(batch * past, -1),
                                            kr_c.reshape(batch * past, -1), w, tm=512)
                o = _mla_attn_sample(q, k_c, k, v_c, v, batch, past, seq)
            lats.append(lat.reshape(batch, seq, -1))
            krs.append(kr.reshape(batch, seq, -1))
        else:
            q, kd, vd, kf, vf = _swa_proj(h, w, swa_tabs, tm)
            kv_heads = w['kv_heads']
            unhead = lambda t: jnp.transpose(t.reshape(kv_heads, batch, seq, SWA_HEAD_DIM), (1, 2, 0, 3))
            if caches is None:
                o = _swa_attn(w['sinks'], q, kd, vd, batch, seq, tq=256)
                keep = min(WINDOW, seq)
                sks.append(unhead(kf)[:, seq - keep:])
                svs.append(unhead(vf)[:, seq - keep:])
            else:
                ck, cv = caches[2][j], caches[3][j]
                keep = ck.shape[1]
                o = _swa_attn_sample(w['sinks'], q, _kv_dup(ck), kd, _kv_dup(cv), vd, batch, keep, seq)
                sks.append(jnp.concatenate([ck, unhead(kf)], axis=1)[:, -keep:])
                svs.append(jnp.concatenate([cv, unhead(vf)], axis=1)[:, -keep:])
        h = _oproj(h, o, w['w_out'], tm)
        g_ffn, wg, wu, wd = ffn_w[i]
        h = _ffn(h, g_ffn, wg, wu, wd, tm, tf)
        g_ple, wgate, wproj = ple_w[i]
        h = _ple(h, g_ple, wgate, p[i].reshape(n, -1), wproj, g_final, i == depth - 1, tm)
    return (h.reshape(batch, seq, d), jnp.stack(lats), jnp.stack(krs), jnp.stack(sks), jnp.stack(svs))


def kernel(x_prompt, x_sample, p_prompt, p_sample, cache_mla_latent, cache_mla_krope, state_swa_k, state_swa_v, g_attn_norm, w_mla_in, g_mla_q_a, w_mla_q_up, g_mla_kv_a, w_mla_kv_up, g_mla_q_nope, g_mla_q_rope, g_mla_k_nope, g_mla_k_rope, w_mla_out, w_swa_qkv, g_swa_q, g_swa_k, swa_sinks, w_swa_out, g_ffn_norm, w_ffn_gate, w_ffn_up, w_ffn_down, g_ple_norm, w_ple_gate, w_ple_proj, g_final):
    depth = g_attn_norm.shape[0]
    layers = []
    for i in range(depth):
        j = i // 2
        if i % 2 == 0:
            layers.append(('mla', _prep_mla(j, g_attn_norm[i], w_mla_in, g_mla_q_a, w_mla_q_up,
                                            g_mla_kv_a, w_mla_kv_up, g_mla_q_nope, g_mla_q_rope,
                                            g_mla_k_nope, g_mla_k_rope, w_mla_out)))
        else:
            layers.append(('swa', _prep_swa(j, g_attn_norm[i], w_swa_qkv, g_swa_q, g_swa_k,
                                            swa_sinks, w_swa_out)))
    ffn_w = [(g_ffn_norm[i][None, :], w_ffn_gate[i].astype(BF16), w_ffn_up[i].astype(BF16),
              w_ffn_down[i].astype(BF16)) for i in range(depth)]
    ple_w = [(g_ple_norm[i][None, :], w_ple_gate[i].astype(BF16), w_ple_proj[i].astype(BF16))
             for i in range(depth)]
    gfin = g_final[None, :]

    seq = x_prompt.shape[1]
    t = x_sample.shape[1]
    past = cache_mla_latent.shape[2]
    pos_p = jnp.arange(seq, dtype=jnp.int32)
    pos_s = past + jnp.arange(t, dtype=jnp.int32)
    y_p, lat_p, kr_p, sk_p, sv_p = _trunk(x_prompt, p_prompt, pos_p, None, layers, ffn_w, ple_w,
                                          gfin, tm=512, tf=512)
    caches = (cache_mla_latent, cache_mla_krope, state_swa_k, state_swa_v)
    n_s = x_sample.shape[0] * t
    y_s, lat_s, kr_s, sk_s, sv_s = _trunk(x_sample, p_sample, pos_s, caches, layers, ffn_w, ple_w,
                                          gfin, tm=n_s, tf=512)
    return (y_p, y_s, lat_p, kr_p, sk_p, sv_p, lat_s, kr_s, sk_s, sv_s)
```

```python
import functools

import numpy as np
import jax
import jax.numpy as jnp
from jax import lax
from jax.experimental import pallas as pl
from jax.experimental.pallas import tpu as pltpu

F32 = jnp.float32
BF16 = jnp.bfloat16

EPS = 1e-6
CHUNK = 64
WINDOW = 128
MLA_THETA = 10000.0
SWA_THETA = 500000.0
MLA_NOPE = 128
MLA_ROPE = 64
MLA_V = 128
SWA_HEAD_DIM = 64
SWA_ROT = SWA_HEAD_DIM // 4

LANES = 128
VMEM_LIMIT = 56 * 1024 * 1024
NEG = -1e30
LOG2E = 1.4426950408889634


def _params(*sem):
    return pltpu.CompilerParams(dimension_semantics=sem, vmem_limit_bytes=VMEM_LIMIT)


def _resident(shape):
    zeros = (0,) * len(shape)
    return pl.BlockSpec(shape, lambda *_: zeros, pipeline_mode=pl.Buffered(1))


def _rms(x, g):
    return x * lax.rsqrt(jnp.mean(x * x, axis=-1, keepdims=True) + EPS) * g


def _dot(a, b):
    return jnp.dot(a, b, preferred_element_type=F32)


def _dot_t(a, b):
    return lax.dot_general(a, b, (((1,), (1,)), ((), ())), preferred_element_type=F32)


def _rope_dup(x, g, tab):
    ss = jnp.sum(x * x, axis=-1, keepdims=True)
    y = x * lax.rsqrt(ss * (1.0 / LANES) + EPS) * g
    t = y * tab
    return t + pltpu.roll(t, LANES // 2, axis=1)


def _mla_heads(c_q, latb, krb, tab, wq_ref, wkn_ref, wv_ref, gqn, gqr, gkn, gsum_ref,
               q_ref, k_ref, v_ref, heads):
    gsum = gsum_ref[...]
    pair = 2 * LANES

    def inv_rms(x):
        return lax.rsqrt(_dot((x * x).astype(BF16), gsum) * (1.0 / LANES) + EPS)

    def finish(p, xq, xk, xv):
        kn = xk * inv_rms(xk) * gkn
        if xq is not None:
            qn = xq[:, :pair] * inv_rms(xq[:, :pair]) * gqn
            t = xq[:, pair:] * inv_rms(xq[:, pair:]) * gqr * tab
        for u in range(2):
            h = 2 * p + u
            sl = slice(u * LANES, (u + 1) * LANES)
            k_ref[h, :, 0:MLA_NOPE] = kn[:, sl].astype(BF16)
            k_ref[h, :, MLA_NOPE:MLA_NOPE + MLA_ROPE] = krb
            v_ref[h] = xv[:, sl].astype(BF16)
            if xq is not None:
                q_ref[h, :, 0:MLA_NOPE] = qn[:, sl].astype(BF16)
                tu = t[:, sl]
                qr = tu + pltpu.roll(tu, LANES // 2, axis=1)
                q_ref[h, :, MLA_NOPE:MLA_NOPE + MLA_ROPE] = qr[:, :MLA_ROPE].astype(BF16)

    pending = None
    for p in range(heads // 2):
        xq = None if c_q is None else _dot(c_q, wq_ref[:, p * 2 * pair:(p + 1) * 2 * pair])
        xk = _dot(latb, wkn_ref[:, p * pair:(p + 1) * pair])
        xv = _dot(latb, wv_ref[:, p * pair:(p + 1) * pair])
        if pending is not None:
            finish(*pending)
        pending = (p, xq, xk, xv)
    finish(*pending)


def _mla_proj_kernel(h_ref, gattn_ref, win_ref, gqa_ref, gkva_ref, gkr_ref, tab_ref, wq_ref,
                     gqn_ref, gqr_ref, wkn_ref, wv_ref, gkn_ref, gsum_ref,
                     lat_ref, kr_ref, q_ref, k_ref, v_ref, *, heads, q_lora, kv_lora, scale):
    hn = _rms(h_ref[...], gattn_ref[...]).astype(BF16)
    a = _dot(hn, win_ref[...])
    c_q = _rms(a[:, :q_lora], gqa_ref[...]).astype(BF16)
    lat = _rms(a[:, q_lora:q_lora + kv_lora], gkva_ref[...])
    lat_ref[...] = lat
    tab = tab_ref[...]
    kr = _rope_dup(a[:, q_lora + kv_lora:], gkr_ref[...], tab)
    kr_ref[...] = kr[:, :MLA_ROPE]
    _mla_heads(c_q, lat.astype(BF16), kr[:, :MLA_ROPE].astype(BF16), jnp.concatenate([tab, tab], axis=1),
               wq_ref, wkn_ref, wv_ref, gqn_ref[...] * scale, gqr_ref[...] * scale, gkn_ref[...],
               gsum_ref, q_ref, k_ref, v_ref, heads)


def _mla_expand_kernel(lat_ref, kr_ref, wkn_ref, wv_ref, gkn_ref, gsum_ref, k_ref, v_ref, *, heads):
    _mla_heads(None, lat_ref[...].astype(BF16), kr_ref[...].astype(BF16), None, None, wkn_ref, wv_ref,
               None, None, gkn_ref[...], gsum_ref, None, k_ref, v_ref, heads)


def _mla_proj(h, w, tab, tm):
    n, d = h.shape
    heads = w['heads']
    q_lora, kv_lora = w['g_q_a'].shape[1], w['g_kv_a'].shape[1]
    qk = MLA_NOPE + MLA_ROPE
    nt = tab.shape[0] // tm
    row = lambda i: (i, 0)
    hrow = lambda i: (0, i, 0)
    kern = functools.partial(_mla_proj_kernel, heads=heads, q_lora=q_lora, kv_lora=kv_lora,
                             scale=qk ** -0.5 * LOG2E)
    return pl.pallas_call(
        kern,
        grid=(n // tm,),
        in_specs=[
            pl.BlockSpec((tm, d), row),
            _resident(w['g_attn'].shape), _resident(w['w_in'].shape),
            _resident(w['g_q_a'].shape), _resident(w['g_kv_a'].shape), _resident(w['g_k_rope'].shape),
            pl.BlockSpec((tm, LANES), lambda i: (i % nt, 0)),
            _resident(w['w_q'].shape), _resident(w['g_q_nope'].shape), _resident(w['g_q_rope'].shape),
            _resident(w['w_kn'].shape), _resident(w['w_v'].shape), _resident(w['g_k_nope'].shape),
            _resident(w['gsum'].shape),
        ],
        out_specs=[
            pl.BlockSpec((tm, kv_lora), row),
            pl.BlockSpec((tm, MLA_ROPE), row),
            pl.BlockSpec((heads, tm, qk), hrow),
            pl.BlockSpec((heads, tm, qk), hrow),
            pl.BlockSpec((heads, tm, MLA_V), hrow),
        ],
        out_shape=[
            jax.ShapeDtypeStruct((n, kv_lora), F32),
            jax.ShapeDtypeStruct((n, MLA_ROPE), F32),
            jax.ShapeDtypeStruct((heads, n, qk), BF16),
            jax.ShapeDtypeStruct((heads, n, qk), BF16),
            jax.ShapeDtypeStruct((heads, n, MLA_V), BF16),
        ],
        compiler_params=_params("parallel"),
        name="mla_proj",
    )(h, w['g_attn'], w['w_in'], w['g_q_a'], w['g_kv_a'], w['g_k_rope'], tab, w['w_q'],
      w['g_q_nope'], w['g_q_rope'], w['w_kn'], w['w_v'], w['g_k_nope'], w['gsum'])


def _mla_expand_call(lat, kr, w, tm):
    n, kv_lora = lat.shape
    heads = w['heads']
    qk = MLA_NOPE + MLA_ROPE
    row = lambda i: (i, 0)
    hrow = lambda i: (0, i, 0)
    return pl.pallas_call(
        functools.partial(_mla_expand_kernel, heads=heads),
        grid=(n // tm,),
        in_specs=[pl.BlockSpec((tm, kv_lora), row), pl.BlockSpec((tm, MLA_ROPE), row),
                  _resident(w['w_kn'].shape), _resident(w['w_v'].shape),
                  _resident(w['g_k_nope'].shape), _resident(w['gsum'].shape)],
        out_specs=[pl.BlockSpec((heads, tm, qk), hrow), pl.BlockSpec((heads, tm, MLA_V), hrow)],
        out_shape=[jax.ShapeDtypeStruct((heads, n, qk), BF16),
                   jax.ShapeDtypeStruct((heads, n, MLA_V), BF16)],
        compiler_params=_params("parallel"),
        name="mla_expand",
    )(lat, kr, w['w_kn'], w['w_v'], w['g_k_nope'], w['gsum'])


def _mla_attn_kernel(q_ref, k_ref, v_ref, o_ref, *, hb, tq):
    i = pl.program_id(2)
    kc = lax.broadcasted_iota(jnp.int32, (tq, tq), 0) // CHUNK
    qc = lax.broadcasted_iota(jnp.int32, (tq, tq), 1) // CHUNK
    diag_bias = jnp.where(kc <= qc, 0.0, NEG).astype(F32)
    qs = [q_ref[h] for h in range(hb)]

    def step(j, carry, bias=None):
        start = pl.multiple_of(j * tq, tq)
        ss = [_dot_t(k_ref[h, pl.ds(start, tq), :], qs[h]) for h in range(hb)]
        ps, stats = [], []
        for h in range(hb):
            m, l, _ = carry[h]
            s = ss[h] if bias is None else ss[h] + bias
            m_new = jnp.maximum(m, jnp.max(s, axis=0, keepdims=True))
            alpha = jnp.exp2(m - m_new)
            p = jnp.exp2(s - m_new)
            stats.append((m_new, alpha * l + jnp.sum(p, axis=0, keepdims=True), alpha))
            ps.append(p.astype(BF16))
        out = []
        for h in range(hb):
            pv = lax.dot_general(v_ref[h, pl.ds(start, tq), :], ps[h],
                                 (((0,), (0,)), ((), ())), preferred_element_type=F32)
            m_new, l, alpha = stats[h]
            out.append((m_new, l, alpha * carry[h][2] + pv))
        return tuple(out)

    init = tuple((jnp.full((1, tq), NEG, F32), jnp.zeros((1, tq), F32), jnp.zeros((MLA_V, tq), F32))
                 for _ in range(hb))
    carry = lax.fori_loop(0, i, step, init)
    carry = step(i, carry, diag_bias)
    for h in range(hb):
        _, l, acc = carry[h]
        o_ref[:, h * MLA_V:(h + 1) * MLA_V] = (acc / l).T.astype(BF16)


def _mla_attn(q, k, v, batch, seq, tq, hb):
    heads, n, qk = q.shape
    nq = seq // tq
    kv_map = lambda b, g, i: (g, b, 0)
    return pl.pallas_call(
        functools.partial(_mla_attn_kernel, hb=hb, tq=tq),
        grid=(batch, heads // hb, nq),
        in_specs=[pl.BlockSpec((hb, tq, qk), lambda b, g, i: (g, b * nq + i, 0)),
                  pl.BlockSpec((hb, seq, qk), kv_map),
                  pl.BlockSpec((hb, seq, MLA_V), kv_map)],
        out_specs=pl.BlockSpec((tq, hb * MLA_V), lambda b, g, i: (b * nq + i, g)),
        out_shape=jax.ShapeDtypeStruct((n, heads * MLA_V), BF16),
        compiler_params=_params("parallel", "parallel", "arbitrary"),
        name="mla_attn",
    )(q, k, v)


def _mla_attn_sample_kernel(q_ref, kc_ref, kn_ref, vc_ref, vn_ref, o_ref, *, hb):
    ss = [(_dot_t(kc_ref[h], q_ref[h]), _dot_t(kn_ref[h], q_ref[h])) for h in range(hb)]
    ps = []
    for s_c, s_n in ss:
        m = jnp.maximum(jnp.max(s_c, axis=0, keepdims=True), jnp.max(s_n, axis=0, keepdims=True))
        p_c = jnp.exp2(s_c - m)
        p_n = jnp.exp2(s_n - m)
        l = jnp.sum(p_c, axis=0, keepdims=True) + jnp.sum(p_n, axis=0, keepdims=True)
        ps.append((p_c.astype(BF16), p_n.astype(BF16), l))
    tn = (((0,), (0,)), ((), ()))
    for h, (p_c, p_n, l) in enumerate(ps):
        acc = (lax.dot_general(vc_ref[h], p_c, tn, preferred_element_type=F32)
               + lax.dot_general(vn_ref[h], p_n, tn, preferred_element_type=F32))
        o_ref[:, h * MLA_V:(h + 1) * MLA_V] = (acc / l).T.astype(BF16)


def _mla_attn_sample(q, k_cache, k_new, v_cache, v_new, batch, past, t, hb):
    heads, n, qk = q.shape
    q_pos = past + np.arange(t)
    k_pos = np.arange(past + t)
    assert np.all((k_pos // CHUNK)[None, :] <= (q_pos // CHUNK)[:, None])
    m3 = lambda b, g: (g, b, 0)
    return pl.pallas_call(
        functools.partial(_mla_attn_sample_kernel, hb=hb),
        grid=(batch, heads // hb),
        in_specs=[pl.BlockSpec((hb, t, qk), m3), pl.BlockSpec((hb, past, qk), m3),
                  pl.BlockSpec((hb, t, qk), m3), pl.BlockSpec((hb, past, MLA_V), m3),
                  pl.BlockSpec((hb, t, MLA_V), m3)],
        out_specs=pl.BlockSpec((t, hb * MLA_V), lambda b, g: (b, g)),
        out_shape=jax.ShapeDtypeStruct((n, heads * MLA_V), BF16),
        compiler_params=_params("parallel", "parallel"),
        name="mla_attn_sample",
    )(q, k_cache, k_new, v_cache, v_new)


def _swa_proj_kernel(h_ref, gattn_ref, w_ref, gq_ref, gk_ref, tc_ref, ts1_ref, ts2_ref, gmat_ref,
                     q_ref, kd_ref, vd_ref, kf_ref, vf_ref, *, q_cols, kv_heads, scale):
    hn = _rms(h_ref[...], gattn_ref[...]).astype(BF16)
    tc, ts1, ts2 = tc_ref[...], ts1_ref[...], ts2_ref[...]
    gmat = gmat_ref[...]

    def rope(x, ss, g):
        y = x * lax.rsqrt(ss * (1.0 / SWA_HEAD_DIM) + EPS) * g
        return (y * tc + pltpu.roll(y, LANES - SWA_ROT // 2, axis=1) * ts1
                + pltpu.roll(y, SWA_ROT // 2, axis=1) * ts2)

    gq = gq_ref[...] * scale
    gk = gk_ref[...]
    n_q = q_cols // 2

    def finish(c, x2):
        ss = _dot((x2 * x2).astype(BF16), gmat)
        if c < n_q:
            for u in range(2):
                col = 2 * c + u
                sl = slice(u * LANES, (u + 1) * LANES)
                q_ref[:, col * LANES:(col + 1) * LANES] = rope(x2[:, sl], ss[:, sl], gq).astype(BF16)
        else:
            kh = c - n_q
            kd = rope(x2[:, :LANES], ss[:, :LANES], gk)
            kd_ref[kh] = kd.astype(BF16)
            kf_ref[kh] = kd[:, :SWA_HEAD_DIM]
            vd_ref[kh] = x2[:, LANES:].astype(BF16)
            vf_ref[kh] = x2[:, LANES:LANES + SWA_HEAD_DIM]

    pending = None
    for c in range(n_q + kv_heads):
        x2 = _dot(hn, w_ref[:, c * 256:(c + 1) * 256])
        if pending is not None:
            finish(*pending)
        pending = (c, x2)
    finish(*pending)


def _swa_proj(h, w, tabs, tm):
    n, d = h.shape
    q_heads, kv_heads = w['q_heads'], w['kv_heads']
    q_cols = q_heads * SWA_HEAD_DIM // LANES
    nt = tabs[0].shape[0] // tm
    row = lambda i: (i, 0)
    hrow = lambda i: (0, i, 0)
    tspec = pl.BlockSpec((tm, LANES), lambda i: (i % nt, 0))
    kern = functools.partial(_swa_proj_kernel, q_cols=q_cols, kv_heads=kv_heads,
                             scale=SWA_HEAD_DIM ** -0.5 * LOG2E)
    return pl.pallas_call(
        kern,
        grid=(n // tm,),
        in_specs=[pl.BlockSpec((tm, d), row), _resident(w['g_attn'].shape),
                  _resident(w['w_qkv'].shape), _resident(w['g_q'].shape), _resident(w['g_k'].shape),
                  tspec, tspec, tspec, _resident(w['gmat'].shape)],
        out_specs=[pl.BlockSpec((tm, q_cols * LANES), row),
                   pl.BlockSpec((kv_heads, tm, LANES), hrow),
                   pl.BlockSpec((kv_heads, tm, LANES), hrow),
                   pl.BlockSpec((kv_heads, tm, SWA_HEAD_DIM), hrow),
                   pl.BlockSpec((kv_heads, tm, SWA_HEAD_DIM), hrow)],
        out_shape=[jax.ShapeDtypeStruct((n, q_cols * LANES), BF16),
                   jax.ShapeDtypeStruct((kv_heads, n, LANES), BF16),
                   jax.ShapeDtypeStruct((kv_heads, n, LANES), BF16),
                   jax.ShapeDtypeStruct((kv_heads, n, SWA_HEAD_DIM), F32),
                   jax.ShapeDtypeStruct((kv_heads, n, SWA_HEAD_DIM), F32)],
        compiler_params=_params("parallel"),
        name="swa_proj",
    )(h, w['g_attn'], w['w_qkv'], w['g_q'], w['g_k'], *tabs, w['gmat'])


def _swa_attn_kernel(sink_ref, q_ref, kp_ref, kc_ref, vp_ref, vc_ref, o_ref, *,
                     kv_heads, group, banded):
    tq = q_ref.shape[0]
    n_p, n_c = kp_ref.shape[1], kc_ref.shape[1]
    n_k = n_p + n_c
    lane = lax.broadcasted_iota(jnp.int32, (1, LANES), 1)
    row = lax.broadcasted_iota(jnp.int32, (LANES, 1), 0)
    bias = None
    if banded:
        win = WINDOW // CHUNK
        kc = lax.broadcasted_iota(jnp.int32, (n_k, tq), 0) // CHUNK - n_p // CHUNK
        qc = lax.broadcasted_iota(jnp.int32, (n_k, tq), 1) // CHUNK
        ok = (kc <= qc) & (kc >= qc - win) & ((kc >= 0) | (pl.program_id(1) > 0))
        bias = jnp.where(ok, 0.0, NEG).astype(F32)
    zero = jnp.zeros((), BF16)
    half = group // 2
    for kh in range(kv_heads):
        k_all = jnp.concatenate([kp_ref[kh], kc_ref[kh]], axis=0)
        v_t = jnp.concatenate([vp_ref[kh], vc_ref[kh]], axis=0).T
        ks = (jnp.where(lane < SWA_HEAD_DIM, k_all, zero), jnp.where(lane < SWA_HEAD_DIM, zero, k_all))
        vs = (jnp.where(row < SWA_HEAD_DIM, v_t, zero), jnp.where(row < SWA_HEAD_DIM, zero, v_t))
        cols = [q_ref[:, (kh * half + c) * LANES:(kh * half + c + 1) * LANES] for c in range(half)]
        ss = [_dot_t(ks[u], cols[c]) for c in range(half) for u in range(2)]
        es, rdens = [], []
        for c in range(half):
            for u in range(2):
                s = ss[2 * c + u]
                if banded:
                    s = s + bias
                sink = sink_ref[2 * (kh * half + c) + u] * LOG2E
                m = jnp.maximum(jnp.max(s, axis=0, keepdims=True), sink)
                e = jnp.exp2(s - m)
                rdens.append(1.0 / (jnp.sum(e, axis=0, keepdims=True) + jnp.exp2(sink - m)))
                es.append(e.astype(BF16))
        for c in range(half):
            o_t = (_dot(vs[0], es[2 * c]) * rdens[2 * c]
                   + _dot(vs[1], es[2 * c + 1]) * rdens[2 * c + 1])
            col = kh * half + c
            o_ref[:, col * LANES:(col + 1) * LANES] = o_t.T.astype(BF16)


def _swa_attn(sinks, q, kd, vd, batch, seq, tq):
    n, dq = q.shape
    kv_heads = kd.shape[0]
    group = (dq // SWA_HEAD_DIM) // kv_heads
    nq = seq // tq
    per = tq // WINDOW
    prev = lambda b, i: (0, jnp.maximum(i * per - 1, 0) + b * nq * per, 0)
    cur = lambda b, i: (0, b * nq + i, 0)
    return pl.pallas_call(
        functools.partial(_swa_attn_kernel, kv_heads=kv_heads, group=group, banded=True),
        grid=(batch, nq),
        in_specs=[pl.BlockSpec(memory_space=pltpu.SMEM),
                  pl.BlockSpec((tq, dq), lambda b, i: (b * nq + i, 0)),
                  pl.BlockSpec((kv_heads, WINDOW, LANES), prev),
                  pl.BlockSpec((kv_heads, tq, LANES), cur),
                  pl.BlockSpec((kv_heads, WINDOW, LANES), prev),
                  pl.BlockSpec((kv_heads, tq, LANES), cur)],
        out_specs=pl.BlockSpec((tq, dq), lambda b, i: (b * nq + i, 0)),
        out_shape=jax.ShapeDtypeStruct((n, dq), BF16),
        compiler_params=_params("parallel", "arbitrary"),
        name="swa_attn",
    )(sinks, q, kd, kd, vd, vd)


def _swa_attn_sample(sinks, q, kd_cache, kd_new, vd_cache, vd_new, batch, keep, t):
    n, dq = q.shape
    kv_heads = kd_new.shape[0]
    group = (dq // SWA_HEAD_DIM) // kv_heads
    blk = lambda b: (0, b, 0)
    return pl.pallas_call(
        functools.partial(_swa_attn_kernel, kv_heads=kv_heads, group=group, banded=False),
        grid=(batch,),
        in_specs=[pl.BlockSpec(memory_space=pltpu.SMEM),
                  pl.BlockSpec((t, dq), lambda b: (b, 0)),
                  pl.BlockSpec((kv_heads, keep, LANES), blk),
                  pl.BlockSpec((kv_heads, t, LANES), blk),
                  pl.BlockSpec((kv_heads, keep, LANES), blk),
                  pl.BlockSpec((kv_heads, t, LANES), blk)],
        out_specs=pl.BlockSpec((t, dq), lambda b: (b, 0)),
        out_shape=jax.ShapeDtypeStruct((n, dq), BF16),
        compiler_params=_params("parallel"),
        name="swa_attn_sample",
    )(sinks, q, kd_cache, kd_new, vd_cache, vd_new)


def _oproj_kernel(h_ref, o_ref, w_ref, out_ref):
    out_ref[...] = h_ref[...] + _dot(o_ref[...], w_ref[...])


def _oproj(h, o, w, tm):
    n, d = h.shape
    row = lambda i: (i, 0)
    return pl.pallas_call(
        _oproj_kernel,
        grid=(n // tm,),
        in_specs=[pl.BlockSpec((tm, d), row), pl.BlockSpec((tm, o.shape[1]), row), _resident(w.shape)],
        out_specs=pl.BlockSpec((tm, d), row),
        out_shape=jax.ShapeDtypeStruct((n, d), F32),
        compiler_params=_params("parallel"),
        name="attn_out_proj",
    )(h, o, w)


def _ffn_kernel(h_ref, g_ref, wg_ref, wu_ref, wd_ref, out_ref, hn_ref):
    @pl.when(pl.program_id(1) == 0)
    def _():
        x = h_ref[...]
        hn_ref[...] = _rms(x, g_ref[...]).astype(BF16)
        out_ref[...] = x

    hn = hn_ref[...]
    a = _dot(hn, wg_ref[...])
    b = _dot(hn, wu_ref[...])
    act = (a * jax.nn.sigmoid(a) * b).astype(BF16)
    out_ref[...] += _dot(act, wd_ref[...])


def _ffn(h, g, wg, wu, wd, tm, tf):
    n, d = h.shape
    f = wg.shape[1]
    row = lambda i, j: (i, 0)
    return pl.pallas_call(
        _ffn_kernel,
        grid=(n // tm, f // tf),
        in_specs=[pl.BlockSpec((tm, d), row), pl.BlockSpec((1, d), lambda i, j: (0, 0)),
                  pl.BlockSpec((d, tf), lambda i, j: (0, j)),
                  pl.BlockSpec((d, tf), lambda i, j: (0, j)),
                  pl.BlockSpec((tf, d), lambda i, j: (j, 0))],
        out_specs=pl.BlockSpec((tm, d), row),
        out_shape=jax.ShapeDtypeStruct((n, d), F32),
        scratch_shapes=[pltpu.VMEM((tm, d), BF16)],
        compiler_params=_params("parallel", "arbitrary"),
        name="swiglu_ffn",
    )(h, g, wg, wu, wd)


def _ple_kernel(h_ref, g_ref, wgate_ref, p_ref, wproj_ref, gfin_ref, out_ref, *, final):
    x = h_ref[...]
    gate = jax.nn.sigmoid(_dot(_rms(x, g_ref[...]).astype(BF16), wgate_ref[...]))
    y = x + gate * _dot(p_ref[...].astype(BF16), wproj_ref[...])
    if final:
        y = _rms(y, gfin_ref[...])
    out_ref[...] = y


def _ple(h, g, wgate, p, layer, wproj, gfin, final, tm):
    n, d = h.shape
    row = lambda i: (i, 0)
    return pl.pallas_call(
        functools.partial(_ple_kernel, final=final),
        grid=(n // tm,),
        in_specs=[pl.BlockSpec((tm, d), row), _resident(g.shape), _resident(wgate.shape),
                  pl.BlockSpec((None, tm, p.shape[2]), lambda i: (layer, i, 0)),
                  _resident(wproj.shape), _resident(gfin.shape)],
        out_specs=pl.BlockSpec((tm, d), row),
        out_shape=jax.ShapeDtypeStruct((n, d), F32),
        compiler_params=_params("parallel"),
        name="ple_embed",
    )(h, g, wgate, p, wproj, gfin)


def _angles(pos, r, theta):
    inv = jnp.power(jnp.float32(theta), -jnp.arange(0, r, 2, dtype=jnp.float32) / r)
    ang = pos.astype(jnp.float32)[:, None] * inv[None, :]
    return jnp.cos(ang), jnp.sin(ang)


def _mla_table(pos):
    cos, sin = _angles(pos, MLA_ROPE, MLA_THETA)
    return jnp.concatenate([cos, cos, -sin, sin], axis=-1)


def _swa_tables(pos):
    cos, sin = _angles(pos, SWA_ROT, SWA_THETA)
    s = cos.shape[0]
    rest = SWA_HEAD_DIM - SWA_ROT
    one = jnp.ones((s, rest), F32)
    zero = jnp.zeros((s, rest), F32)
    zh = jnp.zeros_like(sin)
    tc = jnp.concatenate([cos, cos, one], axis=-1)
    ts1 = jnp.concatenate([-sin, zh, zero], axis=-1)
    ts2 = jnp.concatenate([zh, sin, zero], axis=-1)
    return tuple(jnp.concatenate([t, t], axis=-1) for t in (tc, ts1, ts2))


def _tile_rows(tab, rows):
    return tab if tab.shape[0] >= rows else jnp.tile(tab, (rows // tab.shape[0], 1))


def _swap_halves(x):
    half = x.shape[-1] // 2
    return jnp.concatenate([x[..., half:], x[..., :half]], axis=-1)


def _prep_mla(j, g_attn_norm_i, w_mla_in, g_mla_q_a, w_mla_q_up, g_mla_kv_a, w_mla_kv_up, g_mla_q_nope,
              g_mla_q_rope, g_mla_k_nope, g_mla_k_rope, w_mla_out):
    q_lora, kv_lora = g_mla_q_a.shape[1], g_mla_kv_a.shape[1]
    qk = MLA_NOPE + MLA_ROPE
    heads = w_mla_q_up.shape[2] // qk
    w_in = w_mla_in[j]
    k_r = w_in[:, q_lora + kv_lora:]
    w_in = jnp.concatenate([w_in, _swap_halves(k_r)], axis=-1)
    wq = w_mla_q_up[j].reshape(q_lora, heads, qk)
    rope = wq[:, :, MLA_NOPE:]
    wq = jnp.concatenate([wq[:, :, :MLA_NOPE].reshape(q_lora, heads // 2, 2 * MLA_NOPE),
                          jnp.concatenate([rope, _swap_halves(rope)], axis=-1).reshape(q_lora, heads // 2, 2 * LANES)],
                         axis=-1).reshape(q_lora, heads * 2 * LANES)
    wkv = w_mla_kv_up[j].reshape(kv_lora, heads, MLA_NOPE + MLA_V)
    dup = lambda g: jnp.concatenate([g, _swap_halves(g)], axis=-1)[None, :]
    two = lambda g: jnp.concatenate([g, g], axis=-1)
    gsum = np.kron(np.eye(2), np.ones((LANES, LANES)))
    return dict(
        heads=heads, gsum=jnp.asarray(gsum, BF16),
        g_attn=g_attn_norm_i[None, :],
        w_in=w_in.astype(BF16),
        g_q_a=g_mla_q_a[j][None, :], g_kv_a=g_mla_kv_a[j][None, :],
        g_k_rope=dup(g_mla_k_rope[j]), g_q_rope=two(dup(g_mla_q_rope[j])),
        g_q_nope=two(g_mla_q_nope[j][None, :]), g_k_nope=two(g_mla_k_nope[j][None, :]),
        w_q=wq.astype(BF16),
        w_kn=wkv[:, :, :MLA_NOPE].reshape(kv_lora, heads * MLA_NOPE).astype(BF16),
        w_v=wkv[:, :, MLA_NOPE:].reshape(kv_lora, heads * MLA_V).astype(BF16),
        w_out=w_mla_out[j].astype(BF16),
    )


def _prep_swa(j, g_attn_norm_i, w_swa_qkv, g_swa_q, g_swa_k, swa_sinks, w_swa_out):
    q_heads = swa_sinks.shape[1]
    w = w_swa_qkv[j]
    d = w.shape[0]
    nq = q_heads * SWA_HEAD_DIM
    kv_heads = (w.shape[1] - nq) // (2 * SWA_HEAD_DIM)
    nk = kv_heads * SWA_HEAD_DIM
    wk = w[:, nq:nq + nk].reshape(d, kv_heads, SWA_HEAD_DIM)
    wv = w[:, nq + nk:].reshape(d, kv_heads, SWA_HEAD_DIM)
    wkv = jnp.concatenate([wk, wk, wv, wv], axis=-1).reshape(d, kv_heads * 4 * SWA_HEAD_DIM)
    gmat = np.kron(np.eye(2 * LANES // SWA_HEAD_DIM), np.ones((SWA_HEAD_DIM, SWA_HEAD_DIM)))
    dup = lambda g: jnp.concatenate([g, g], axis=-1)[None, :]
    return dict(
        q_heads=q_heads, kv_heads=kv_heads,
        g_attn=g_attn_norm_i[None, :],
        w_qkv=jnp.concatenate([w[:, :nq], wkv], axis=-1).astype(BF16),
        g_q=dup(g_swa_q[j]), g_k=dup(g_swa_k[j]),
        gmat=jnp.asarray(gmat, BF16),
        sinks=swa_sinks[j],
        w_out=w_swa_out[j].astype(BF16),
    )


def _kv_dup(t):
    b, l, kv, d = t.shape
    t = jnp.transpose(t, (2, 0, 1, 3)).reshape(kv, b * l, d)
    return jnp.concatenate([t, t], axis=-1).astype(BF16)


def _trunk(x, p, pos, caches, layers, ffn_w, ple_w, g_final, tm, tf):
    batch, seq, d = x.shape
    n = batch * seq
    h = x.reshape(n, d)
    rows = max(tm, seq)
    mla_tab = _tile_rows(_mla_table(pos), rows)
    swa_tabs = tuple(_tile_rows(t, rows) for t in _swa_tables(pos))
    depth = len(layers)
    p_rows = p.reshape(depth, n, -1)
    lats, krs, sks, svs = [], [], [], []
    for i, (kind, w) in enumerate(layers):
        j = i // 2
        if kind == 'mla':
            lat, kr, q, k, v = _mla_proj(h, w, mla_tab, tm)
            if caches is None:
                o = _mla_attn(q, k, v, batch, seq, tq=512, hb=4)
            else:
                lat_c, kr_c = caches[0][j], caches[1][j]
                past = lat_c.shape[1]
                k_c, v_c = _mla_expand_call(lat_c.reshape(batch * past, -1),
                                            kr_c.reshape(batch * past, -1), w, tm=512)
                o = _mla_attn_sample(q, k_c, k, v_c, v, batch, past, seq, hb=4)
            lats.append(lat.reshape(batch, seq, -1))
            krs.append(kr.reshape(batch, seq, -1))
        else:
            q, kd, vd, kf, vf = _swa_proj(h, w, swa_tabs, tm)
            kv_heads = w['kv_heads']
            unhead = lambda t, rows: jnp.transpose(
                t.reshape(kv_heads, batch, seq, SWA_HEAD_DIM)[:, :, seq - rows:], (1, 2, 0, 3))
            if caches is None:
                o = _swa_attn(w['sinks'], q, kd, vd, batch, seq, tq=256)
                keep = min(WINDOW, seq)
                sks.append(unhead(kf, keep))
                svs.append(unhead(vf, keep))
            else:
                ck, cv = caches[2][j], caches[3][j]
                keep = ck.shape[1]
                o = _swa_attn_sample(w['sinks'], q, _kv_dup(ck), kd, _kv_dup(cv), vd, batch, keep, seq)
                sks.append(jnp.concatenate([ck, unhead(kf, seq)], axis=1)[:, -keep:])
                svs.append(jnp.concatenate([cv, unhead(vf, seq)], axis=1)[:, -keep:])
        h = _oproj(h, o, w['w_out'], tm)
        g_ffn, wg, wu, wd = ffn_w[i]
        h = _ffn(h, g_ffn, wg, wu, wd, tm, tf)
        g_ple, wgate, wproj = ple_w[i]
        h = _ple(h, g_ple, wgate, p_rows, i, wproj, g_final, i == depth - 1, tm)
    return (h.reshape(batch, seq, d), jnp.stack(lats), jnp.stack(krs), jnp.stack(sks), jnp.stack(svs))


def kernel(x_prompt, x_sample, p_prompt, p_sample, cache_mla_latent, cache_mla_krope, state_swa_k, state_swa_v, g_attn_norm, w_mla_in, g_mla_q_a, w_mla_q_up, g_mla_kv_a, w_mla_kv_up, g_mla_q_nope, g_mla_q_rope, g_mla_k_nope, g_mla_k_rope, w_mla_out, w_swa_qkv, g_swa_q, g_swa_k, swa_sinks, w_swa_out, g_ffn_norm, w_ffn_gate, w_ffn_up, w_ffn_down, g_ple_norm, w_ple_gate, w_ple_proj, g_final):
    depth = g_attn_norm.shape[0]
    layers = []
    for i in range(depth):
        j = i // 2
        if i % 2 == 0:
            layers.append(('mla', _prep_mla(j, g_attn_norm[i], w_mla_in, g_mla_q_a, w_mla_q_up,
                                            g_mla_kv_a, w_mla_kv_up, g_mla_q_nope, g_mla_q_rope,
                                            g_mla_k_nope, g_mla_k_rope, w_mla_out)))
        else:
            layers.append(('swa', _prep_swa(j, g_attn_norm[i], w_swa_qkv, g_swa_q, g_swa_k,
                                            swa_sinks, w_swa_out)))
    ffn_w = [(g_ffn_norm[i][None, :], w_ffn_gate[i].astype(BF16), w_ffn_up[i].astype(BF16),
              w_ffn_down[i].astype(BF16)) for i in range(depth)]
    ple_w = [(g_ple_norm[i][None, :], w_ple_gate[i].astype(BF16), w_ple_proj[i].astype(BF16))
             for i in range(depth)]
    gfin = g_final[None, :]

    seq = x_prompt.shape[1]
    t = x_sample.shape[1]
    past = cache_mla_latent.shape[2]
    pos_p = jnp.arange(seq, dtype=jnp.int32)
    pos_s = past + jnp.arange(t, dtype=jnp.int32)
    y_p, lat_p, kr_p, sk_p, sv_p = _trunk(x_prompt, p_prompt, pos_p, None, layers, ffn_w, ple_w,
                                          gfin, tm=512, tf=512)
    caches = (cache_mla_latent, cache_mla_krope, state_swa_k, state_swa_v)
    n_s = x_sample.shape[0] * t
    y_s, lat_s, kr_s, sk_s, sv_s = _trunk(x_sample, p_sample, pos_s, caches, layers, ffn_w, ple_w,
                                          gfin, tm=n_s, tf=512)
    return (y_p, y_s, lat_p, kr_p, sk_p, sv_p, lat_s, kr_s, sk_s, sv_s)
```

```python
import functools

import numpy as np
import jax
import jax.numpy as jnp
from jax import lax
from jax.experimental import pallas as pl
from jax.experimental.pallas import tpu as pltpu

F32 = jnp.float32
BF16 = jnp.bfloat16

EPS = 1e-6
CHUNK = 64
WINDOW = 128
MLA_THETA = 10000.0
SWA_THETA = 500000.0
MLA_NOPE = 128
MLA_ROPE = 64
MLA_V = 128
SWA_HEAD_DIM = 64
SWA_ROT = SWA_HEAD_DIM // 4

LANES = 128
VMEM_LIMIT = 56 * 1024 * 1024
NEG = -1e30
LOG2E = 1.4426950408889634


def _params(*sem):
    return pltpu.CompilerParams(dimension_semantics=sem, vmem_limit_bytes=VMEM_LIMIT)


def _resident(shape, layer=None):
    if layer is None:
        zeros = (0,) * len(shape)
        return pl.BlockSpec(shape, lambda *_: zeros, pipeline_mode=pl.Buffered(1))
    index = (layer,) + (0,) * (len(shape) - 1)
    return pl.BlockSpec((None,) + tuple(shape[1:]), lambda *_: index, pipeline_mode=pl.Buffered(1))


def _rms(x, g):
    return x * lax.rsqrt(jnp.mean(x * x, axis=-1, keepdims=True) + EPS) * g


def _dot(a, b):
    return jnp.dot(a, b, preferred_element_type=F32)


def _dot_t(a, b):
    return lax.dot_general(a, b, (((1,), (1,)), ((), ())), preferred_element_type=F32)


def _rope_dup(x, g, tab):
    ss = jnp.sum(x * x, axis=-1, keepdims=True)
    y = x * lax.rsqrt(ss * (1.0 / LANES) + EPS) * g
    t = y * tab
    return t + pltpu.roll(t, LANES // 2, axis=1)


def _mla_heads(c_q, latb, krb, tab, wq_ref, wkn_ref, wv_ref, gqn, gqr, gkn, gsum_ref,
               q_ref, k_ref, v_ref, heads):
    gsum = gsum_ref[...]
    pair = 2 * LANES

    def inv_rms(x):
        return lax.rsqrt(_dot((x * x).astype(BF16), gsum) * (1.0 / LANES) + EPS)

    def finish(p, xq, xk, xv):
        kn = xk * inv_rms(xk) * gkn
        if xq is not None:
            qn = xq[:, :pair] * inv_rms(xq[:, :pair]) * gqn
            t = xq[:, pair:] * inv_rms(xq[:, pair:]) * gqr * tab
        for u in range(2):
            h = 2 * p + u
            sl = slice(u * LANES, (u + 1) * LANES)
            k_ref[h, :, 0:MLA_NOPE] = kn[:, sl].astype(BF16)
            k_ref[h, :, MLA_NOPE:MLA_NOPE + MLA_ROPE] = krb
            v_ref[h] = xv[:, sl].astype(BF16)
            if xq is not None:
                q_ref[h, :, 0:MLA_NOPE] = qn[:, sl].astype(BF16)
                tu = t[:, sl]
                qr = tu + pltpu.roll(tu, LANES // 2, axis=1)
                q_ref[h, :, MLA_NOPE:MLA_NOPE + MLA_ROPE] = qr[:, :MLA_ROPE].astype(BF16)

    pending = None
    for p in range(heads // 2):
        xq = None if c_q is None else _dot(c_q, wq_ref[:, p * 2 * pair:(p + 1) * 2 * pair])
        xk = _dot(latb, wkn_ref[:, p * pair:(p + 1) * pair])
        xv = _dot(latb, wv_ref[:, p * pair:(p + 1) * pair])
        if pending is not None:
            finish(*pending)
        pending = (p, xq, xk, xv)
    finish(*pending)


def _mla_proj_kernel(h_ref, gattn_ref, win_ref, gqa_ref, gkva_ref, gkr_ref, tab_ref, wq_ref,
                     gqn_ref, gqr_ref, wkn_ref, wv_ref, gkn_ref, gsum_ref,
                     lat_ref, kr_ref, q_ref, k_ref, v_ref, *, heads, q_lora, kv_lora, scale):
    hn = _rms(h_ref[...], gattn_ref[...]).astype(BF16)
    a = _dot(hn, win_ref[...])
    c_q = _rms(a[:, :q_lora], gqa_ref[...]).astype(BF16)
    lat = _rms(a[:, q_lora:q_lora + kv_lora], gkva_ref[...])
    lat_ref[...] = lat
    tab = tab_ref[...]
    kr = _rope_dup(a[:, q_lora + kv_lora:], gkr_ref[...], tab)
    kr_ref[...] = kr[:, :MLA_ROPE]
    _mla_heads(c_q, lat.astype(BF16), kr[:, :MLA_ROPE].astype(BF16), jnp.concatenate([tab, tab], axis=1),
               wq_ref, wkn_ref, wv_ref, gqn_ref[...] * scale, gqr_ref[...] * scale, gkn_ref[...],
               gsum_ref, q_ref, k_ref, v_ref, heads)


def _mla_expand_kernel(lat_ref, kr_ref, wkn_ref, wv_ref, gkn_ref, gsum_ref, k_ref, v_ref, *, heads):
    _mla_heads(None, lat_ref[...].astype(BF16), kr_ref[...].astype(BF16), None, None, wkn_ref, wv_ref,
               None, None, gkn_ref[...], gsum_ref, None, k_ref, v_ref, heads)


def _mla_proj(h, w, tab, tm):
    n, d = h.shape
    heads = w['heads']
    q_lora, kv_lora = w['g_q_a'].shape[1], w['g_kv_a'].shape[1]
    qk = MLA_NOPE + MLA_ROPE
    nt = tab.shape[0] // tm
    row = lambda i: (i, 0)
    hrow = lambda i: (0, i, 0)
    kern = functools.partial(_mla_proj_kernel, heads=heads, q_lora=q_lora, kv_lora=kv_lora,
                             scale=qk ** -0.5 * LOG2E)
    return pl.pallas_call(
        kern,
        grid=(n // tm,),
        in_specs=[
            pl.BlockSpec((tm, d), row),
            _resident(w['g_attn'].shape), _resident(w['w_in'].shape),
            _resident(w['g_q_a'].shape), _resident(w['g_kv_a'].shape), _resident(w['g_k_rope'].shape),
            pl.BlockSpec((tm, LANES), lambda i: (i % nt, 0)),
            _resident(w['w_q'].shape), _resident(w['g_q_nope'].shape), _resident(w['g_q_rope'].shape),
            _resident(w['w_kn'].shape), _resident(w['w_v'].shape), _resident(w['g_k_nope'].shape),
            _resident(w['gsum'].shape),
        ],
        out_specs=[
            pl.BlockSpec((tm, kv_lora), row),
            pl.BlockSpec((tm, MLA_ROPE), row),
            pl.BlockSpec((heads, tm, qk), hrow),
            pl.BlockSpec((heads, tm, qk), hrow),
            pl.BlockSpec((heads, tm, MLA_V), hrow),
        ],
        out_shape=[
            jax.ShapeDtypeStruct((n, kv_lora), F32),
            jax.ShapeDtypeStruct((n, MLA_ROPE), F32),
            jax.ShapeDtypeStruct((heads, n, qk), BF16),
            jax.ShapeDtypeStruct((heads, n, qk), BF16),
            jax.ShapeDtypeStruct((heads, n, MLA_V), BF16),
        ],
        compiler_params=_params("parallel"),
        name="mla_proj",
    )(h, w['g_attn'], w['w_in'], w['g_q_a'], w['g_kv_a'], w['g_k_rope'], tab, w['w_q'],
      w['g_q_nope'], w['g_q_rope'], w['w_kn'], w['w_v'], w['g_k_nope'], w['gsum'])


def _mla_expand_call(lat, kr, layer, w, tm):
    _, n, kv_lora = lat.shape
    heads = w['heads']
    qk = MLA_NOPE + MLA_ROPE
    hrow = lambda i: (0, i, 0)
    return pl.pallas_call(
        functools.partial(_mla_expand_kernel, heads=heads),
        grid=(n // tm,),
        in_specs=[pl.BlockSpec((None, tm, kv_lora), lambda i: (layer, i, 0)),
                  pl.BlockSpec((None, tm, MLA_ROPE), lambda i: (layer, i, 0)),
                  _resident(w['w_kn'].shape), _resident(w['w_v'].shape),
                  _resident(w['g_k_nope'].shape), _resident(w['gsum'].shape)],
        out_specs=[pl.BlockSpec((heads, tm, qk), hrow), pl.BlockSpec((heads, tm, MLA_V), hrow)],
        out_shape=[jax.ShapeDtypeStruct((heads, n, qk), BF16),
                   jax.ShapeDtypeStruct((heads, n, MLA_V), BF16)],
        compiler_params=_params("parallel"),
        name="mla_expand",
    )(lat, kr, w['w_kn'], w['w_v'], w['g_k_nope'], w['gsum'])


def _mla_attn_kernel(q_ref, k_ref, v_ref, o_ref, s_ref, *, hb, tq):
    i = pl.program_id(2)
    kc = lax.broadcasted_iota(jnp.int32, (tq, tq), 0) // CHUNK
    qc = lax.broadcasted_iota(jnp.int32, (tq, tq), 1) // CHUNK
    diag_bias = jnp.where(kc <= qc, 0.0, NEG).astype(F32)
    qs = [q_ref[h] for h in range(hb)]

    def scores(j, slot):
        start = pl.multiple_of(j * tq, tq)
        for h in range(hb):
            s_ref[slot, h] = _dot_t(k_ref[h, pl.ds(start, tq), :], qs[h])

    def update(j, slot, carry, bias=None):
        start = pl.multiple_of(j * tq, tq)
        ps, stats = [], []
        for h in range(hb):
            m, l, _ = carry[h]
            s = s_ref[slot, h] if bias is None else s_ref[slot, h] + bias
            m_new = jnp.maximum(m, jnp.max(s, axis=0, keepdims=True))
            alpha = jnp.exp2(m - m_new)
            p = jnp.exp2(s - m_new)
            stats.append((m_new, alpha * l + jnp.sum(p, axis=0, keepdims=True), alpha))
            ps.append(p.astype(BF16))
        out = []
        for h in range(hb):
            pv = lax.dot_general(v_ref[h, pl.ds(start, tq), :], ps[h],
                                 (((0,), (0,)), ((), ())), preferred_element_type=F32)
            m_new, l, alpha = stats[h]
            out.append((m_new, l, alpha * carry[h][2] + pv))
        return tuple(out)

    def finish(carry):
        for h in range(hb):
            _, l, acc = carry[h]
            o_ref[:, h * MLA_V:(h + 1) * MLA_V] = (acc / l).T.astype(BF16)

    def pair(t, carry):
        scores(2 * t + 1, 1)
        carry = update(2 * t, 0, carry)
        scores(2 * t + 2, 0)
        return update(2 * t + 1, 1, carry)

    init = tuple((jnp.full((1, tq), NEG, F32), jnp.zeros((1, tq), F32), jnp.zeros((MLA_V, tq), F32))
                 for _ in range(hb))
    scores(0, 0)
    carry = lax.fori_loop(0, i // 2, pair, init)

    @pl.when(i % 2 == 0)
    def _():
        finish(update(i, 0, carry, diag_bias))

    @pl.when(i % 2 == 1)
    def _():
        scores(i, 1)
        finish(update(i, 1, update(i - 1, 0, carry), diag_bias))


def _mla_attn(q, k, v, batch, seq, tq, hb):
    heads, n, qk = q.shape
    nq = seq // tq
    kv_map = lambda b, g, i: (g, b, 0)
    return pl.pallas_call(
        functools.partial(_mla_attn_kernel, hb=hb, tq=tq),
        grid=(batch, heads // hb, nq),
        in_specs=[pl.BlockSpec((hb, tq, qk), lambda b, g, i: (g, b * nq + i, 0)),
                  pl.BlockSpec((hb, seq, qk), kv_map),
                  pl.BlockSpec((hb, seq, MLA_V), kv_map)],
        out_specs=pl.BlockSpec((tq, hb * MLA_V), lambda b, g, i: (b * nq + i, g)),
        out_shape=jax.ShapeDtypeStruct((n, heads * MLA_V), BF16),
        scratch_shapes=[pltpu.VMEM((2, hb, tq, tq), F32)],
        compiler_params=_params("parallel", "parallel", "arbitrary"),
        name="mla_attn",
    )(q, k, v)


def _mla_attn_sample_kernel(q_ref, kc_ref, kn_ref, vc_ref, vn_ref, o_ref, *, hb):
    ss = [(_dot_t(kc_ref[h], q_ref[h]), _dot_t(kn_ref[h], q_ref[h])) for h in range(hb)]
    ps = []
    for s_c, s_n in ss:
        m = jnp.maximum(jnp.max(s_c, axis=0, keepdims=True), jnp.max(s_n, axis=0, keepdims=True))
        p_c = jnp.exp2(s_c - m)
        p_n = jnp.exp2(s_n - m)
        l = jnp.sum(p_c, axis=0, keepdims=True) + jnp.sum(p_n, axis=0, keepdims=True)
        ps.append((p_c.astype(BF16), p_n.astype(BF16), l))
    tn = (((0,), (0,)), ((), ()))
    for h, (p_c, p_n, l) in enumerate(ps):
        acc = (lax.dot_general(vc_ref[h], p_c, tn, preferred_element_type=F32)
               + lax.dot_general(vn_ref[h], p_n, tn, preferred_element_type=F32))
        o_ref[:, h * MLA_V:(h + 1) * MLA_V] = (acc / l).T.astype(BF16)


def _mla_attn_sample(q, k_cache, k_new, v_cache, v_new, batch, past, t, hb):
    heads, n, qk = q.shape
    q_pos = past + np.arange(t)
    k_pos = np.arange(past + t)
    assert np.all((k_pos // CHUNK)[None, :] <= (q_pos // CHUNK)[:, None])
    m3 = lambda b, g: (g, b, 0)
    return pl.pallas_call(
        functools.partial(_mla_attn_sample_kernel, hb=hb),
        grid=(batch, heads // hb),
        in_specs=[pl.BlockSpec((hb, t, qk), m3), pl.BlockSpec((hb, past, qk), m3),
                  pl.BlockSpec((hb, t, qk), m3), pl.BlockSpec((hb, past, MLA_V), m3),
                  pl.BlockSpec((hb, t, MLA_V), m3)],
        out_specs=pl.BlockSpec((t, hb * MLA_V), lambda b, g: (b, g)),
        out_shape=jax.ShapeDtypeStruct((n, heads * MLA_V), BF16),
        compiler_params=_params("parallel", "parallel"),
        name="mla_attn_sample",
    )(q, k_cache, k_new, v_cache, v_new)


def _swa_proj_kernel(h_ref, gattn_ref, w_ref, gq_ref, gk_ref, tc_ref, ts1_ref, ts2_ref, gmat_ref,
                     q_ref, kd_ref, vd_ref, kf_ref, vf_ref, *, q_cols, kv_heads, scale):
    hn = _rms(h_ref[...], gattn_ref[...]).astype(BF16)
    tc, ts1, ts2 = tc_ref[...], ts1_ref[...], ts2_ref[...]
    gmat = gmat_ref[...]

    def rope(x, ss, g):
        y = x * lax.rsqrt(ss * (1.0 / SWA_HEAD_DIM) + EPS) * g
        return (y * tc + pltpu.roll(y, LANES - SWA_ROT // 2, axis=1) * ts1
                + pltpu.roll(y, SWA_ROT // 2, axis=1) * ts2)

    gq = gq_ref[...] * scale
    gk = gk_ref[...]
    n_q = q_cols // 2

    def finish(c, x2):
        ss = _dot((x2 * x2).astype(BF16), gmat)
        if c < n_q:
            for u in range(2):
                col = 2 * c + u
                sl = slice(u * LANES, (u + 1) * LANES)
                q_ref[:, col * LANES:(col + 1) * LANES] = rope(x2[:, sl], ss[:, sl], gq).astype(BF16)
        else:
            kh = c - n_q
            kd = rope(x2[:, :LANES], ss[:, :LANES], gk)
            kd_ref[kh] = kd.astype(BF16)
            kf_ref[kh] = kd[:, :SWA_HEAD_DIM]
            vd_ref[kh] = x2[:, LANES:].astype(BF16)
            vf_ref[kh] = x2[:, LANES:LANES + SWA_HEAD_DIM]

    pending = None
    for c in range(n_q + kv_heads):
        x2 = _dot(hn, w_ref[:, c * 256:(c + 1) * 256])
        if pending is not None:
            finish(*pending)
        pending = (c, x2)
    finish(*pending)


def _swa_proj(h, w, tabs, tm):
    n, d = h.shape
    q_heads, kv_heads = w['q_heads'], w['kv_heads']
    q_cols = q_heads * SWA_HEAD_DIM // LANES
    nt = tabs[0].shape[0] // tm
    row = lambda i: (i, 0)
    hrow = lambda i: (0, i, 0)
    tspec = pl.BlockSpec((tm, LANES), lambda i: (i % nt, 0))
    kern = functools.partial(_swa_proj_kernel, q_cols=q_cols, kv_heads=kv_heads,
                             scale=SWA_HEAD_DIM ** -0.5 * LOG2E)
    return pl.pallas_call(
        kern,
        grid=(n // tm,),
        in_specs=[pl.BlockSpec((tm, d), row), _resident(w['g_attn'].shape),
                  _resident(w['w_qkv'].shape), _resident(w['g_q'].shape), _resident(w['g_k'].shape),
                  tspec, tspec, tspec, _resident(w['gmat'].shape)],
        out_specs=[pl.BlockSpec((tm, q_cols * LANES), row),
                   pl.BlockSpec((kv_heads, tm, LANES), hrow),
                   pl.BlockSpec((kv_heads, tm, LANES), hrow),
                   pl.BlockSpec((kv_heads, tm, SWA_HEAD_DIM), hrow),
                   pl.BlockSpec((kv_heads, tm, SWA_HEAD_DIM), hrow)],
        out_shape=[jax.ShapeDtypeStruct((n, q_cols * LANES), BF16),
                   jax.ShapeDtypeStruct((kv_heads, n, LANES), BF16),
                   jax.ShapeDtypeStruct((kv_heads, n, LANES), BF16),
                   jax.ShapeDtypeStruct((kv_heads, n, SWA_HEAD_DIM), F32),
                   jax.ShapeDtypeStruct((kv_heads, n, SWA_HEAD_DIM), F32)],
        compiler_params=_params("parallel"),
        name="swa_proj",
    )(h, w['g_attn'], w['w_qkv'], w['g_q'], w['g_k'], *tabs, w['gmat'])


def _swa_attn_kernel(sink_ref, q_ref, kp_ref, kc_ref, vp_ref, vc_ref, o_ref, *,
                     kv_heads, group, banded):
    tq = q_ref.shape[0]
    n_p, n_c = kp_ref.shape[1], kc_ref.shape[1]
    n_k = n_p + n_c
    lane = lax.broadcasted_iota(jnp.int32, (1, LANES), 1)
    row = lax.broadcasted_iota(jnp.int32, (LANES, 1), 0)
    bias = None
    if banded:
        win = WINDOW // CHUNK
        kc = lax.broadcasted_iota(jnp.int32, (n_k, tq), 0) // CHUNK - n_p // CHUNK
        qc = lax.broadcasted_iota(jnp.int32, (n_k, tq), 1) // CHUNK
        ok = (kc <= qc) & (kc >= qc - win) & ((kc >= 0) | (pl.program_id(1) > 0))
        bias = jnp.where(ok, 0.0, NEG).astype(F32)
    zero = jnp.zeros((), BF16)
    half = group // 2
    for kh in range(kv_heads):
        k_all = jnp.concatenate([kp_ref[kh], kc_ref[kh]], axis=0)
        v_t = jnp.concatenate([vp_ref[kh], vc_ref[kh]], axis=0).T
        ks = (jnp.where(lane < SWA_HEAD_DIM, k_all, zero), jnp.where(lane < SWA_HEAD_DIM, zero, k_all))
        vs = (jnp.where(row < SWA_HEAD_DIM, v_t, zero), jnp.where(row < SWA_HEAD_DIM, zero, v_t))
        cols = [q_ref[:, (kh * half + c) * LANES:(kh * half + c + 1) * LANES] for c in range(half)]
        ss = [_dot_t(ks[u], cols[c]) for c in range(half) for u in range(2)]
        es, rdens = [], []
        for c in range(half):
            for u in range(2):
                s = ss[2 * c + u]
                if banded:
                    s = s + bias
                sink = sink_ref[2 * (kh * half + c) + u] * LOG2E
                m = jnp.maximum(jnp.max(s, axis=0, keepdims=True), sink)
                e = jnp.exp2(s - m)
                rdens.append(1.0 / (jnp.sum(e, axis=0, keepdims=True) + jnp.exp2(sink - m)))
                es.append(e.astype(BF16))
        for c in range(half):
            o_t = (_dot(vs[0], es[2 * c]) * rdens[2 * c]
                   + _dot(vs[1], es[2 * c + 1]) * rdens[2 * c + 1])
            col = kh * half + c
            o_ref[:, col * LANES:(col + 1) * LANES] = o_t.T.astype(BF16)


def _swa_attn(sinks, q, kd, vd, batch, seq, tq):
    n, dq = q.shape
    kv_heads = kd.shape[0]
    group = (dq // SWA_HEAD_DIM) // kv_heads
    nq = seq // tq
    per = tq // WINDOW
    prev = lambda b, i: (0, jnp.maximum(i * per - 1, 0) + b * nq * per, 0)
    cur = lambda b, i: (0, b * nq + i, 0)
    return pl.pallas_call(
        functools.partial(_swa_attn_kernel, kv_heads=kv_heads, group=group, banded=True),
        grid=(batch, nq),
        in_specs=[pl.BlockSpec(memory_space=pltpu.SMEM),
                  pl.BlockSpec((tq, dq), lambda b, i: (b * nq + i, 0)),
                  pl.BlockSpec((kv_heads, WINDOW, LANES), prev),
                  pl.BlockSpec((kv_heads, tq, LANES), cur),
                  pl.BlockSpec((kv_heads, WINDOW, LANES), prev),
                  pl.BlockSpec((kv_heads, tq, LANES), cur)],
        out_specs=pl.BlockSpec((tq, dq), lambda b, i: (b * nq + i, 0)),
        out_shape=jax.ShapeDtypeStruct((n, dq), BF16),
        compiler_params=_params("parallel", "arbitrary"),
        name="swa_attn",
    )(sinks, q, kd, kd, vd, vd)


def _swa_attn_sample(sinks, q, kd_cache, kd_new, vd_cache, vd_new, batch, keep, t):
    n, dq = q.shape
    kv_heads = kd_new.shape[0]
    group = (dq // SWA_HEAD_DIM) // kv_heads
    blk = lambda b: (0, b, 0)
    return pl.pallas_call(
        functools.partial(_swa_attn_kernel, kv_heads=kv_heads, group=group, banded=False),
        grid=(batch,),
        in_specs=[pl.BlockSpec(memory_space=pltpu.SMEM),
                  pl.BlockSpec((t, dq), lambda b: (b, 0)),
                  pl.BlockSpec((kv_heads, keep, LANES), blk),
                  pl.BlockSpec((kv_heads, t, LANES), blk),
                  pl.BlockSpec((kv_heads, keep, LANES), blk),
                  pl.BlockSpec((kv_heads, t, LANES), blk)],
        out_specs=pl.BlockSpec((t, dq), lambda b: (b, 0)),
        out_shape=jax.ShapeDtypeStruct((n, dq), BF16),
        compiler_params=_params("parallel"),
        name="swa_attn_sample",
    )(sinks, q, kd_cache, kd_new, vd_cache, vd_new)


def _oproj_kernel(h_ref, o_ref, w_ref, out_ref):
    out_ref[...] = h_ref[...] + _dot(o_ref[...], w_ref[...])


def _oproj(h, o, w, layer, tm):
    n, d = h.shape
    row = lambda i: (i, 0)
    return pl.pallas_call(
        _oproj_kernel,
        grid=(n // tm,),
        in_specs=[pl.BlockSpec((tm, d), row), pl.BlockSpec((tm, o.shape[1]), row),
                  _resident(w.shape, layer)],
        out_specs=pl.BlockSpec((tm, d), row),
        out_shape=jax.ShapeDtypeStruct((n, d), F32),
        compiler_params=_params("parallel"),
        name="attn_out_proj",
    )(h, o, w)


def _ffn_kernel(h_ref, g_ref, wg_ref, wu_ref, wd_ref, out_ref, hn_ref):
    @pl.when(pl.program_id(1) == 0)
    def _():
        x = h_ref[...]
        hn_ref[...] = _rms(x, g_ref[...]).astype(BF16)
        out_ref[...] = x

    hn = hn_ref[...]
    a = _dot(hn, wg_ref[...])
    b = _dot(hn, wu_ref[...])
    act = (a * jax.nn.sigmoid(a) * b).astype(BF16)
    out_ref[...] += _dot(act, wd_ref[...])


def _ffn(h, g, wg, wu, wd, layer, tm, tf):
    n, d = h.shape
    f = wg.shape[2]
    row = lambda i, j: (i, 0)
    return pl.pallas_call(
        _ffn_kernel,
        grid=(n // tm, f // tf),
        in_specs=[pl.BlockSpec((tm, d), row), pl.BlockSpec((None, 1, d), lambda i, j: (layer, 0, 0)),
                  pl.BlockSpec((None, d, tf), lambda i, j: (layer, 0, j)),
                  pl.BlockSpec((None, d, tf), lambda i, j: (layer, 0, j)),
                  pl.BlockSpec((None, tf, d), lambda i, j: (layer, j, 0))],
        out_specs=pl.BlockSpec((tm, d), row),
        out_shape=jax.ShapeDtypeStruct((n, d), F32),
        scratch_shapes=[pltpu.VMEM((tm, d), BF16)],
        compiler_params=_params("parallel", "arbitrary"),
        name="swiglu_ffn",
    )(h, g, wg, wu, wd)


def _ple_kernel(h_ref, g_ref, wgate_ref, p_ref, wproj_ref, gfin_ref, out_ref, *, final):
    x = h_ref[...]
    gate = jax.nn.sigmoid(_dot(_rms(x, g_ref[...]).astype(BF16), wgate_ref[...]))
    y = x + gate * _dot(p_ref[...].astype(BF16), wproj_ref[...])
    if final:
        y = _rms(y, gfin_ref[...])
    out_ref[...] = y


def _ple(h, g, wgate, p, layer, wproj, gfin, final, tm):
    n, d = h.shape
    row = lambda i: (i, 0)
    return pl.pallas_call(
        functools.partial(_ple_kernel, final=final),
        grid=(n // tm,),
        in_specs=[pl.BlockSpec((tm, d), row), _resident(g.shape, layer), _resident(wgate.shape, layer),
                  pl.BlockSpec((None, tm, p.shape[2]), lambda i: (layer, i, 0)),
                  _resident(wproj.shape, layer), _resident(gfin.shape)],
        out_specs=pl.BlockSpec((tm, d), row),
        out_shape=jax.ShapeDtypeStruct((n, d), F32),
        compiler_params=_params("parallel"),
        name="ple_embed",
    )(h, g, wgate, p, wproj, gfin)


def _angles(pos, r, theta):
    inv = jnp.power(jnp.float32(theta), -jnp.arange(0, r, 2, dtype=jnp.float32) / r)
    ang = pos.astype(jnp.float32)[:, None] * inv[None, :]
    return jnp.cos(ang), jnp.sin(ang)


def _mla_table(pos):
    cos, sin = _angles(pos, MLA_ROPE, MLA_THETA)
    return jnp.concatenate([cos, cos, -sin, sin], axis=-1)


def _swa_tables(pos):
    cos, sin = _angles(pos, SWA_ROT, SWA_THETA)
    s = cos.shape[0]
    rest = SWA_HEAD_DIM - SWA_ROT
    one = jnp.ones((s, rest), F32)
    zero = jnp.zeros((s, rest), F32)
    zh = jnp.zeros_like(sin)
    tc = jnp.concatenate([cos, cos, one], axis=-1)
    ts1 = jnp.concatenate([-sin, zh, zero], axis=-1)
    ts2 = jnp.concatenate([zh, sin, zero], axis=-1)
    return tuple(jnp.concatenate([t, t], axis=-1) for t in (tc, ts1, ts2))


def _tile_rows(tab, rows):
    return tab if tab.shape[0] >= rows else jnp.tile(tab, (rows // tab.shape[0], 1))


def _swap_halves(x):
    half = x.shape[-1] // 2
    return jnp.concatenate([x[..., half:], x[..., :half]], axis=-1)


def _prep_mla(j, g_attn_norm_i, w_mla_in, g_mla_q_a, w_mla_q_up, g_mla_kv_a, w_mla_kv_up, g_mla_q_nope,
              g_mla_q_rope, g_mla_k_nope, g_mla_k_rope, w_mla_out):
    q_lora, kv_lora = g_mla_q_a.shape[1], g_mla_kv_a.shape[1]
    qk = MLA_NOPE + MLA_ROPE
    heads = w_mla_q_up.shape[2] // qk
    w_in = w_mla_in[j]
    k_r = w_in[:, q_lora + kv_lora:]
    w_in = jnp.concatenate([w_in, _swap_halves(k_r)], axis=-1)
    wq = w_mla_q_up[j].reshape(q_lora, heads, qk)
    rope = wq[:, :, MLA_NOPE:]
    wq = jnp.concatenate([wq[:, :, :MLA_NOPE].reshape(q_lora, heads // 2, 2 * MLA_NOPE),
                          jnp.concatenate([rope, _swap_halves(rope)], axis=-1).reshape(q_lora, heads // 2, 2 * LANES)],
                         axis=-1).reshape(q_lora, heads * 2 * LANES)
    wkv = w_mla_kv_up[j].reshape(kv_lora, heads, MLA_NOPE + MLA_V)
    dup = lambda g: jnp.concatenate([g, _swap_halves(g)], axis=-1)[None, :]
    two = lambda g: jnp.concatenate([g, g], axis=-1)
    gsum = np.kron(np.eye(2), np.ones((LANES, LANES)))
    return dict(
        heads=heads, gsum=jnp.asarray(gsum, BF16),
        g_attn=g_attn_norm_i[None, :],
        w_in=w_in.astype(BF16),
        g_q_a=g_mla_q_a[j][None, :], g_kv_a=g_mla_kv_a[j][None, :],
        g_k_rope=dup(g_mla_k_rope[j]), g_q_rope=two(dup(g_mla_q_rope[j])),
        g_q_nope=two(g_mla_q_nope[j][None, :]), g_k_nope=two(g_mla_k_nope[j][None, :]),
        w_q=wq.astype(BF16),
        w_kn=wkv[:, :, :MLA_NOPE].reshape(kv_lora, heads * MLA_NOPE).astype(BF16),
        w_v=wkv[:, :, MLA_NOPE:].reshape(kv_lora, heads * MLA_V).astype(BF16),
        w_out=w_mla_out.astype(BF16),
    )


def _prep_swa(j, g_attn_norm_i, w_swa_qkv, g_swa_q, g_swa_k, swa_sinks, w_swa_out):
    q_heads = swa_sinks.shape[1]
    w = w_swa_qkv[j]
    d = w.shape[0]
    nq = q_heads * SWA_HEAD_DIM
    kv_heads = (w.shape[1] - nq) // (2 * SWA_HEAD_DIM)
    nk = kv_heads * SWA_HEAD_DIM
    wk = w[:, nq:nq + nk].reshape(d, kv_heads, SWA_HEAD_DIM)
    wv = w[:, nq + nk:].reshape(d, kv_heads, SWA_HEAD_DIM)
    wkv = jnp.concatenate([wk, wk, wv, wv], axis=-1).reshape(d, kv_heads * 4 * SWA_HEAD_DIM)
    gmat = np.kron(np.eye(2 * LANES // SWA_HEAD_DIM), np.ones((SWA_HEAD_DIM, SWA_HEAD_DIM)))
    dup = lambda g: jnp.concatenate([g, g], axis=-1)[None, :]
    return dict(
        q_heads=q_heads, kv_heads=kv_heads,
        g_attn=g_attn_norm_i[None, :],
        w_qkv=jnp.concatenate([w[:, :nq], wkv], axis=-1).astype(BF16),
        g_q=dup(g_swa_q[j]), g_k=dup(g_swa_k[j]),
        gmat=jnp.asarray(gmat, BF16),
        sinks=swa_sinks[j],
        w_out=w_swa_out.astype(BF16),
    )


def _kv_dup(t):
    b, l, kv, d = t.shape
    t = jnp.transpose(t, (2, 0, 1, 3)).reshape(kv, b * l, d)
    return jnp.concatenate([t, t], axis=-1).astype(BF16)


def _trunk(x, p, pos, caches, layers, ffn_w, ple_w, g_final, tm, tf):
    batch, seq, d = x.shape
    n = batch * seq
    h = x.reshape(n, d)
    rows = max(tm, seq)
    mla_tab = _tile_rows(_mla_table(pos), rows)
    swa_tabs = tuple(_tile_rows(t, rows) for t in _swa_tables(pos))
    depth = len(layers)
    p_rows = p.reshape(depth, n, -1)
    lats, krs, sks, svs = [], [], [], []
    for i, (kind, w) in enumerate(layers):
        j = i // 2
        if kind == 'mla':
            lat, kr, q, k, v = _mla_proj(h, w, mla_tab, tm)
            if caches is None:
                o = _mla_attn(q, k, v, batch, seq, tq=512, hb=4)
            else:
                lat_c, kr_c = caches[0], caches[1]
                past = lat_c.shape[2]
                k_c, v_c = _mla_expand_call(lat_c.reshape(lat_c.shape[0], batch * past, -1),
                                            kr_c.reshape(kr_c.shape[0], batch * past, -1), j, w, tm=512)
                o = _mla_attn_sample(q, k_c, k, v_c, v, batch, past, seq, hb=4)
            lats.append(lat.reshape(batch, seq, -1))
            krs.append(kr.reshape(batch, seq, -1))
        else:
            q, kd, vd, kf, vf = _swa_proj(h, w, swa_tabs, tm)
            kv_heads = w['kv_heads']
            unhead = lambda t, rows: jnp.transpose(
                t.reshape(kv_heads, batch, seq, SWA_HEAD_DIM)[:, :, seq - rows:], (1, 2, 0, 3))
            if caches is None:
                o = _swa_attn(w['sinks'], q, kd, vd, batch, seq, tq=256)
                keep = min(WINDOW, seq)
                sks.append(unhead(kf, keep))
                svs.append(unhead(vf, keep))
            else:
                ck, cv = caches[2][j], caches[3][j]
                keep = ck.shape[1]
                o = _swa_attn_sample(w['sinks'], q, _kv_dup(ck), kd, _kv_dup(cv), vd, batch, keep, seq)
                sks.append(jnp.concatenate([ck, unhead(kf, seq)], axis=1)[:, -keep:])
                svs.append(jnp.concatenate([cv, unhead(vf, seq)], axis=1)[:, -keep:])
        h = _oproj(h, o, w['w_out'], j, tm)
        h = _ffn(h, *ffn_w, i, min(n, 2 * tm), tf)
        g_ple, wgate, wproj = ple_w
        h = _ple(h, g_ple, wgate, p_rows, i, wproj, g_final, i == depth - 1, tm)
    return (h.reshape(batch, seq, d), jnp.stack(lats), jnp.stack(krs), jnp.stack(sks), jnp.stack(svs))


def kernel(x_prompt, x_sample, p_prompt, p_sample, cache_mla_latent, cache_mla_krope, state_swa_k, state_swa_v, g_attn_norm, w_mla_in, g_mla_q_a, w_mla_q_up, g_mla_kv_a, w_mla_kv_up, g_mla_q_nope, g_mla_q_rope, g_mla_k_nope, g_mla_k_rope, w_mla_out, w_swa_qkv, g_swa_q, g_swa_k, swa_sinks, w_swa_out, g_ffn_norm, w_ffn_gate, w_ffn_up, w_ffn_down, g_ple_norm, w_ple_gate, w_ple_proj, g_final):
    depth = g_attn_norm.shape[0]
    layers = []
    for i in range(depth):
        j = i // 2
        if i % 2 == 0:
            layers.append(('mla', _prep_mla(j, g_attn_norm[i], w_mla_in, g_mla_q_a, w_mla_q_up,
                                            g_mla_kv_a, w_mla_kv_up, g_mla_q_nope, g_mla_q_rope,
                                            g_mla_k_nope, g_mla_k_rope, w_mla_out)))
        else:
            layers.append(('swa', _prep_swa(j, g_attn_norm[i], w_swa_qkv, g_swa_q, g_swa_k,
                                            swa_sinks, w_swa_out)))
    ffn_w = (g_ffn_norm[:, None, :], w_ffn_gate.astype(BF16), w_ffn_up.astype(BF16),
             w_ffn_down.astype(BF16))
    ple_w = (g_ple_norm[:, None, :], w_ple_gate.astype(BF16), w_ple_proj.astype(BF16))
    gfin = g_final[None, :]

    seq = x_prompt.shape[1]
    t = x_sample.shape[1]
    past = cache_mla_latent.shape[2]
    pos_p = jnp.arange(seq, dtype=jnp.int32)
    pos_s = past + jnp.arange(t, dtype=jnp.int32)
    y_p, lat_p, kr_p, sk_p, sv_p = _trunk(x_prompt, p_prompt, pos_p, None, layers, ffn_w, ple_w,
                                          gfin, tm=512, tf=512)
    caches = (cache_mla_latent, cache_mla_krope, state_swa_k, state_swa_v)
    n_s = x_sample.shape[0] * t
    y_s, lat_s, kr_s, sk_s, sv_s = _trunk(x_sample, p_sample, pos_s, caches, layers, ffn_w, ple_w,
                                          gfin, tm=n_s, tf=512)
    return (y_p, y_s, lat_p, kr_p, sk_p, sv_p, lat_s, kr_s, sk_s, sv_s)
```

```python
import functools

import numpy as np
import jax
import jax.numpy as jnp
from jax import lax
from jax.experimental import pallas as pl
from jax.experimental.pallas import tpu as pltpu

F32 = jnp.float32
BF16 = jnp.bfloat16

EPS = 1e-6
CHUNK = 64
WINDOW = 128
MLA_THETA = 10000.0
SWA_THETA = 500000.0
MLA_NOPE = 128
MLA_ROPE = 64
MLA_V = 128
SWA_HEAD_DIM = 64
SWA_ROT = SWA_HEAD_DIM // 4

LANES = 128
VMEM_LIMIT = 56 * 1024 * 1024
NEG = -1e30
LOG2E = 1.4426950408889634


def _params(*sem):
    return pltpu.CompilerParams(dimension_semantics=sem, vmem_limit_bytes=VMEM_LIMIT)


def _resident(shape, layer=None):
    if layer is None:
        zeros = (0,) * len(shape)
        return pl.BlockSpec(shape, lambda *_: zeros, pipeline_mode=pl.Buffered(1))
    index = (layer,) + (0,) * (len(shape) - 1)
    return pl.BlockSpec((None,) + tuple(shape[1:]), lambda *_: index, pipeline_mode=pl.Buffered(1))


def _rms(x, g):
    return x * lax.rsqrt(jnp.mean(x * x, axis=-1, keepdims=True) + EPS) * g


def _dot(a, b):
    return jnp.dot(a, b, preferred_element_type=F32)


def _dot_t(a, b):
    return lax.dot_general(a, b, (((1,), (1,)), ((), ())), preferred_element_type=F32)


def _rope_dup(x, g, tab):
    ss = jnp.sum(x * x, axis=-1, keepdims=True)
    y = x * lax.rsqrt(ss * (1.0 / LANES) + EPS) * g
    t = y * tab
    return t + pltpu.roll(t, LANES // 2, axis=1)


def _mla_heads(c_q, latb, krb, tab, wq_ref, wkn_ref, wv_ref, gqn, gqr, gkn, gsum_ref,
               q_ref, k_ref, v_ref, heads):
    gsum = gsum_ref[...]
    pair = 2 * LANES

    def inv_rms(x):
        return lax.rsqrt(_dot((x * x).astype(BF16), gsum) * (1.0 / LANES) + EPS)

    def finish(p, xq, xk, xv):
        kn = xk * inv_rms(xk) * gkn
        if xq is not None:
            qn = xq[:, :pair] * inv_rms(xq[:, :pair]) * gqn
            t = xq[:, pair:] * inv_rms(xq[:, pair:]) * gqr * tab
        for u in range(2):
            h = 2 * p + u
            sl = slice(u * LANES, (u + 1) * LANES)
            k_ref[h, :, 0:MLA_NOPE] = kn[:, sl].astype(BF16)
            k_ref[h, :, MLA_NOPE:MLA_NOPE + MLA_ROPE] = krb
            v_ref[h] = xv[:, sl].astype(BF16)
            if xq is not None:
                q_ref[h, :, 0:MLA_NOPE] = qn[:, sl].astype(BF16)
                tu = t[:, sl]
                qr = tu + pltpu.roll(tu, LANES // 2, axis=1)
                q_ref[h, :, MLA_NOPE:MLA_NOPE + MLA_ROPE] = qr[:, :MLA_ROPE].astype(BF16)

    pending = None
    for p in range(heads // 2):
        xq = None if c_q is None else _dot(c_q, wq_ref[:, p * 2 * pair:(p + 1) * 2 * pair])
        xk = _dot(latb, wkn_ref[:, p * pair:(p + 1) * pair])
        xv = _dot(latb, wv_ref[:, p * pair:(p + 1) * pair])
        if pending is not None:
            finish(*pending)
        pending = (p, xq, xk, xv)
    finish(*pending)


def _mla_proj_kernel(h_ref, gattn_ref, win_ref, gqa_ref, gkva_ref, gkr_ref, tab_ref, wq_ref,
                     gqn_ref, gqr_ref, wkn_ref, wv_ref, gkn_ref, gsum_ref,
                     lat_ref, kr_ref, q_ref, k_ref, v_ref, *, heads, q_lora, kv_lora, scale):
    hn = _rms(h_ref[...], gattn_ref[...]).astype(BF16)
    a = _dot(hn, win_ref[...])
    c_q = _rms(a[:, :q_lora], gqa_ref[...]).astype(BF16)
    lat = _rms(a[:, q_lora:q_lora + kv_lora], gkva_ref[...])
    lat_ref[...] = lat
    tab = tab_ref[...]
    kr = _rope_dup(a[:, q_lora + kv_lora:], gkr_ref[...], tab)
    kr_ref[...] = kr[:, :MLA_ROPE]
    _mla_heads(c_q, lat.astype(BF16), kr[:, :MLA_ROPE].astype(BF16), jnp.concatenate([tab, tab], axis=1),
               wq_ref, wkn_ref, wv_ref, gqn_ref[...] * scale, gqr_ref[...] * scale, gkn_ref[...],
               gsum_ref, q_ref, k_ref, v_ref, heads)


def _mla_expand_kernel(lat_ref, kr_ref, wkn_ref, wv_ref, gkn_ref, gsum_ref, k_ref, v_ref, *, heads):
    _mla_heads(None, lat_ref[...].astype(BF16), kr_ref[...].astype(BF16), None, None, wkn_ref, wv_ref,
               None, None, gkn_ref[...], gsum_ref, None, k_ref, v_ref, heads)


def _mla_proj(h, w, tab, tm):
    n, d = h.shape
    heads = w['heads']
    q_lora, kv_lora = w['g_q_a'].shape[1], w['g_kv_a'].shape[1]
    qk = MLA_NOPE + MLA_ROPE
    nt = tab.shape[0] // tm
    row = lambda i: (i, 0)
    hrow = lambda i: (0, i, 0)
    kern = functools.partial(_mla_proj_kernel, heads=heads, q_lora=q_lora, kv_lora=kv_lora,
                             scale=qk ** -0.5 * LOG2E)
    return pl.pallas_call(
        kern,
        grid=(n // tm,),
        in_specs=[
            pl.BlockSpec((tm, d), row),
            _resident(w['g_attn'].shape), _resident(w['w_in'].shape),
            _resident(w['g_q_a'].shape), _resident(w['g_kv_a'].shape), _resident(w['g_k_rope'].shape),
            pl.BlockSpec((tm, LANES), lambda i: (i % nt, 0)),
            _resident(w['w_q'].shape), _resident(w['g_q_nope'].shape), _resident(w['g_q_rope'].shape),
            _resident(w['w_kn'].shape), _resident(w['w_v'].shape), _resident(w['g_k_nope'].shape),
            _resident(w['gsum'].shape),
        ],
        out_specs=[
            pl.BlockSpec((tm, kv_lora), row),
            pl.BlockSpec((tm, MLA_ROPE), row),
            pl.BlockSpec((heads, tm, qk), hrow),
            pl.BlockSpec((heads, tm, qk), hrow),
            pl.BlockSpec((heads, tm, MLA_V), hrow),
        ],
        out_shape=[
            jax.ShapeDtypeStruct((n, kv_lora), F32),
            jax.ShapeDtypeStruct((n, MLA_ROPE), F32),
            jax.ShapeDtypeStruct((heads, n, qk), BF16),
            jax.ShapeDtypeStruct((heads, n, qk), BF16),
            jax.ShapeDtypeStruct((heads, n, MLA_V), BF16),
        ],
        compiler_params=_params("parallel"),
        name="mla_proj",
    )(h, w['g_attn'], w['w_in'], w['g_q_a'], w['g_kv_a'], w['g_k_rope'], tab, w['w_q'],
      w['g_q_nope'], w['g_q_rope'], w['w_kn'], w['w_v'], w['g_k_nope'], w['gsum'])


def _mla_expand_call(lat, kr, layer, w, tm):
    _, n, kv_lora = lat.shape
    heads = w['heads']
    qk = MLA_NOPE + MLA_ROPE
    hrow = lambda i: (0, i, 0)
    return pl.pallas_call(
        functools.partial(_mla_expand_kernel, heads=heads),
        grid=(n // tm,),
        in_specs=[pl.BlockSpec((None, tm, kv_lora), lambda i: (layer, i, 0)),
                  pl.BlockSpec((None, tm, MLA_ROPE), lambda i: (layer, i, 0)),
                  _resident(w['w_kn'].shape), _resident(w['w_v'].shape),
                  _resident(w['g_k_nope'].shape), _resident(w['gsum'].shape)],
        out_specs=[pl.BlockSpec((heads, tm, qk), hrow), pl.BlockSpec((heads, tm, MLA_V), hrow)],
        out_shape=[jax.ShapeDtypeStruct((heads, n, qk), BF16),
                   jax.ShapeDtypeStruct((heads, n, MLA_V), BF16)],
        compiler_params=_params("parallel"),
        name="mla_expand",
    )(lat, kr, w['w_kn'], w['w_v'], w['g_k_nope'], w['gsum'])


def _mla_attn_kernel(q_ref, k_ref, v_ref, o_ref, s_ref, *, hb, tq):
    i = pl.program_id(2)
    kc = lax.broadcasted_iota(jnp.int32, (tq, tq), 0) // CHUNK
    qc = lax.broadcasted_iota(jnp.int32, (tq, tq), 1) // CHUNK
    diag_bias = jnp.where(kc <= qc, 0.0, NEG).astype(F32)
    qs = [q_ref[h] for h in range(hb)]

    def scores(j, slot):
        start = pl.multiple_of(j * tq, tq)
        mx = []
        for h in range(hb):
            s = _dot_t(k_ref[h, pl.ds(start, tq), :], qs[h])
            s_ref[slot, h] = s
            mx.append(jnp.max(s, axis=0, keepdims=True))
        return tuple(mx)

    def update(j, slot, carry, mx, bias=None):
        start = pl.multiple_of(j * tq, tq)
        ps, stats = [], []
        for h in range(hb):
            m, l, _ = carry[h]
            if bias is None:
                s, blk_max = s_ref[slot, h], mx[h]
            else:
                s = s_ref[slot, h] + bias
                blk_max = jnp.max(s, axis=0, keepdims=True)
            m_new = jnp.maximum(m, blk_max)
            alpha = jnp.exp2(m - m_new)
            p = jnp.exp2(s - m_new)
            stats.append((m_new, alpha * l + jnp.sum(p, axis=0, keepdims=True), alpha))
            ps.append(p.astype(BF16))
        out = []
        for h in range(hb):
            pv = lax.dot_general(v_ref[h, pl.ds(start, tq), :], ps[h],
                                 (((0,), (0,)), ((), ())), preferred_element_type=F32)
            m_new, l, alpha = stats[h]
            out.append((m_new, l, alpha * carry[h][2] + pv))
        return tuple(out)

    def finish(carry):
        for h in range(hb):
            _, l, acc = carry[h]
            o_ref[:, h * MLA_V:(h + 1) * MLA_V] = (acc / l).T.astype(BF16)

    def pair(t, state):
        carry, mx0 = state
        mx1 = scores(2 * t + 1, 1)
        carry = update(2 * t, 0, carry, mx0)
        mx0 = scores(2 * t + 2, 0)
        return update(2 * t + 1, 1, carry, mx1), mx0

    init = tuple((jnp.full((1, tq), NEG, F32), jnp.zeros((1, tq), F32), jnp.zeros((MLA_V, tq), F32))
                 for _ in range(hb))
    carry, mx0 = lax.fori_loop(0, i // 2, pair, (init, scores(0, 0)))

    @pl.when(i % 2 == 0)
    def _():
        finish(update(i, 0, carry, None, diag_bias))

    @pl.when(i % 2 == 1)
    def _():
        scores(i, 1)
        finish(update(i, 1, update(i - 1, 0, carry, mx0), None, diag_bias))


def _mla_attn(q, k, v, batch, seq, tq, hb):
    heads, n, qk = q.shape
    nq = seq // tq
    kv_map = lambda b, g, i: (g, b, 0)
    return pl.pallas_call(
        functools.partial(_mla_attn_kernel, hb=hb, tq=tq),
        grid=(batch, heads // hb, nq),
        in_specs=[pl.BlockSpec((hb, tq, qk), lambda b, g, i: (g, b * nq + i, 0)),
                  pl.BlockSpec((hb, seq, qk), kv_map),
                  pl.BlockSpec((hb, seq, MLA_V), kv_map)],
        out_specs=pl.BlockSpec((tq, hb * MLA_V), lambda b, g, i: (b * nq + i, g)),
        out_shape=jax.ShapeDtypeStruct((n, heads * MLA_V), BF16),
        scratch_shapes=[pltpu.VMEM((2, hb, tq, tq), F32)],
        compiler_params=_params("parallel", "parallel", "arbitrary"),
        name="mla_attn",
    )(q, k, v)


def _mla_attn_sample_kernel(q_ref, kc_ref, kn_ref, vc_ref, vn_ref, o_ref, *, hb):
    ss = [(_dot_t(kc_ref[h], q_ref[h]), _dot_t(kn_ref[h], q_ref[h])) for h in range(hb)]
    ps = []
    for s_c, s_n in ss:
        m = jnp.maximum(jnp.max(s_c, axis=0, keepdims=True), jnp.max(s_n, axis=0, keepdims=True))
        p_c = jnp.exp2(s_c - m)
        p_n = jnp.exp2(s_n - m)
        l = jnp.sum(p_c, axis=0, keepdims=True) + jnp.sum(p_n, axis=0, keepdims=True)
        ps.append((p_c.astype(BF16), p_n.astype(BF16), l))
    tn = (((0,), (0,)), ((), ()))
    for h, (p_c, p_n, l) in enumerate(ps):
        acc = (lax.dot_general(vc_ref[h], p_c, tn, preferred_element_type=F32)
               + lax.dot_general(vn_ref[h], p_n, tn, preferred_element_type=F32))
        o_ref[:, h * MLA_V:(h + 1) * MLA_V] = (acc / l).T.astype(BF16)


def _mla_attn_sample(q, k_cache, k_new, v_cache, v_new, batch, past, t, hb):
    heads, n, qk = q.shape
    q_pos = past + np.arange(t)
    k_pos = np.arange(past + t)
    assert np.all((k_pos // CHUNK)[None, :] <= (q_pos // CHUNK)[:, None])
    m3 = lambda b, g: (g, b, 0)
    return pl.pallas_call(
        functools.partial(_mla_attn_sample_kernel, hb=hb),
        grid=(batch, heads // hb),
        in_specs=[pl.BlockSpec((hb, t, qk), m3), pl.BlockSpec((hb, past, qk), m3),
                  pl.BlockSpec((hb, t, qk), m3), pl.BlockSpec((hb, past, MLA_V), m3),
                  pl.BlockSpec((hb, t, MLA_V), m3)],
        out_specs=pl.BlockSpec((t, hb * MLA_V), lambda b, g: (b, g)),
        out_shape=jax.ShapeDtypeStruct((n, heads * MLA_V), BF16),
        compiler_params=_params("parallel", "parallel"),
        name="mla_attn_sample",
    )(q, k_cache, k_new, v_cache, v_new)


def _swa_proj_kernel(h_ref, gattn_ref, w_ref, gq_ref, gk_ref, tc_ref, ts1_ref, ts2_ref, gmat_ref,
                     q_ref, kd_ref, vd_ref, kf_ref, vf_ref, *, q_cols, kv_heads, scale):
    hn = _rms(h_ref[...], gattn_ref[...]).astype(BF16)
    tc, ts1, ts2 = tc_ref[...], ts1_ref[...], ts2_ref[...]
    gmat = gmat_ref[...]

    def rope(x, ss, g):
        y = x * lax.rsqrt(ss * (1.0 / SWA_HEAD_DIM) + EPS) * g
        return (y * tc + pltpu.roll(y, LANES - SWA_ROT // 2, axis=1) * ts1
                + pltpu.roll(y, SWA_ROT // 2, axis=1) * ts2)

    gq = gq_ref[...] * scale
    gk = gk_ref[...]
    n_q = q_cols // 2

    def finish(c, x2):
        ss = _dot((x2 * x2).astype(BF16), gmat)
        if c < n_q:
            for u in range(2):
                col = 2 * c + u
                sl = slice(u * LANES, (u + 1) * LANES)
                q_ref[:, col * LANES:(col + 1) * LANES] = rope(x2[:, sl], ss[:, sl], gq).astype(BF16)
        else:
            kh = c - n_q
            kd = rope(x2[:, :LANES], ss[:, :LANES], gk)
            kd_ref[kh] = kd.astype(BF16)
            kf_ref[kh] = kd[:, :SWA_HEAD_DIM]
            vd_ref[kh] = x2[:, LANES:].astype(BF16)
            vf_ref[kh] = x2[:, LANES:LANES + SWA_HEAD_DIM]

    pending = None
    for c in range(n_q + kv_heads):
        x2 = _dot(hn, w_ref[:, c * 256:(c + 1) * 256])
        if pending is not None:
            finish(*pending)
        pending = (c, x2)
    finish(*pending)


def _swa_proj(h, w, tabs, tm):
    n, d = h.shape
    q_heads, kv_heads = w['q_heads'], w['kv_heads']
    q_cols = q_heads * SWA_HEAD_DIM // LANES
    nt = tabs[0].shape[0] // tm
    row = lambda i: (i, 0)
    hrow = lambda i: (0, i, 0)
    tspec = pl.BlockSpec((tm, LANES), lambda i: (i % nt, 0))
    kern = functools.partial(_swa_proj_kernel, q_cols=q_cols, kv_heads=kv_heads,
                             scale=SWA_HEAD_DIM ** -0.5 * LOG2E)
    return pl.pallas_call(
        kern,
        grid=(n // tm,),
        in_specs=[pl.BlockSpec((tm, d), row), _resident(w['g_attn'].shape),
                  _resident(w['w_qkv'].shape), _resident(w['g_q'].shape), _resident(w['g_k'].shape),
                  tspec, tspec, tspec, _resident(w['gmat'].shape)],
        out_specs=[pl.BlockSpec((tm, q_cols * LANES), row),
                   pl.BlockSpec((kv_heads, tm, LANES), hrow),
                   pl.BlockSpec((kv_heads, tm, LANES), hrow),
                   pl.BlockSpec((kv_heads, tm, SWA_HEAD_DIM), hrow),
                   pl.BlockSpec((kv_heads, tm, SWA_HEAD_DIM), hrow)],
        out_shape=[jax.ShapeDtypeStruct((n, q_cols * LANES), BF16),
                   jax.ShapeDtypeStruct((kv_heads, n, LANES), BF16),
                   jax.ShapeDtypeStruct((kv_heads, n, LANES), BF16),
                   jax.ShapeDtypeStruct((kv_heads, n, SWA_HEAD_DIM), F32),
                   jax.ShapeDtypeStruct((kv_heads, n, SWA_HEAD_DIM), F32)],
        compiler_params=_params("parallel"),
        name="swa_proj",
    )(h, w['g_attn'], w['w_qkv'], w['g_q'], w['g_k'], *tabs, w['gmat'])


def _swa_attn_kernel(sink_ref, q_ref, kp_ref, kc_ref, vp_ref, vc_ref, o_ref, *,
                     kv_heads, group, banded):
    tq = q_ref.shape[0]
    n_p, n_c = kp_ref.shape[1], kc_ref.shape[1]
    n_k = n_p + n_c
    lane = lax.broadcasted_iota(jnp.int32, (1, LANES), 1)
    row = lax.broadcasted_iota(jnp.int32, (LANES, 1), 0)
    bias = None
    if banded:
        win = WINDOW // CHUNK
        kc = lax.broadcasted_iota(jnp.int32, (n_k, tq), 0) // CHUNK - n_p // CHUNK
        qc = lax.broadcasted_iota(jnp.int32, (n_k, tq), 1) // CHUNK
        ok = (kc <= qc) & (kc >= qc - win) & ((kc >= 0) | (pl.program_id(1) > 0))
        bias = jnp.where(ok, 0.0, NEG).astype(F32)
    zero = jnp.zeros((), BF16)
    half = group // 2
    for kh in range(kv_heads):
        k_all = jnp.concatenate([kp_ref[kh], kc_ref[kh]], axis=0)
        v_t = jnp.concatenate([vp_ref[kh], vc_ref[kh]], axis=0).T
        ks = (jnp.where(lane < SWA_HEAD_DIM, k_all, zero), jnp.where(lane < SWA_HEAD_DIM, zero, k_all))
        vs = (jnp.where(row < SWA_HEAD_DIM, v_t, zero), jnp.where(row < SWA_HEAD_DIM, zero, v_t))
        cols = [q_ref[:, (kh * half + c) * LANES:(kh * half + c + 1) * LANES] for c in range(half)]
        ss = [_dot_t(ks[u], cols[c]) for c in range(half) for u in range(2)]
        es, rdens = [], []
        for c in range(half):
            for u in range(2):
                s = ss[2 * c + u]
                if banded:
                    s = s + bias
                sink = sink_ref[2 * (kh * half + c) + u] * LOG2E
                m = jnp.maximum(jnp.max(s, axis=0, keepdims=True), sink)
                e = jnp.exp2(s - m)
                rdens.append(1.0 / (jnp.sum(e, axis=0, keepdims=True) + jnp.exp2(sink - m)))
                es.append(e.astype(BF16))
        for c in range(half):
            o_t = (_dot(vs[0], es[2 * c]) * rdens[2 * c]
                   + _dot(vs[1], es[2 * c + 1]) * rdens[2 * c + 1])
            col = kh * half + c
            o_ref[:, col * LANES:(col + 1) * LANES] = o_t.T.astype(BF16)


def _swa_attn(sinks, q, kd, vd, batch, seq, tq):
    n, dq = q.shape
    kv_heads = kd.shape[0]
    group = (dq // SWA_HEAD_DIM) // kv_heads
    nq = seq // tq
    per = tq // WINDOW
    prev = lambda b, i: (0, jnp.maximum(i * per - 1, 0) + b * nq * per, 0)
    cur = lambda b, i: (0, b * nq + i, 0)
    return pl.pallas_call(
        functools.partial(_swa_attn_kernel, kv_heads=kv_heads, group=group, banded=True),
        grid=(batch, nq),
        in_specs=[pl.BlockSpec(memory_space=pltpu.SMEM),
                  pl.BlockSpec((tq, dq), lambda b, i: (b * nq + i, 0)),
                  pl.BlockSpec((kv_heads, WINDOW, LANES), prev),
                  pl.BlockSpec((kv_heads, tq, LANES), cur),
                  pl.BlockSpec((kv_heads, WINDOW, LANES), prev),
                  pl.BlockSpec((kv_heads, tq, LANES), cur)],
        out_specs=pl.BlockSpec((tq, dq), lambda b, i: (b * nq + i, 0)),
        out_shape=jax.ShapeDtypeStruct((n, dq), BF16),
        compiler_params=_params("parallel", "arbitrary"),
        name="swa_attn",
    )(sinks, q, kd, kd, vd, vd)


def _swa_attn_sample(sinks, q, kd_cache, kd_new, vd_cache, vd_new, batch, keep, t):
    n, dq = q.shape
    kv_heads = kd_new.shape[0]
    group = (dq // SWA_HEAD_DIM) // kv_heads
    blk = lambda b: (0, b, 0)
    return pl.pallas_call(
        functools.partial(_swa_attn_kernel, kv_heads=kv_heads, group=group, banded=False),
        grid=(batch,),
        in_specs=[pl.BlockSpec(memory_space=pltpu.SMEM),
                  pl.BlockSpec((t, dq), lambda b: (b, 0)),
                  pl.BlockSpec((kv_heads, keep, LANES), blk),
                  pl.BlockSpec((kv_heads, t, LANES), blk),
                  pl.BlockSpec((kv_heads, keep, LANES), blk),
                  pl.BlockSpec((kv_heads, t, LANES), blk)],
        out_specs=pl.BlockSpec((t, dq), lambda b: (b, 0)),
        out_shape=jax.ShapeDtypeStruct((n, dq), BF16),
        compiler_params=_params("parallel"),
        name="swa_attn_sample",
    )(sinks, q, kd_cache, kd_new, vd_cache, vd_new)


def _oproj_kernel(h_ref, o_ref, w_ref, out_ref):
    out_ref[...] = h_ref[...] + _dot(o_ref[...], w_ref[...])


def _oproj(h, o, w, layer, tm):
    n, d = h.shape
    row = lambda i: (i, 0)
    return pl.pallas_call(
        _oproj_kernel,
        grid=(n // tm,),
        in_specs=[pl.BlockSpec((tm, d), row), pl.BlockSpec((tm, o.shape[1]), row),
                  _resident(w.shape, layer)],
        out_specs=pl.BlockSpec((tm, d), row),
        out_shape=jax.ShapeDtypeStruct((n, d), F32),
        compiler_params=_params("parallel"),
        name="attn_out_proj",
    )(h, o, w)


def _ffn_kernel(h_ref, g_ref, wg_ref, wu_ref, wd_ref, out_ref, hn_ref):
    @pl.when(pl.program_id(1) == 0)
    def _():
        x = h_ref[...]
        hn_ref[...] = _rms(x, g_ref[...]).astype(BF16)
        out_ref[...] = x

    hn = hn_ref[...]
    a = _dot(hn, wg_ref[...])
    b = _dot(hn, wu_ref[...])
    act = (a * jax.nn.sigmoid(a) * b).astype(BF16)
    out_ref[...] += _dot(act, wd_ref[...])


def _ffn(h, g, wg, wu, wd, layer, tm, tf):
    n, d = h.shape
    f = wg.shape[2]
    row = lambda i, j: (i, 0)
    return pl.pallas_call(
        _ffn_kernel,
        grid=(n // tm, f // tf),
        in_specs=[pl.BlockSpec((tm, d), row), pl.BlockSpec((None, 1, d), lambda i, j: (layer, 0, 0)),
                  pl.BlockSpec((None, d, tf), lambda i, j: (layer, 0, j)),
                  pl.BlockSpec((None, d, tf), lambda i, j: (layer, 0, j)),
                  pl.BlockSpec((None, tf, d), lambda i, j: (layer, j, 0))],
        out_specs=pl.BlockSpec((tm, d), row),
        out_shape=jax.ShapeDtypeStruct((n, d), F32),
        scratch_shapes=[pltpu.VMEM((tm, d), BF16)],
        compiler_params=_params("parallel", "arbitrary"),
        name="swiglu_ffn",
    )(h, g, wg, wu, wd)


def _ple_kernel(h_ref, g_ref, wgate_ref, p_ref, wproj_ref, gfin_ref, *rest, final, n_cast):
    cast_in, out_ref, cast_out = rest[:n_cast], rest[n_cast], rest[n_cast + 1:]
    x = h_ref[...]
    gate = jax.nn.sigmoid(_dot(_rms(x, g_ref[...]).astype(BF16), wgate_ref[...]))
    y = x + gate * _dot(p_ref[...].astype(BF16), wproj_ref[...])
    if final:
        y = _rms(y, gfin_ref[...])
    out_ref[...] = y
    for src, dst in zip(cast_in, cast_out):
        dst[...] = src[...].astype(BF16)


def _ple(h, g, wgate, p, layer, wproj, gfin, final, tm, cast=(), cast_layer=0):
    n, d = h.shape
    steps = n // tm
    row = lambda i: (i, 0)
    cast_specs, cast_out_specs, cast_shapes = [], [], []
    for a in cast:
        _, r, c = a.shape
        cast_specs.append(pl.BlockSpec((None, r // steps, c), lambda i: (cast_layer, i, 0)))
        cast_out_specs.append(pl.BlockSpec((None, r // steps, c), lambda i: (0, i, 0)))
        cast_shapes.append(jax.ShapeDtypeStruct((1, r, c), BF16))
    out = pl.pallas_call(
        functools.partial(_ple_kernel, final=final, n_cast=len(cast)),
        grid=(steps,),
        in_specs=[pl.BlockSpec((tm, d), row), _resident(g.shape, layer), _resident(wgate.shape, layer),
                  pl.BlockSpec((None, tm, p.shape[2]), lambda i: (layer, i, 0)),
                  _resident(wproj.shape, layer), _resident(gfin.shape)] + cast_specs,
        out_specs=[pl.BlockSpec((tm, d), row)] + cast_out_specs,
        out_shape=[jax.ShapeDtypeStruct((n, d), F32)] + cast_shapes,
        compiler_params=_params("parallel"),
        name="ple_embed",
    )(h, g, wgate, p, wproj, gfin, *cast)
    return out[0], tuple(out[1:])


def _angles(pos, r, theta):
    inv = jnp.power(jnp.float32(theta), -jnp.arange(0, r, 2, dtype=jnp.float32) / r)
    ang = pos.astype(jnp.float32)[:, None] * inv[None, :]
    return jnp.cos(ang), jnp.sin(ang)


def _mla_table(pos):
    cos, sin = _angles(pos, MLA_ROPE, MLA_THETA)
    return jnp.concatenate([cos, cos, -sin, sin], axis=-1)


def _swa_tables(pos):
    cos, sin = _angles(pos, SWA_ROT, SWA_THETA)
    s = cos.shape[0]
    rest = SWA_HEAD_DIM - SWA_ROT
    one = jnp.ones((s, rest), F32)
    zero = jnp.zeros((s, rest), F32)
    zh = jnp.zeros_like(sin)
    tc = jnp.concatenate([cos, cos, one], axis=-1)
    ts1 = jnp.concatenate([-sin, zh, zero], axis=-1)
    ts2 = jnp.concatenate([zh, sin, zero], axis=-1)
    return tuple(jnp.concatenate([t, t], axis=-1) for t in (tc, ts1, ts2))


def _tile_rows(tab, rows):
    return tab if tab.shape[0] >= rows else jnp.tile(tab, (rows // tab.shape[0], 1))


def _swap_halves(x):
    half = x.shape[-1] // 2
    return jnp.concatenate([x[..., half:], x[..., :half]], axis=-1)


def _prep_mla(j, g_attn_norm_i, w_mla_in, g_mla_q_a, w_mla_q_up, g_mla_kv_a, w_mla_kv_up, g_mla_q_nope,
              g_mla_q_rope, g_mla_k_nope, g_mla_k_rope, w_mla_out):
    q_lora, kv_lora = g_mla_q_a.shape[1], g_mla_kv_a.shape[1]
    qk = MLA_NOPE + MLA_ROPE
    heads = w_mla_q_up.shape[2] // qk
    w_in = w_mla_in[j]
    k_r = w_in[:, q_lora + kv_lora:]
    w_in = jnp.concatenate([w_in, _swap_halves(k_r)], axis=-1)
    wq = w_mla_q_up[j].reshape(q_lora, heads, qk)
    rope = wq[:, :, MLA_NOPE:]
    wq = jnp.concatenate([wq[:, :, :MLA_NOPE].reshape(q_lora, heads // 2, 2 * MLA_NOPE),
                          jnp.concatenate([rope, _swap_halves(rope)], axis=-1).reshape(q_lora, heads // 2, 2 * LANES)],
                         axis=-1).reshape(q_lora, heads * 2 * LANES)
    wkv = w_mla_kv_up[j].reshape(kv_lora, heads, MLA_NOPE + MLA_V)
    dup = lambda g: jnp.concatenate([g, _swap_halves(g)], axis=-1)[None, :]
    two = lambda g: jnp.concatenate([g, g], axis=-1)
    gsum = np.kron(np.eye(2), np.ones((LANES, LANES)))
    return dict(
        heads=heads, gsum=jnp.asarray(gsum, BF16),
        g_attn=g_attn_norm_i[None, :],
        w_in=w_in.astype(BF16),
        g_q_a=g_mla_q_a[j][None, :], g_kv_a=g_mla_kv_a[j][None, :],
        g_k_rope=dup(g_mla_k_rope[j]), g_q_rope=two(dup(g_mla_q_rope[j])),
        g_q_nope=two(g_mla_q_nope[j][None, :]), g_k_nope=two(g_mla_k_nope[j][None, :]),
        w_q=wq.astype(BF16),
        w_kn=wkv[:, :, :MLA_NOPE].reshape(kv_lora, heads * MLA_NOPE).astype(BF16),
        w_v=wkv[:, :, MLA_NOPE:].reshape(kv_lora, heads * MLA_V).astype(BF16),
        w_out=w_mla_out.astype(BF16),
    )


def _prep_swa(j, g_attn_norm_i, w_swa_qkv, g_swa_q, g_swa_k, swa_sinks, w_swa_out):
    q_heads = swa_sinks.shape[1]
    w = w_swa_qkv[j]
    d = w.shape[0]
    nq = q_heads * SWA_HEAD_DIM
    kv_heads = (w.shape[1] - nq) // (2 * SWA_HEAD_DIM)
    nk = kv_heads * SWA_HEAD_DIM
    wk = w[:, nq:nq + nk].reshape(d, kv_heads, SWA_HEAD_DIM)
    wv = w[:, nq + nk:].reshape(d, kv_heads, SWA_HEAD_DIM)
    wkv = jnp.concatenate([wk, wk, wv, wv], axis=-1).reshape(d, kv_heads * 4 * SWA_HEAD_DIM)
    gmat = np.kron(np.eye(2 * LANES // SWA_HEAD_DIM), np.ones((SWA_HEAD_DIM, SWA_HEAD_DIM)))
    dup = lambda g: jnp.concatenate([g, g], axis=-1)[None, :]
    return dict(
        q_heads=q_heads, kv_heads=kv_heads,
        g_attn=g_attn_norm_i[None, :],
        w_qkv=jnp.concatenate([w[:, :nq], wkv], axis=-1).astype(BF16),
        g_q=dup(g_swa_q[j]), g_k=dup(g_swa_k[j]),
        gmat=jnp.asarray(gmat, BF16),
        sinks=swa_sinks[j],
        w_out=w_swa_out.astype(BF16),
    )


def _kv_dup(t):
    b, l, kv, d = t.shape
    t = jnp.transpose(t, (2, 0, 1, 3)).reshape(kv, b * l, d)
    return jnp.concatenate([t, t], axis=-1).astype(BF16)


def _trunk(x, p, pos, caches, layers, ffn_w, ple_w, g_final, tm, tf):
    batch, seq, d = x.shape
    n = batch * seq
    h = x.reshape(n, d)
    rows = max(tm, seq)
    mla_tab = _tile_rows(_mla_table(pos), rows)
    swa_tabs = tuple(_tile_rows(t, rows) for t in _swa_tables(pos))
    depth = len(layers)
    p_rows = p.reshape(depth, n, -1)
    lats, krs, sks, svs = [], [], [], []
    for i, (kind, w) in enumerate(layers):
        j = i // 2
        if kind == 'mla':
            lat, kr, q, k, v = _mla_proj(h, w, mla_tab, tm)
            if caches is None:
                o = _mla_attn(q, k, v, batch, seq, tq=512, hb=4)
            else:
                lat_c, kr_c = caches[0], caches[1]
                past = lat_c.shape[2]
                k_c, v_c = _mla_expand_call(lat_c.reshape(lat_c.shape[0], batch * past, -1),
                                            kr_c.reshape(kr_c.shape[0], batch * past, -1), j, w, tm=512)
                o = _mla_attn_sample(q, k_c, k, v_c, v, batch, past, seq, hb=4)
            lats.append(lat.reshape(batch, seq, -1))
            krs.append(kr.reshape(batch, seq, -1))
        else:
            q, kd, vd, kf, vf = _swa_proj(h, w, swa_tabs, tm)
            kv_heads = w['kv_heads']
            unhead = lambda t, rows: jnp.transpose(
                t.reshape(kv_heads, batch, seq, SWA_HEAD_DIM)[:, :, seq - rows:], (1, 2, 0, 3))
            if caches is None:
                o = _swa_attn(w['sinks'], q, kd, vd, batch, seq, tq=256)
                keep = min(WINDOW, seq)
                sks.append(unhead(kf, keep))
                svs.append(unhead(vf, keep))
            else:
                ck, cv = caches[2][j], caches[3][j]
                keep = ck.shape[1]
                o = _swa_attn_sample(w['sinks'], q, _kv_dup(ck), kd, _kv_dup(cv), vd, batch, keep, seq)
                sks.append(jnp.concatenate([ck, unhead(kf, seq)], axis=1)[:, -keep:])
                svs.append(jnp.concatenate([cv, unhead(vf, seq)], axis=1)[:, -keep:])
        h = _oproj(h, o, w['w_out'], j, tm)
        ffn_g, ffn_raw, ffn_cast = ffn_w
        h = _ffn(h, ffn_g[i:i + 1], *ffn_cast[i], 0, min(n, 2 * tm), tf)
        g_ple, wgate, wproj = ple_w
        cast = ffn_raw if (i + 1 not in ffn_cast and i + 1 < depth) else ()
        h, casted = _ple(h, g_ple, wgate, p_rows, i, wproj, g_final, i == depth - 1, tm, cast, i + 1)
        if casted:
            ffn_cast[i + 1] = casted
    return (h.reshape(batch, seq, d), jnp.stack(lats), jnp.stack(krs), jnp.stack(sks), jnp.stack(svs))


def kernel(x_prompt, x_sample, p_prompt, p_sample, cache_mla_latent, cache_mla_krope, state_swa_k, state_swa_v, g_attn_norm, w_mla_in, g_mla_q_a, w_mla_q_up, g_mla_kv_a, w_mla_kv_up, g_mla_q_nope, g_mla_q_rope, g_mla_k_nope, g_mla_k_rope, w_mla_out, w_swa_qkv, g_swa_q, g_swa_k, swa_sinks, w_swa_out, g_ffn_norm, w_ffn_gate, w_ffn_up, w_ffn_down, g_ple_norm, w_ple_gate, w_ple_proj, g_final):
    depth = g_attn_norm.shape[0]
    layers = []
    for i in range(depth):
        j = i // 2
        if i % 2 == 0:
            layers.append(('mla', _prep_mla(j, g_attn_norm[i], w_mla_in, g_mla_q_a, w_mla_q_up,
                                            g_mla_kv_a, w_mla_kv_up, g_mla_q_nope, g_mla_q_rope,
                                            g_mla_k_nope, g_mla_k_rope, w_mla_out)))
        else:
            layers.append(('swa', _prep_swa(j, g_attn_norm[i], w_swa_qkv, g_swa_q, g_swa_k,
                                            swa_sinks, w_swa_out)))
    ffn_raw = (w_ffn_gate, w_ffn_up, w_ffn_down)
    ffn_w = (g_ffn_norm[:, None, :], ffn_raw, {0: tuple(w[0:1].astype(BF16) for w in ffn_raw)})
    ple_w = (g_ple_norm[:, None, :], w_ple_gate.astype(BF16), w_ple_proj.astype(BF16))
    gfin = g_final[None, :]

    seq = x_prompt.shape[1]
    t = x_sample.shape[1]
    past = cache_mla_latent.shape[2]
    pos_p = jnp.arange(seq, dtype=jnp.int32)
    pos_s = past + jnp.arange(t, dtype=jnp.int32)
    y_p, lat_p, kr_p, sk_p, sv_p = _trunk(x_prompt, p_prompt, pos_p, None, layers, ffn_w, ple_w,
                                          gfin, tm=512, tf=512)
    caches = (cache_mla_latent, cache_mla_krope, state_swa_k, state_swa_v)
    n_s = x_sample.shape[0] * t
    y_s, lat_s, kr_s, sk_s, sv_s = _trunk(x_sample, p_sample, pos_s, caches, layers, ffn_w, ple_w,
                                          gfin, tm=n_s, tf=512)
    return (y_p, y_s, lat_p, kr_p, sk_p, sv_p, lat_s, kr_s, sk_s, sv_s)
```

```python
import functools

import numpy as np
import jax
import jax.numpy as jnp
from jax import lax
from jax.experimental import pallas as pl
from jax.experimental.pallas import tpu as pltpu

F32 = jnp.float32
BF16 = jnp.bfloat16

EPS = 1e-6
CHUNK = 64
WINDOW = 128
MLA_THETA = 10000.0
SWA_THETA = 500000.0
MLA_NOPE = 128
MLA_ROPE = 64
MLA_V = 128
SWA_HEAD_DIM = 64
SWA_ROT = SWA_HEAD_DIM // 4

LANES = 128
VMEM_LIMIT = 56 * 1024 * 1024
NEG = -1e30
LOG2E = 1.4426950408889634


def _params(*sem):
    return pltpu.CompilerParams(dimension_semantics=sem, vmem_limit_bytes=VMEM_LIMIT)


def _resident(shape, layer=None):
    if layer is None:
        zeros = (0,) * len(shape)
        return pl.BlockSpec(shape, lambda *_: zeros, pipeline_mode=pl.Buffered(1))
    index = (layer,) + (0,) * (len(shape) - 1)
    return pl.BlockSpec((None,) + tuple(shape[1:]), lambda *_: index, pipeline_mode=pl.Buffered(1))


def _cast_item(a, steps, layer=None, col_split=1):
    nl, r, c = a.shape
    lead, first = (nl, 0) if layer is None else (1, layer)
    return dict(a=a, block=(lead, r * col_split // steps, c // col_split),
                in_idx=lambda s: (first, s // col_split, s % col_split),
                out_idx=lambda s: (0, s // col_split, s % col_split),
                out_shape=jax.ShapeDtypeStruct((lead, r, c), BF16))


def _cast_specs(items, step_of):
    ins = [pl.BlockSpec(it['block'], lambda *g, it=it: it['in_idx'](step_of(*g))) for it in items]
    outs = [pl.BlockSpec(it['block'], lambda *g, it=it: it['out_idx'](step_of(*g))) for it in items]
    return ins, outs, [it['out_shape'] for it in items]


def _cast_chunks(srcs, dsts):
    for src, dst in zip(srcs, dsts):
        dst[...] = src[...].astype(BF16)


def _rms(x, g):
    return x * lax.rsqrt(jnp.mean(x * x, axis=-1, keepdims=True) + EPS) * g


def _dot(a, b):
    return jnp.dot(a, b, preferred_element_type=F32)


def _dot_t(a, b):
    return lax.dot_general(a, b, (((1,), (1,)), ((), ())), preferred_element_type=F32)


def _rope_dup(x, g, tab):
    ss = jnp.sum(x * x, axis=-1, keepdims=True)
    y = x * lax.rsqrt(ss * (1.0 / LANES) + EPS) * g
    t = y * tab
    return t + pltpu.roll(t, LANES // 2, axis=1)


def _mla_heads(c_q, latb, krb, tab, wq_ref, wkn_ref, wv_ref, gqn, gqr, gkn, gsum_ref,
               q_ref, k_ref, v_ref, heads):
    gsum = gsum_ref[...]
    pair = 2 * LANES

    def inv_rms(x):
        return lax.rsqrt(_dot((x * x).astype(BF16), gsum) * (1.0 / LANES) + EPS)

    def finish(p, xq, xk, xv):
        kn = xk * inv_rms(xk) * gkn
        if xq is not None:
            qn = xq[:, :pair] * inv_rms(xq[:, :pair]) * gqn
            t = xq[:, pair:] * inv_rms(xq[:, pair:]) * gqr * tab
        for u in range(2):
            h = 2 * p + u
            sl = slice(u * LANES, (u + 1) * LANES)
            k_ref[h, :, 0:MLA_NOPE] = kn[:, sl].astype(BF16)
            k_ref[h, :, MLA_NOPE:MLA_NOPE + MLA_ROPE] = krb
            v_ref[h] = xv[:, sl].astype(BF16)
            if xq is not None:
                q_ref[h, :, 0:MLA_NOPE] = qn[:, sl].astype(BF16)
                tu = t[:, sl]
                qr = tu + pltpu.roll(tu, LANES // 2, axis=1)
                q_ref[h, :, MLA_NOPE:MLA_NOPE + MLA_ROPE] = qr[:, :MLA_ROPE].astype(BF16)

    pending = None
    for p in range(heads // 2):
        xq = None if c_q is None else _dot(c_q, wq_ref[:, p * 2 * pair:(p + 1) * 2 * pair])
        xk = _dot(latb, wkn_ref[:, p * pair:(p + 1) * pair])
        xv = _dot(latb, wv_ref[:, p * pair:(p + 1) * pair])
        if pending is not None:
            finish(*pending)
        pending = (p, xq, xk, xv)
    finish(*pending)


def _mla_proj_kernel(h_ref, gattn_ref, win_ref, gqa_ref, gkva_ref, gkr_ref, tab_ref, wq_ref,
                     gqn_ref, gqr_ref, wkn_ref, wv_ref, gkn_ref, gsum_ref, *rest,
                     heads, q_lora, kv_lora, scale):
    lat_ref, kr_ref, q_ref, k_ref, v_ref = rest[-5:]
    hn = _rms(h_ref[...], gattn_ref[...]).astype(BF16)
    a = _dot(hn, win_ref[...])
    c_q = _rms(a[:, :q_lora], gqa_ref[...]).astype(BF16)
    lat = _rms(a[:, q_lora:q_lora + kv_lora], gkva_ref[...])
    lat_ref[...] = lat
    tab = tab_ref[...]
    kr = _rope_dup(a[:, q_lora + kv_lora:], gkr_ref[...], tab)
    kr_ref[...] = kr[:, :MLA_ROPE]
    _mla_heads(c_q, lat.astype(BF16), kr[:, :MLA_ROPE].astype(BF16), jnp.concatenate([tab, tab], axis=1),
               wq_ref, wkn_ref, wv_ref, gqn_ref[...] * scale, gqr_ref[...] * scale, gkn_ref[...],
               gsum_ref, q_ref, k_ref, v_ref, heads)


def _mla_expand_kernel(lat_ref, kr_ref, wkn_ref, wv_ref, gkn_ref, gsum_ref, k_ref, v_ref, *, heads):
    _mla_heads(None, lat_ref[...].astype(BF16), kr_ref[...].astype(BF16), None, None, wkn_ref, wv_ref,
               None, None, gkn_ref[...], gsum_ref, None, k_ref, v_ref, heads)


def _mla_proj(h, w, tab, tm, slab, n_slabs, stacks=None):
    n, d = h.shape
    heads = w['heads']
    q_lora, kv_lora = w['g_q_a'].shape[1], w['g_kv_a'].shape[1]
    qk = MLA_NOPE + MLA_ROPE
    nt = tab.shape[0] // tm
    row = lambda i: (i, 0)
    hrow = lambda i: (0, i, 0)
    srow = lambda i: (slab, i, 0)
    kern = functools.partial(_mla_proj_kernel, heads=heads, q_lora=q_lora, kv_lora=kv_lora,
                             scale=qk ** -0.5 * LOG2E)
    weights = (w['g_attn'], w['w_in'], w['g_q_a'], w['g_kv_a'], w['g_k_rope'], tab, w['w_q'],
               w['g_q_nope'], w['g_q_rope'], w['w_kn'], w['w_v'], w['g_k_nope'], w['gsum'])
    stacks = () if stacks is None else tuple(stacks)
    n_in = 1 + len(weights)
    return pl.pallas_call(
        kern,
        grid=(n // tm,),
        in_specs=[
            pl.BlockSpec((tm, d), row),
            _resident(w['g_attn'].shape), _resident(w['w_in'].shape),
            _resident(w['g_q_a'].shape), _resident(w['g_kv_a'].shape), _resident(w['g_k_rope'].shape),
            pl.BlockSpec((tm, LANES), lambda i: (i % nt, 0)),
            _resident(w['w_q'].shape), _resident(w['g_q_nope'].shape), _resident(w['g_q_rope'].shape),
            _resident(w['w_kn'].shape), _resident(w['w_v'].shape), _resident(w['g_k_nope'].shape),
            _resident(w['gsum'].shape),
        ] + [pl.BlockSpec(memory_space=pl.ANY)] * len(stacks),
        out_specs=[
            pl.BlockSpec((None, tm, kv_lora), srow),
            pl.BlockSpec((None, tm, MLA_ROPE), srow),
            pl.BlockSpec((heads, tm, qk), hrow),
            pl.BlockSpec((heads, tm, qk), hrow),
            pl.BlockSpec((heads, tm, MLA_V), hrow),
        ],
        out_shape=[
            jax.ShapeDtypeStruct((n_slabs, n, kv_lora), F32),
            jax.ShapeDtypeStruct((n_slabs, n, MLA_ROPE), F32),
            jax.ShapeDtypeStruct((heads, n, qk), BF16),
            jax.ShapeDtypeStruct((heads, n, qk), BF16),
            jax.ShapeDtypeStruct((heads, n, MLA_V), BF16),
        ],
        input_output_aliases={n_in + s: s for s in range(len(stacks))},
        compiler_params=_params("parallel"),
        name="mla_proj",
    )(h, *weights, *stacks)


def _mla_expand_call(lat, kr, layer, w, tm):
    _, n, kv_lora = lat.shape
    heads = w['heads']
    qk = MLA_NOPE + MLA_ROPE
    hrow = lambda i: (0, i, 0)
    return pl.pallas_call(
        functools.partial(_mla_expand_kernel, heads=heads),
        grid=(n // tm,),
        in_specs=[pl.BlockSpec((None, tm, kv_lora), lambda i: (layer, i, 0)),
                  pl.BlockSpec((None, tm, MLA_ROPE), lambda i: (layer, i, 0)),
                  _resident(w['w_kn'].shape), _resident(w['w_v'].shape),
                  _resident(w['g_k_nope'].shape), _resident(w['gsum'].shape)],
        out_specs=[pl.BlockSpec((heads, tm, qk), hrow), pl.BlockSpec((heads, tm, MLA_V), hrow)],
        out_shape=[jax.ShapeDtypeStruct((heads, n, qk), BF16),
                   jax.ShapeDtypeStruct((heads, n, MLA_V), BF16)],
        compiler_params=_params("parallel"),
        name="mla_expand",
    )(lat, kr, w['w_kn'], w['w_v'], w['g_k_nope'], w['gsum'])


def _mla_attn_kernel(q_ref, k_ref, v_ref, *rest, hb, tq, n_cast):
    cast_in, o_ref, cast_out, s_ref = rest[:n_cast], rest[n_cast], rest[n_cast + 1:-1], rest[-1]
    _cast_chunks(cast_in, cast_out)
    i = pl.program_id(2)
    kc = lax.broadcasted_iota(jnp.int32, (tq, tq), 0) // CHUNK
    qc = lax.broadcasted_iota(jnp.int32, (tq, tq), 1) // CHUNK
    diag_bias = jnp.where(kc <= qc, 0.0, NEG).astype(F32)
    qs = [q_ref[h] for h in range(hb)]

    def scores(j, slot):
        start = pl.multiple_of(j * tq, tq)
        mx = []
        for h in range(hb):
            s = _dot_t(k_ref[h, pl.ds(start, tq), :], qs[h])
            s_ref[slot, h] = s
            mx.append(jnp.max(s, axis=0, keepdims=True))
        return tuple(mx)

    def update(j, slot, carry, mx, bias=None):
        start = pl.multiple_of(j * tq, tq)
        ps, stats = [], []
        for h in range(hb):
            m, l, _ = carry[h]
            if bias is None:
                s, blk_max = s_ref[slot, h], mx[h]
            else:
                s = s_ref[slot, h] + bias
                blk_max = jnp.max(s, axis=0, keepdims=True)
            m_new = jnp.maximum(m, blk_max)
            alpha = jnp.exp2(m - m_new)
            p = jnp.exp2(s - m_new)
            stats.append((m_new, alpha * l + jnp.sum(p, axis=0, keepdims=True), alpha))
            ps.append(p.astype(BF16))
        out = []
        for h in range(hb):
            pv = lax.dot_general(v_ref[h, pl.ds(start, tq), :], ps[h],
                                 (((0,), (0,)), ((), ())), preferred_element_type=F32)
            m_new, l, alpha = stats[h]
            out.append((m_new, l, alpha * carry[h][2] + pv))
        return tuple(out)

    def finish(carry):
        for h in range(hb):
            _, l, acc = carry[h]
            o_ref[:, h * MLA_V:(h + 1) * MLA_V] = (acc / l).T.astype(BF16)

    def pair(t, state):
        carry, mx0 = state
        mx1 = scores(2 * t + 1, 1)
        carry = update(2 * t, 0, carry, mx0)
        mx0 = scores(2 * t + 2, 0)
        return update(2 * t + 1, 1, carry, mx1), mx0

    init = tuple((jnp.full((1, tq), NEG, F32), jnp.zeros((1, tq), F32), jnp.zeros((MLA_V, tq), F32))
                 for _ in range(hb))
    carry, mx0 = lax.fori_loop(0, i // 2, pair, (init, scores(0, 0)))

    @pl.when(i % 2 == 0)
    def _():
        finish(update(i, 0, carry, None, diag_bias))

    @pl.when(i % 2 == 1)
    def _():
        scores(i, 1)
        finish(update(i, 1, update(i - 1, 0, carry, mx0), None, diag_bias))


def _mla_attn_steps(batch, heads, seq, tq, hb):
    return batch * (heads // hb) * (seq // tq)


def _mla_attn(q, k, v, batch, seq, tq, hb, cast=()):
    heads, n, qk = q.shape
    nq = seq // tq
    ng = heads // hb
    kv_map = lambda b, g, i: (g, b, 0)
    cast_in, cast_out, cast_shapes = _cast_specs(cast, lambda b, g, i: (b * ng + g) * nq + i)
    out = pl.pallas_call(
        functools.partial(_mla_attn_kernel, hb=hb, tq=tq, n_cast=len(cast)),
        grid=(batch, ng, nq),
        in_specs=[pl.BlockSpec((hb, tq, qk), lambda b, g, i: (g, b * nq + i, 0)),
                  pl.BlockSpec((hb, seq, qk), kv_map),
                  pl.BlockSpec((hb, seq, MLA_V), kv_map)] + cast_in,
        out_specs=[pl.BlockSpec((tq, hb * MLA_V), lambda b, g, i: (b * nq + i, g))] + cast_out,
        out_shape=[jax.ShapeDtypeStruct((n, heads * MLA_V), BF16)] + cast_shapes,
        scratch_shapes=[pltpu.VMEM((2, hb, tq, tq), F32)],
        compiler_params=_params("parallel", "parallel", "arbitrary"),
        name="mla_attn",
    )(q, k, v, *[it['a'] for it in cast])
    return out[0], tuple(out[1:])


def _mla_attn_sample_kernel(q_ref, kc_ref, kn_ref, vc_ref, vn_ref, o_ref, *, hb):
    ss = [(_dot_t(kc_ref[h], q_ref[h]), _dot_t(kn_ref[h], q_ref[h])) for h in range(hb)]
    ps = []
    for s_c, s_n in ss:
        m = jnp.maximum(jnp.max(s_c, axis=0, keepdims=True), jnp.max(s_n, axis=0, keepdims=True))
        p_c = jnp.exp2(s_c - m)
        p_n = jnp.exp2(s_n - m)
        l = jnp.sum(p_c, axis=0, keepdims=True) + jnp.sum(p_n, axis=0, keepdims=True)
        ps.append((p_c.astype(BF16), p_n.astype(BF16), l))
    tn = (((0,), (0,)), ((), ()))
    for h, (p_c, p_n, l) in enumerate(ps):
        acc = (lax.dot_general(vc_ref[h], p_c, tn, preferred_element_type=F32)
               + lax.dot_general(vn_ref[h], p_n, tn, preferred_element_type=F32))
        o_ref[:, h * MLA_V:(h + 1) * MLA_V] = (acc / l).T.astype(BF16)


def _mla_attn_sample(q, k_cache, k_new, v_cache, v_new, batch, past, t, hb):
    heads, n, qk = q.shape
    q_pos = past + np.arange(t)
    k_pos = np.arange(past + t)
    assert np.all((k_pos // CHUNK)[None, :] <= (q_pos // CHUNK)[:, None])
    m3 = lambda b, g: (g, b, 0)
    return pl.pallas_call(
        functools.partial(_mla_attn_sample_kernel, hb=hb),
        grid=(batch, heads // hb),
        in_specs=[pl.BlockSpec((hb, t, qk), m3), pl.BlockSpec((hb, past, qk), m3),
                  pl.BlockSpec((hb, t, qk), m3), pl.BlockSpec((hb, past, MLA_V), m3),
                  pl.BlockSpec((hb, t, MLA_V), m3)],
        out_specs=pl.BlockSpec((t, hb * MLA_V), lambda b, g: (b, g)),
        out_shape=jax.ShapeDtypeStruct((n, heads * MLA_V), BF16),
        compiler_params=_params("parallel", "parallel"),
        name="mla_attn_sample",
    )(q, k_cache, k_new, v_cache, v_new)


def _swa_proj_kernel(h_ref, gattn_ref, w_ref, gq_ref, gk_ref, tc_ref, ts1_ref, ts2_ref, gmat_ref,
                     q_ref, kd_ref, vd_ref, kf_ref, vf_ref, *, q_cols, kv_heads, scale):
    hn = _rms(h_ref[...], gattn_ref[...]).astype(BF16)
    tc, ts1, ts2 = tc_ref[...], ts1_ref[...], ts2_ref[...]
    gmat = gmat_ref[...]

    def rope(x, ss, g):
        y = x * lax.rsqrt(ss * (1.0 / SWA_HEAD_DIM) + EPS) * g
        return (y * tc + pltpu.roll(y, LANES - SWA_ROT // 2, axis=1) * ts1
                + pltpu.roll(y, SWA_ROT // 2, axis=1) * ts2)

    gq = gq_ref[...] * scale
    gk = gk_ref[...]
    n_q = q_cols // 2

    def finish(c, x2):
        ss = _dot((x2 * x2).astype(BF16), gmat)
        if c < n_q:
            for u in range(2):
                col = 2 * c + u
                sl = slice(u * LANES, (u + 1) * LANES)
                q_ref[:, col * LANES:(col + 1) * LANES] = rope(x2[:, sl], ss[:, sl], gq).astype(BF16)
        else:
            kh = c - n_q
            kd = rope(x2[:, :LANES], ss[:, :LANES], gk)
            kd_ref[kh] = kd.astype(BF16)
            kf_ref[kh] = kd[:, :SWA_HEAD_DIM]
            vd_ref[kh] = x2[:, LANES:].astype(BF16)
            vf_ref[kh] = x2[:, LANES:LANES + SWA_HEAD_DIM]

    pending = None
    for c in range(n_q + kv_heads):
        x2 = _dot(hn, w_ref[:, c * 256:(c + 1) * 256])
        if pending is not None:
            finish(*pending)
        pending = (c, x2)
    finish(*pending)


def _swa_proj(h, w, tabs, tm):
    n, d = h.shape
    q_heads, kv_heads = w['q_heads'], w['kv_heads']
    q_cols = q_heads * SWA_HEAD_DIM // LANES
    nt = tabs[0].shape[0] // tm
    row = lambda i: (i, 0)
    hrow = lambda i: (0, i, 0)
    tspec = pl.BlockSpec((tm, LANES), lambda i: (i % nt, 0))
    kern = functools.partial(_swa_proj_kernel, q_cols=q_cols, kv_heads=kv_heads,
                             scale=SWA_HEAD_DIM ** -0.5 * LOG2E)
    return pl.pallas_call(
        kern,
        grid=(n // tm,),
        in_specs=[pl.BlockSpec((tm, d), row), _resident(w['g_attn'].shape),
                  _resident(w['w_qkv'].shape), _resident(w['g_q'].shape), _resident(w['g_k'].shape),
                  tspec, tspec, tspec, _resident(w['gmat'].shape)],
        out_specs=[pl.BlockSpec((tm, q_cols * LANES), row),
                   pl.BlockSpec((kv_heads, tm, LANES), hrow),
                   pl.BlockSpec((kv_heads, tm, LANES), hrow),
                   pl.BlockSpec((kv_heads, tm, SWA_HEAD_DIM), hrow),
                   pl.BlockSpec((kv_heads, tm, SWA_HEAD_DIM), hrow)],
        out_shape=[jax.ShapeDtypeStruct((n, q_cols * LANES), BF16),
                   jax.ShapeDtypeStruct((kv_heads, n, LANES), BF16),
                   jax.ShapeDtypeStruct((kv_heads, n, LANES), BF16),
                   jax.ShapeDtypeStruct((kv_heads, n, SWA_HEAD_DIM), F32),
                   jax.ShapeDtypeStruct((kv_heads, n, SWA_HEAD_DIM), F32)],
        compiler_params=_params("parallel"),
        name="swa_proj",
    )(h, w['g_attn'], w['w_qkv'], w['g_q'], w['g_k'], *tabs, w['gmat'])


def _swa_attn_kernel(sink_ref, q_ref, kp_ref, kc_ref, vp_ref, vc_ref, o_ref, *,
                     kv_heads, group, banded):
    tq = q_ref.shape[0]
    n_p, n_c = kp_ref.shape[1], kc_ref.shape[1]
    n_k = n_p + n_c
    lane = lax.broadcasted_iota(jnp.int32, (1, LANES), 1)
    row = lax.broadcasted_iota(jnp.int32, (LANES, 1), 0)
    bias = None
    if banded:
        win = WINDOW // CHUNK
        kc = lax.broadcasted_iota(jnp.int32, (n_k, tq), 0) // CHUNK - n_p // CHUNK
        qc = lax.broadcasted_iota(jnp.int32, (n_k, tq), 1) // CHUNK
        ok = (kc <= qc) & (kc >= qc - win) & ((kc >= 0) | (pl.program_id(1) > 0))
        bias = jnp.where(ok, 0.0, NEG).astype(F32)
    zero = jnp.zeros((), BF16)
    half = group // 2
    for kh in range(kv_heads):
        k_all = jnp.concatenate([kp_ref[kh], kc_ref[kh]], axis=0)
        v_t = jnp.concatenate([vp_ref[kh], vc_ref[kh]], axis=0).T
        ks = (jnp.where(lane < SWA_HEAD_DIM, k_all, zero), jnp.where(lane < SWA_HEAD_DIM, zero, k_all))
        vs = (jnp.where(row < SWA_HEAD_DIM, v_t, zero), jnp.where(row < SWA_HEAD_DIM, zero, v_t))
        cols = [q_ref[:, (kh * half + c) * LANES:(kh * half + c + 1) * LANES] for c in range(half)]
        ss = [_dot_t(ks[u], cols[c]) for c in range(half) for u in range(2)]
        es, rdens = [], []
        for c in range(half):
            for u in range(2):
                s = ss[2 * c + u]
                if banded:
                    s = s + bias
                sink = sink_ref[2 * (kh * half + c) + u] * LOG2E
                m = jnp.maximum(jnp.max(s, axis=0, keepdims=True), sink)
                e = jnp.exp2(s - m)
                rdens.append(1.0 / (jnp.sum(e, axis=0, keepdims=True) + jnp.exp2(sink - m)))
                es.append(e.astype(BF16))
        for c in range(half):
            o_t = (_dot(vs[0], es[2 * c]) * rdens[2 * c]
                   + _dot(vs[1], es[2 * c + 1]) * rdens[2 * c + 1])
            col = kh * half + c
            o_ref[:, col * LANES:(col + 1) * LANES] = o_t.T.astype(BF16)


def _swa_attn(sinks, q, kd, vd, batch, seq, tq):
    n, dq = q.shape
    kv_heads = kd.shape[0]
    group = (dq // SWA_HEAD_DIM) // kv_heads
    nq = seq // tq
    per = tq // WINDOW
    prev = lambda b, i: (0, jnp.maximum(i * per - 1, 0) + b * nq * per, 0)
    cur = lambda b, i: (0, b * nq + i, 0)
    return pl.pallas_call(
        functools.partial(_swa_attn_kernel, kv_heads=kv_heads, group=group, banded=True),
        grid=(batch, nq),
        in_specs=[pl.BlockSpec(memory_space=pltpu.SMEM),
                  pl.BlockSpec((tq, dq), lambda b, i: (b * nq + i, 0)),
                  pl.BlockSpec((kv_heads, WINDOW, LANES), prev),
                  pl.BlockSpec((kv_heads, tq, LANES), cur),
                  pl.BlockSpec((kv_heads, WINDOW, LANES), prev),
                  pl.BlockSpec((kv_heads, tq, LANES), cur)],
        out_specs=pl.BlockSpec((tq, dq), lambda b, i: (b * nq + i, 0)),
        out_shape=jax.ShapeDtypeStruct((n, dq), BF16),
        compiler_params=_params("parallel", "arbitrary"),
        name="swa_attn",
    )(sinks, q, kd, kd, vd, vd)


def _swa_attn_sample(sinks, q, kd_cache, kd_new, vd_cache, vd_new, batch, keep, t):
    n, dq = q.shape
    kv_heads = kd_new.shape[0]
    group = (dq // SWA_HEAD_DIM) // kv_heads
    blk = lambda b: (0, b, 0)
    return pl.pallas_call(
        functools.partial(_swa_attn_kernel, kv_heads=kv_heads, group=group, banded=False),
        grid=(batch,),
        in_specs=[pl.BlockSpec(memory_space=pltpu.SMEM),
                  pl.BlockSpec((t, dq), lambda b: (b, 0)),
                  pl.BlockSpec((kv_heads, keep, LANES), blk),
                  pl.BlockSpec((kv_heads, t, LANES), blk),
                  pl.BlockSpec((kv_heads, keep, LANES), blk),
                  pl.BlockSpec((kv_heads, t, LANES), blk)],
        out_specs=pl.BlockSpec((t, dq), lambda b: (b, 0)),
        out_shape=jax.ShapeDtypeStruct((n, dq), BF16),
        compiler_params=_params("parallel"),
        name="swa_attn_sample",
    )(sinks, q, kd_cache, kd_new, vd_cache, vd_new)


def _oproj_kernel(h_ref, o_ref, w_ref, out_ref):
    out_ref[...] = h_ref[...] + _dot(o_ref[...], w_ref[...])


def _oproj(h, o, w, layer, tm):
    n, d = h.shape
    row = lambda i: (i, 0)
    return pl.pallas_call(
        _oproj_kernel,
        grid=(n // tm,),
        in_specs=[pl.BlockSpec((tm, d), row), pl.BlockSpec((tm, o.shape[1]), row),
                  _resident(w.shape, layer)],
        out_specs=pl.BlockSpec((tm, d), row),
        out_shape=jax.ShapeDtypeStruct((n, d), F32),
        compiler_params=_params("parallel"),
        name="attn_out_proj",
    )(h, o, w)


def _ffn_kernel(h_ref, g_ref, wg_ref, wu_ref, wd_ref, out_ref, hn_ref):
    @pl.when(pl.program_id(1) == 0)
    def _():
        x = h_ref[...]
        hn_ref[...] = _rms(x, g_ref[...]).astype(BF16)
        out_ref[...] = x

    hn = hn_ref[...]
    a = _dot(hn, wg_ref[...])
    b = _dot(hn, wu_ref[...])
    act = (a * jax.nn.sigmoid(a) * b).astype(BF16)
    out_ref[...] += _dot(act, wd_ref[...])


def _ffn(h, g, wg, wu, wd, layer, tm, tf):
    n, d = h.shape
    f = wg.shape[2]
    row = lambda i, j: (i, 0)
    return pl.pallas_call(
        _ffn_kernel,
        grid=(n // tm, f // tf),
        in_specs=[pl.BlockSpec((tm, d), row), pl.BlockSpec((None, 1, d), lambda i, j: (layer, 0, 0)),
                  pl.BlockSpec((None, d, tf), lambda i, j: (layer, 0, j)),
                  pl.BlockSpec((None, d, tf), lambda i, j: (layer, 0, j)),
                  pl.BlockSpec((None, tf, d), lambda i, j: (layer, j, 0))],
        out_specs=pl.BlockSpec((tm, d), row),
        out_shape=jax.ShapeDtypeStruct((n, d), F32),
        scratch_shapes=[pltpu.VMEM((tm, d), BF16)],
        compiler_params=_params("parallel", "arbitrary"),
        name="swiglu_ffn",
    )(h, g, wg, wu, wd)


def _ple_kernel(h_ref, g_ref, wgate_ref, p_ref, wproj_ref, gfin_ref, *rest, final, n_cast):
    cast_in, out_ref, cast_out = rest[:n_cast], rest[n_cast], rest[n_cast + 1:]
    x = h_ref[...]
    gate = jax.nn.sigmoid(_dot(_rms(x, g_ref[...]).astype(BF16), wgate_ref[...]))
    y = x + gate * _dot(p_ref[...].astype(BF16), wproj_ref[...])
    if final:
        y = _rms(y, gfin_ref[...])
    out_ref[...] = y
    _cast_chunks(cast_in, cast_out)


def _ple(h, g, wgate, p, layer, wproj, gfin, final, tm, cast=(), cast_layer=0):
    n, d = h.shape
    steps = n // tm
    row = lambda i: (i, 0)
    items = [_cast_item(a, steps, cast_layer) for a in cast]
    cast_specs, cast_out_specs, cast_shapes = _cast_specs(items, lambda i: i)
    out = pl.pallas_call(
        functools.partial(_ple_kernel, final=final, n_cast=len(cast)),
        grid=(steps,),
        in_specs=[pl.BlockSpec((tm, d), row), _resident(g.shape, layer), _resident(wgate.shape, layer),
                  pl.BlockSpec((None, tm, p.shape[2]), lambda i: (layer, i, 0)),
                  _resident(wproj.shape, layer), _resident(gfin.shape)] + cast_specs,
        out_specs=[pl.BlockSpec((tm, d), row)] + cast_out_specs,
        out_shape=[jax.ShapeDtypeStruct((n, d), F32)] + cast_shapes,
        compiler_params=_params("parallel"),
        name="ple_embed",
    )(h, g, wgate, p, wproj, gfin, *cast)
    return out[0], tuple(out[1:])


def _angles(pos, r, theta):
    inv = np.power(np.float64(theta), -np.arange(0, r, 2, dtype=np.float64) / r)
    ang = pos.astype(np.float64)[:, None] * inv[None, :]
    return np.cos(ang).astype(np.float32), np.sin(ang).astype(np.float32)


def _mla_table(pos):
    cos, sin = _angles(pos, MLA_ROPE, MLA_THETA)
    return np.concatenate([cos, cos, -sin, sin], axis=-1)


def _swa_tables(pos):
    cos, sin = _angles(pos, SWA_ROT, SWA_THETA)
    s = cos.shape[0]
    rest = SWA_HEAD_DIM - SWA_ROT
    one = np.ones((s, rest), np.float32)
    zero = np.zeros((s, rest), np.float32)
    zh = np.zeros_like(sin)
    tc = np.concatenate([cos, cos, one], axis=-1)
    ts1 = np.concatenate([-sin, zh, zero], axis=-1)
    ts2 = np.concatenate([zh, sin, zero], axis=-1)
    return tuple(np.concatenate([t, t], axis=-1) for t in (tc, ts1, ts2))


def _tile_rows(tab, rows):
    tab = tab if tab.shape[0] >= rows else np.tile(tab, (rows // tab.shape[0], 1))
    return jnp.asarray(tab)


def _swap_halves(x):
    half = x.shape[-1] // 2
    return jnp.concatenate([x[..., half:], x[..., :half]], axis=-1)


def _prep_mla(j, g_attn_norm_i, w_mla_in, g_mla_q_a, w_mla_q_up, g_mla_kv_a, w_mla_kv_up, g_mla_q_nope,
              g_mla_q_rope, g_mla_k_nope, g_mla_k_rope):
    q_lora, kv_lora = g_mla_q_a.shape[1], g_mla_kv_a.shape[1]
    qk = MLA_NOPE + MLA_ROPE
    heads = w_mla_q_up.shape[2] // qk
    w_in = w_mla_in[j]
    k_r = w_in[:, q_lora + kv_lora:]
    w_in = jnp.concatenate([w_in, _swap_halves(k_r)], axis=-1)
    wq = w_mla_q_up[j].reshape(q_lora, heads, qk)
    rope = wq[:, :, MLA_NOPE:]
    wq = jnp.concatenate([wq[:, :, :MLA_NOPE].reshape(q_lora, heads // 2, 2 * MLA_NOPE),
                          jnp.concatenate([rope, _swap_halves(rope)], axis=-1).reshape(q_lora, heads // 2, 2 * LANES)],
                         axis=-1).reshape(q_lora, heads * 2 * LANES)
    wkv = w_mla_kv_up[j].reshape(kv_lora, heads, MLA_NOPE + MLA_V)
    dup = lambda g: jnp.concatenate([g, _swap_halves(g)], axis=-1)[None, :]
    two = lambda g: jnp.concatenate([g, g], axis=-1)
    gsum = np.kron(np.eye(2), np.ones((LANES, LANES)))
    return dict(
        heads=heads, gsum=jnp.asarray(gsum, BF16),
        g_attn=g_attn_norm_i[None, :],
        w_in=w_in.astype(BF16),
        g_q_a=g_mla_q_a[j][None, :], g_kv_a=g_mla_kv_a[j][None, :],
        g_k_rope=dup(g_mla_k_rope[j]), g_q_rope=two(dup(g_mla_q_rope[j])),
        g_q_nope=two(g_mla_q_nope[j][None, :]), g_k_nope=two(g_mla_k_nope[j][None, :]),
        w_q=wq.astype(BF16),
        w_kn=wkv[:, :, :MLA_NOPE].reshape(kv_lora, heads * MLA_NOPE).astype(BF16),
        w_v=wkv[:, :, MLA_NOPE:].reshape(kv_lora, heads * MLA_V).astype(BF16),
    )


def _prep_swa(j, g_attn_norm_i, w_swa_qkv, g_swa_q, g_swa_k, swa_sinks):
    q_heads = swa_sinks.shape[1]
    w = w_swa_qkv[j]
    d = w.shape[0]
    nq = q_heads * SWA_HEAD_DIM
    kv_heads = (w.shape[1] - nq) // (2 * SWA_HEAD_DIM)
    nk = kv_heads * SWA_HEAD_DIM
    wk = w[:, nq:nq + nk].reshape(d, kv_heads, SWA_HEAD_DIM)
    wv = w[:, nq + nk:].reshape(d, kv_heads, SWA_HEAD_DIM)
    wkv = jnp.concatenate([wk, wk, wv, wv], axis=-1).reshape(d, kv_heads * 4 * SWA_HEAD_DIM)
    gmat = np.kron(np.eye(2 * LANES // SWA_HEAD_DIM), np.ones((SWA_HEAD_DIM, SWA_HEAD_DIM)))
    dup = lambda g: jnp.concatenate([g, g], axis=-1)[None, :]
    return dict(
        q_heads=q_heads, kv_heads=kv_heads,
        g_attn=g_attn_norm_i[None, :],
        w_qkv=jnp.concatenate([w[:, :nq], wkv], axis=-1).astype(BF16),
        g_q=dup(g_swa_q[j]), g_k=dup(g_swa_k[j]),
        gmat=jnp.asarray(gmat, BF16),
        sinks=swa_sinks[j],
    )


def _kv_dup(t):
    b, l, kv, d = t.shape
    t = jnp.transpose(t, (2, 0, 1, 3)).reshape(kv, b * l, d)
    return jnp.concatenate([t, t], axis=-1).astype(BF16)


def _first_casts(shared, steps):
    raw = shared['raw']
    wg, wu, wd = raw['ffn']
    return ([_cast_item(wg, steps, 0), _cast_item(wu, steps, 0), _cast_item(wd, steps, 0, col_split=8),
             _cast_item(raw['ple_gate'], steps), _cast_item(raw['ple_proj'], steps, col_split=8),
             _cast_item(raw['mla_out'], steps), _cast_item(raw['swa_out'], steps)])


def _trunk(x, p, pos, caches, layers, shared, g_final, tm, tf):
    batch, seq, d = x.shape
    n = batch * seq
    h = x.reshape(n, d)
    rows = max(tm, seq)
    mla_tab = _tile_rows(_mla_table(pos), rows)
    swa_tabs = tuple(_tile_rows(t, rows) for t in _swa_tables(pos))
    depth = len(layers)
    n_mla = sum(kind == 'mla' for kind, _ in layers)
    p_rows = p.reshape(depth, n, -1)
    lat_kr, sks, svs = None, [], []
    for i, (kind, w) in enumerate(layers):
        j = i // 2
        if kind == 'mla':
            *lat_kr, q, k, v = _mla_proj(h, w, mla_tab, tm, j, n_mla, lat_kr)
            if caches is None:
                tq, hb = 512, 4
                first = 'ple' not in shared
                cast = _first_casts(shared, _mla_attn_steps(batch, w['heads'], seq, tq, hb)) if first else ()
                o, casted = _mla_attn(q, k, v, batch, seq, tq=tq, hb=hb, cast=cast)
                if first:
                    shared['ffn'] = {0: casted[:3]}
                    shared['ple'] = casted[3:5]
                    shared['out'] = dict(mla=casted[5], swa=casted[6])
            else:
                lat_c, kr_c = caches[0], caches[1]
                past = lat_c.shape[2]
                k_c, v_c = _mla_expand_call(lat_c.reshape(lat_c.shape[0], batch * past, -1),
                                            kr_c.reshape(kr_c.shape[0], batch * past, -1), j, w, tm=512)
                o = _mla_attn_sample(q, k_c, k, v_c, v, batch, past, seq, hb=4)
        else:
            q, kd, vd, kf, vf = _swa_proj(h, w, swa_tabs, tm)
            kv_heads = w['kv_heads']
            unhead = lambda t, rows: jnp.transpose(
                t.reshape(kv_heads, batch, seq, SWA_HEAD_DIM)[:, :, seq - rows:], (1, 2, 0, 3))
            if caches is None:
                o = _swa_attn(w['sinks'], q, kd, vd, batch, seq, tq=256)
                keep = min(WINDOW, seq)
                sks.append(unhead(kf, keep))
                svs.append(unhead(vf, keep))
            else:
                ck, cv = caches[2][j], caches[3][j]
                keep = ck.shape[1]
                o = _swa_attn_sample(w['sinks'], q, _kv_dup(ck), kd, _kv_dup(cv), vd, batch, keep, seq)
                sks.append(jnp.concatenate([ck, unhead(kf, seq)], axis=1)[:, -keep:])
                svs.append(jnp.concatenate([cv, unhead(vf, seq)], axis=1)[:, -keep:])
        h = _oproj(h, o, shared['out'][kind], j, tm)
        ffn_cast = shared['ffn']
        h = _ffn(h, shared['raw']['ffn_g'][i:i + 1], *ffn_cast[i], 0, min(n, 2 * tm), tf)
        cast = shared['raw']['ffn'] if (i + 1 not in ffn_cast and i + 1 < depth) else ()
        h, casted = _ple(h, shared['raw']['ple_g'], shared['ple'][0], p_rows, i, shared['ple'][1], g_final,
                         i == depth - 1, tm, cast, i + 1)
        if casted:
            ffn_cast[i + 1] = casted
    lat, kr = (t.reshape(n_mla, batch, seq, -1) for t in lat_kr)
    return h.reshape(batch, seq, d), lat, kr, jnp.stack(sks), jnp.stack(svs)


def kernel(x_prompt, x_sample, p_prompt, p_sample, cache_mla_latent, cache_mla_krope, state_swa_k, state_swa_v, g_attn_norm, w_mla_in, g_mla_q_a, w_mla_q_up, g_mla_kv_a, w_mla_kv_up, g_mla_q_nope, g_mla_q_rope, g_mla_k_nope, g_mla_k_rope, w_mla_out, w_swa_qkv, g_swa_q, g_swa_k, swa_sinks, w_swa_out, g_ffn_norm, w_ffn_gate, w_ffn_up, w_ffn_down, g_ple_norm, w_ple_gate, w_ple_proj, g_final):
    depth = g_attn_norm.shape[0]
    layers = []
    for i in range(depth):
        j = i // 2
        if i % 2 == 0:
            layers.append(('mla', _prep_mla(j, g_attn_norm[i], w_mla_in, g_mla_q_a, w_mla_q_up,
                                            g_mla_kv_a, w_mla_kv_up, g_mla_q_nope, g_mla_q_rope,
                                            g_mla_k_nope, g_mla_k_rope)))
        else:
            layers.append(('swa', _prep_swa(j, g_attn_norm[i], w_swa_qkv, g_swa_q, g_swa_k, swa_sinks)))
    shared = dict(raw=dict(ffn=(w_ffn_gate, w_ffn_up, w_ffn_down), ffn_g=g_ffn_norm[:, None, :],
                           ple_gate=w_ple_gate, ple_proj=w_ple_proj, ple_g=g_ple_norm[:, None, :],
                           mla_out=w_mla_out, swa_out=w_swa_out))
    gfin = g_final[None, :]

    seq = x_prompt.shape[1]
    t = x_sample.shape[1]
    past = cache_mla_latent.shape[2]
    pos_p = np.arange(seq)
    pos_s = past + np.arange(t)
    y_p, lat_p, kr_p, sk_p, sv_p = _trunk(x_prompt, p_prompt, pos_p, None, layers, shared,
                                          gfin, tm=512, tf=512)
    caches = (cache_mla_latent, cache_mla_krope, state_swa_k, state_swa_v)
    n_s = x_sample.shape[0] * t
    tf_s = next(c for c in (1408, 512) if w_ffn_gate.shape[2] % c == 0)
    y_s, lat_s, kr_s, sk_s, sv_s = _trunk(x_sample, p_sample, pos_s, caches, layers, shared,
                                          gfin, tm=n_s, tf=tf_s)
    return (y_p, y_s, lat_p, kr_p, sk_p, sv_p, lat_s, kr_s, sk_s, sv_s)
```

```python
import functools

import numpy as np
import jax
import jax.numpy as jnp
from jax import lax
from jax.experimental import pallas as pl
from jax.experimental.pallas import tpu as pltpu

F32 = jnp.float32
BF16 = jnp.bfloat16

EPS = 1e-6
CHUNK = 64
WINDOW = 128
MLA_THETA = 10000.0
SWA_THETA = 500000.0
MLA_NOPE = 128
MLA_ROPE = 64
MLA_V = 128
SWA_HEAD_DIM = 64
SWA_ROT = SWA_HEAD_DIM // 4

LANES = 128
VMEM_LIMIT = 56 * 1024 * 1024
NEG = -1e30
LOG2E = 1.4426950408889634


def _params(*sem):
    return pltpu.CompilerParams(dimension_semantics=sem, vmem_limit_bytes=VMEM_LIMIT)


def _resident(shape, layer=None):
    if layer is None:
        zeros = (0,) * len(shape)
        return pl.BlockSpec(shape, lambda *_: zeros, pipeline_mode=pl.Buffered(1))
    index = (layer,) + (0,) * (len(shape) - 1)
    return pl.BlockSpec((None,) + tuple(shape[1:]), lambda *_: index, pipeline_mode=pl.Buffered(1))


def _cast_item(a, steps, layer=None, col_split=1):
    nl, r, c = a.shape
    lead, first = (nl, 0) if layer is None else (1, layer)
    return dict(a=a, block=(lead, r * col_split // steps, c // col_split),
                in_idx=lambda s: (first, s // col_split, s % col_split),
                out_idx=lambda s: (0, s // col_split, s % col_split),
                out_shape=jax.ShapeDtypeStruct((lead, r, c), BF16))


def _cast_specs(items, step_of):
    ins = [pl.BlockSpec(it['block'], lambda *g, it=it: it['in_idx'](step_of(*g))) for it in items]
    outs = [pl.BlockSpec(it['block'], lambda *g, it=it: it['out_idx'](step_of(*g))) for it in items]
    return ins, outs, [it['out_shape'] for it in items]


def _cast_chunks(srcs, dsts):
    for src, dst in zip(srcs, dsts):
        dst[...] = src[...].astype(BF16)


def _rms(x, g):
    return x * lax.rsqrt(jnp.mean(x * x, axis=-1, keepdims=True) + EPS) * g


def _dot(a, b):
    return jnp.dot(a, b, preferred_element_type=F32)


def _dot_t(a, b):
    return lax.dot_general(a, b, (((1,), (1,)), ((), ())), preferred_element_type=F32)


def _rope_dup(x, g, tab):
    ss = jnp.sum(x * x, axis=-1, keepdims=True)
    y = x * lax.rsqrt(ss * (1.0 / LANES) + EPS) * g
    t = y * tab
    return t + pltpu.roll(t, LANES // 2, axis=1)


def _mla_heads(c_q, latb, krb, tab, wq_ref, wkn_ref, wv_ref, gqn, gqr, gkn, gsum_ref,
               q_ref, k_ref, v_ref, heads):
    gsum = gsum_ref[...]
    pair = 2 * LANES

    def inv_rms(x):
        return lax.rsqrt(_dot((x * x).astype(BF16), gsum) * (1.0 / LANES) + EPS)

    def finish(p, xq, xk, xv):
        kn = xk * inv_rms(xk) * gkn
        if xq is not None:
            qn = xq[:, :pair] * inv_rms(xq[:, :pair]) * gqn
            t = xq[:, pair:] * inv_rms(xq[:, pair:]) * gqr * tab
        for u in range(2):
            h = 2 * p + u
            sl = slice(u * LANES, (u + 1) * LANES)
            k_ref[h, :, 0:MLA_NOPE] = kn[:, sl].astype(BF16)
            k_ref[h, :, MLA_NOPE:MLA_NOPE + MLA_ROPE] = krb
            v_ref[h] = xv[:, sl].astype(BF16)
            if xq is not None:
                q_ref[h, :, 0:MLA_NOPE] = qn[:, sl].astype(BF16)
                tu = t[:, sl]
                qr = tu + pltpu.roll(tu, LANES // 2, axis=1)
                q_ref[h, :, MLA_NOPE:MLA_NOPE + MLA_ROPE] = qr[:, :MLA_ROPE].astype(BF16)

    pending = None
    for p in range(heads // 2):
        xq = None if c_q is None else _dot(c_q, wq_ref[:, p * 2 * pair:(p + 1) * 2 * pair])
        xk = _dot(latb, wkn_ref[:, p * pair:(p + 1) * pair])
        xv = _dot(latb, wv_ref[:, p * pair:(p + 1) * pair])
        if pending is not None:
            finish(*pending)
        pending = (p, xq, xk, xv)
    finish(*pending)


def _mla_proj_kernel(h_ref, gattn_ref, win_ref, gqa_ref, gkva_ref, gkr_ref, tab_ref, wq_ref,
                     gqn_ref, gqr_ref, wkn_ref, wv_ref, gkn_ref, gsum_ref, *rest,
                     heads, q_lora, kv_lora, scale):
    lat_ref, kr_ref, q_ref, k_ref, v_ref = rest[-5:]
    hn = _rms(h_ref[...], gattn_ref[...]).astype(BF16)
    a = _dot(hn, win_ref[...])
    c_q = _rms(a[:, :q_lora], gqa_ref[...]).astype(BF16)
    lat = _rms(a[:, q_lora:q_lora + kv_lora], gkva_ref[...])
    lat_ref[...] = lat
    tab = tab_ref[...]
    kr = _rope_dup(a[:, q_lora + kv_lora:], gkr_ref[...], tab)
    kr_ref[...] = kr[:, :MLA_ROPE]
    _mla_heads(c_q, lat.astype(BF16), kr[:, :MLA_ROPE].astype(BF16), jnp.concatenate([tab, tab], axis=1),
               wq_ref, wkn_ref, wv_ref, gqn_ref[...] * scale, gqr_ref[...] * scale, gkn_ref[...],
               gsum_ref, q_ref, k_ref, v_ref, heads)


def _mla_expand_kernel(lat_ref, kr_ref, wkn_ref, wv_ref, gkn_ref, gsum_ref, k_ref, v_ref, *, heads):
    _mla_heads(None, lat_ref[...].astype(BF16), kr_ref[...].astype(BF16), None, None, wkn_ref, wv_ref,
               None, None, gkn_ref[...], gsum_ref, None, k_ref, v_ref, heads)


def _mla_proj(h, w, tab, tm, slab, n_slabs, stacks=None):
    n, d = h.shape
    heads = w['heads']
    q_lora, kv_lora = w['g_q_a'].shape[1], w['g_kv_a'].shape[1]
    qk = MLA_NOPE + MLA_ROPE
    nt = tab.shape[0] // tm
    row = lambda i: (i, 0)
    hrow = lambda i: (0, i, 0)
    srow = lambda i: (slab, i, 0)
    kern = functools.partial(_mla_proj_kernel, heads=heads, q_lora=q_lora, kv_lora=kv_lora,
                             scale=qk ** -0.5 * LOG2E)
    weights = (w['g_attn'], w['w_in'], w['g_q_a'], w['g_kv_a'], w['g_k_rope'], tab, w['w_q'],
               w['g_q_nope'], w['g_q_rope'], w['w_kn'], w['w_v'], w['g_k_nope'], w['gsum'])
    stacks = () if stacks is None else tuple(stacks)
    n_in = 1 + len(weights)
    return pl.pallas_call(
        kern,
        grid=(n // tm,),
        in_specs=[
            pl.BlockSpec((tm, d), row),
            _resident(w['g_attn'].shape), _resident(w['w_in'].shape),
            _resident(w['g_q_a'].shape), _resident(w['g_kv_a'].shape), _resident(w['g_k_rope'].shape),
            pl.BlockSpec((tm, LANES), lambda i: (i % nt, 0)),
            _resident(w['w_q'].shape), _resident(w['g_q_nope'].shape), _resident(w['g_q_rope'].shape),
            _resident(w['w_kn'].shape), _resident(w['w_v'].shape), _resident(w['g_k_nope'].shape),
            _resident(w['gsum'].shape),
        ] + [pl.BlockSpec(memory_space=pl.ANY)] * len(stacks),
        out_specs=[
            pl.BlockSpec((None, tm, kv_lora), srow),
            pl.BlockSpec((None, tm, MLA_ROPE), srow),
            pl.BlockSpec((heads, tm, qk), hrow),
            pl.BlockSpec((heads, tm, qk), hrow),
            pl.BlockSpec((heads, tm, MLA_V), hrow),
        ],
        out_shape=[
            jax.ShapeDtypeStruct((n_slabs, n, kv_lora), F32),
            jax.ShapeDtypeStruct((n_slabs, n, MLA_ROPE), F32),
            jax.ShapeDtypeStruct((heads, n, qk), BF16),
            jax.ShapeDtypeStruct((heads, n, qk), BF16),
            jax.ShapeDtypeStruct((heads, n, MLA_V), BF16),
        ],
        input_output_aliases={n_in + s: s for s in range(len(stacks))},
        compiler_params=_params("parallel"),
        name="mla_proj",
    )(h, *weights, *stacks)


def _mla_expand_call(lat, kr, layer, w, tm):
    _, n, kv_lora = lat.shape
    heads = w['heads']
    qk = MLA_NOPE + MLA_ROPE
    hrow = lambda i: (0, i, 0)
    return pl.pallas_call(
        functools.partial(_mla_expand_kernel, heads=heads),
        grid=(n // tm,),
        in_specs=[pl.BlockSpec((None, tm, kv_lora), lambda i: (layer, i, 0)),
                  pl.BlockSpec((None, tm, MLA_ROPE), lambda i: (layer, i, 0)),
                  _resident(w['w_kn'].shape), _resident(w['w_v'].shape),
                  _resident(w['g_k_nope'].shape), _resident(w['gsum'].shape)],
        out_specs=[pl.BlockSpec((heads, tm, qk), hrow), pl.BlockSpec((heads, tm, MLA_V), hrow)],
        out_shape=[jax.ShapeDtypeStruct((heads, n, qk), BF16),
                   jax.ShapeDtypeStruct((heads, n, MLA_V), BF16)],
        compiler_params=_params("parallel"),
        name="mla_expand",
    )(lat, kr, w['w_kn'], w['w_v'], w['g_k_nope'], w['gsum'])


def _mla_attn_kernel(q_ref, k_ref, v_ref, *rest, hb, tq, n_cast):
    cast_in, o_ref, cast_out, s_ref = rest[:n_cast], rest[n_cast], rest[n_cast + 1:-1], rest[-1]
    _cast_chunks(cast_in, cast_out)
    i = pl.program_id(2)
    kc = lax.broadcasted_iota(jnp.int32, (tq, tq), 0) // CHUNK
    qc = lax.broadcasted_iota(jnp.int32, (tq, tq), 1) // CHUNK
    diag_bias = jnp.where(kc <= qc, 0.0, NEG).astype(F32)
    qs = [q_ref[h] for h in range(hb)]

    def scores(j, slot):
        start = pl.multiple_of(j * tq, tq)
        mx = []
        for h in range(hb):
            s = _dot_t(k_ref[h, pl.ds(start, tq), :], qs[h])
            s_ref[slot, h] = s
            mx.append(jnp.max(s, axis=0, keepdims=True))
        return tuple(mx)

    def update(j, slot, carry, mx, bias=None):
        start = pl.multiple_of(j * tq, tq)
        ps, stats = [], []
        for h in range(hb):
            m, l, _ = carry[h]
            if bias is None:
                s, blk_max = s_ref[slot, h], mx[h]
            else:
                s = s_ref[slot, h] + bias
                blk_max = jnp.max(s, axis=0, keepdims=True)
            m_new = jnp.maximum(m, blk_max)
            alpha = jnp.exp2(m - m_new)
            p = jnp.exp2(s - m_new)
            stats.append((m_new, alpha * l + jnp.sum(p, axis=0, keepdims=True), alpha))
            ps.append(p.astype(BF16))
        out = []
        for h in range(hb):
            pv = lax.dot_general(v_ref[h, pl.ds(start, tq), :], ps[h],
                                 (((0,), (0,)), ((), ())), preferred_element_type=F32)
            m_new, l, alpha = stats[h]
            out.append((m_new, l, alpha * carry[h][2] + pv))
        return tuple(out)

    def finish(carry):
        for h in range(hb):
            _, l, acc = carry[h]
            o_ref[:, h * MLA_V:(h + 1) * MLA_V] = (acc / l).T.astype(BF16)

    def pair(t, state):
        carry, mx0 = state
        mx1 = scores(2 * t + 1, 1)
        carry = update(2 * t, 0, carry, mx0)
        mx0 = scores(2 * t + 2, 0)
        return update(2 * t + 1, 1, carry, mx1), mx0

    init = tuple((jnp.full((1, tq), NEG, F32), jnp.zeros((1, tq), F32), jnp.zeros((MLA_V, tq), F32))
                 for _ in range(hb))
    carry, mx0 = lax.fori_loop(0, i // 2, pair, (init, scores(0, 0)))

    @pl.when(i % 2 == 0)
    def _():
        finish(update(i, 0, carry, None, diag_bias))

    @pl.when(i % 2 == 1)
    def _():
        scores(i, 1)
        finish(update(i, 1, update(i - 1, 0, carry, mx0), None, diag_bias))


def _mla_attn_steps(batch, heads, seq, tq, hb):
    return batch * (heads // hb) * (seq // tq)


def _mla_attn(q, k, v, batch, seq, tq, hb, cast=()):
    heads, n, qk = q.shape
    nq = seq // tq
    ng = heads // hb
    kv_map = lambda b, g, i: (g, b, 0)
    cast_in, cast_out, cast_shapes = _cast_specs(cast, lambda b, g, i: (b * ng + g) * nq + i)
    out = pl.pallas_call(
        functools.partial(_mla_attn_kernel, hb=hb, tq=tq, n_cast=len(cast)),
        grid=(batch, ng, nq),
        in_specs=[pl.BlockSpec((hb, tq, qk), lambda b, g, i: (g, b * nq + i, 0)),
                  pl.BlockSpec((hb, seq, qk), kv_map),
                  pl.BlockSpec((hb, seq, MLA_V), kv_map)] + cast_in,
        out_specs=[pl.BlockSpec((tq, hb * MLA_V), lambda b, g, i: (b * nq + i, g))] + cast_out,
        out_shape=[jax.ShapeDtypeStruct((n, heads * MLA_V), BF16)] + cast_shapes,
        scratch_shapes=[pltpu.VMEM((2, hb, tq, tq), F32)],
        compiler_params=_params("parallel", "parallel", "arbitrary"),
        name="mla_attn",
    )(q, k, v, *[it['a'] for it in cast])
    return out[0], tuple(out[1:])


def _mla_attn_sample_kernel(q_ref, kc_ref, kn_ref, vc_ref, vn_ref, o_ref, *, hb):
    ss = [(_dot_t(kc_ref[h], q_ref[h]), _dot_t(kn_ref[h], q_ref[h])) for h in range(hb)]
    ps = []
    for s_c, s_n in ss:
        m = jnp.maximum(jnp.max(s_c, axis=0, keepdims=True), jnp.max(s_n, axis=0, keepdims=True))
        p_c = jnp.exp2(s_c - m)
        p_n = jnp.exp2(s_n - m)
        l = jnp.sum(p_c, axis=0, keepdims=True) + jnp.sum(p_n, axis=0, keepdims=True)
        ps.append((p_c.astype(BF16), p_n.astype(BF16), l))
    tn = (((0,), (0,)), ((), ()))
    for h, (p_c, p_n, l) in enumerate(ps):
        acc = (lax.dot_general(vc_ref[h], p_c, tn, preferred_element_type=F32)
               + lax.dot_general(vn_ref[h], p_n, tn, preferred_element_type=F32))
        o_ref[:, h * MLA_V:(h + 1) * MLA_V] = (acc / l).T.astype(BF16)


def _mla_attn_sample(q, k_cache, k_new, v_cache, v_new, batch, past, t, hb):
    heads, n, qk = q.shape
    q_pos = past + np.arange(t)
    k_pos = np.arange(past + t)
    assert np.all((k_pos // CHUNK)[None, :] <= (q_pos // CHUNK)[:, None])
    m3 = lambda b, g: (g, b, 0)
    return pl.pallas_call(
        functools.partial(_mla_attn_sample_kernel, hb=hb),
        grid=(batch, heads // hb),
        in_specs=[pl.BlockSpec((hb, t, qk), m3), pl.BlockSpec((hb, past, qk), m3),
                  pl.BlockSpec((hb, t, qk), m3), pl.BlockSpec((hb, past, MLA_V), m3),
                  pl.BlockSpec((hb, t, MLA_V), m3)],
        out_specs=pl.BlockSpec((t, hb * MLA_V), lambda b, g: (b, g)),
        out_shape=jax.ShapeDtypeStruct((n, heads * MLA_V), BF16),
        compiler_params=_params("parallel", "parallel"),
        name="mla_attn_sample",
    )(q, k_cache, k_new, v_cache, v_new)


def _swa_proj_kernel(h_ref, gattn_ref, w_ref, gq_ref, gk_ref, tc_ref, ts1_ref, ts2_ref, gmat_ref,
                     q_ref, kd_ref, vd_ref, kf_ref, vf_ref, *, q_cols, kv_heads, scale):
    hn = _rms(h_ref[...], gattn_ref[...]).astype(BF16)
    tc, ts1, ts2 = tc_ref[...], ts1_ref[...], ts2_ref[...]
    gmat = gmat_ref[...]

    def rope(x, ss, g):
        y = x * lax.rsqrt(ss * (1.0 / SWA_HEAD_DIM) + EPS) * g
        return (y * tc + pltpu.roll(y, LANES - SWA_ROT // 2, axis=1) * ts1
                + pltpu.roll(y, SWA_ROT // 2, axis=1) * ts2)

    gq = gq_ref[...] * scale
    gk = gk_ref[...]
    n_q = q_cols // 2

    def finish(c, x2):
        ss = _dot((x2 * x2).astype(BF16), gmat)
        if c < n_q:
            for u in range(2):
                col = 2 * c + u
                sl = slice(u * LANES, (u + 1) * LANES)
                q_ref[:, col * LANES:(col + 1) * LANES] = rope(x2[:, sl], ss[:, sl], gq).astype(BF16)
        else:
            kh = c - n_q
            kd = rope(x2[:, :LANES], ss[:, :LANES], gk)
            kd_ref[kh] = kd.astype(BF16)
            kf_ref[kh] = kd[:, :SWA_HEAD_DIM]
            vd_ref[kh] = x2[:, LANES:].astype(BF16)
            vf_ref[kh] = x2[:, LANES:LANES + SWA_HEAD_DIM]

    pending = None
    for c in range(n_q + kv_heads):
        x2 = _dot(hn, w_ref[:, c * 256:(c + 1) * 256])
        if pending is not None:
            finish(*pending)
        pending = (c, x2)
    finish(*pending)


def _swa_proj(h, w, tabs, tm):
    n, d = h.shape
    q_heads, kv_heads = w['q_heads'], w['kv_heads']
    q_cols = q_heads * SWA_HEAD_DIM // LANES
    nt = tabs[0].shape[0] // tm
    row = lambda i: (i, 0)
    hrow = lambda i: (0, i, 0)
    tspec = pl.BlockSpec((tm, LANES), lambda i: (i % nt, 0))
    kern = functools.partial(_swa_proj_kernel, q_cols=q_cols, kv_heads=kv_heads,
                             scale=SWA_HEAD_DIM ** -0.5 * LOG2E)
    return pl.pallas_call(
        kern,
        grid=(n // tm,),
        in_specs=[pl.BlockSpec((tm, d), row), _resident(w['g_attn'].shape),
                  _resident(w['w_qkv'].shape), _resident(w['g_q'].shape), _resident(w['g_k'].shape),
                  tspec, tspec, tspec, _resident(w['gmat'].shape)],
        out_specs=[pl.BlockSpec((tm, q_cols * LANES), row),
                   pl.BlockSpec((kv_heads, tm, LANES), hrow),
                   pl.BlockSpec((kv_heads, tm, LANES), hrow),
                   pl.BlockSpec((kv_heads, tm, SWA_HEAD_DIM), hrow),
                   pl.BlockSpec((kv_heads, tm, SWA_HEAD_DIM), hrow)],
        out_shape=[jax.ShapeDtypeStruct((n, q_cols * LANES), BF16),
                   jax.ShapeDtypeStruct((kv_heads, n, LANES), BF16),
                   jax.ShapeDtypeStruct((kv_heads, n, LANES), BF16),
                   jax.ShapeDtypeStruct((kv_heads, n, SWA_HEAD_DIM), F32),
                   jax.ShapeDtypeStruct((kv_heads, n, SWA_HEAD_DIM), F32)],
        compiler_params=_params("parallel"),
        name="swa_proj",
    )(h, w['g_attn'], w['w_qkv'], w['g_q'], w['g_k'], *tabs, w['gmat'])


def _swa_attn_kernel(sink_ref, q_ref, kp_ref, kc_ref, vp_ref, vc_ref, o_ref, *,
                     kv_heads, group, banded):
    tq = q_ref.shape[0]
    n_p, n_c = kp_ref.shape[1], kc_ref.shape[1]
    n_k = n_p + n_c
    lane = lax.broadcasted_iota(jnp.int32, (1, LANES), 1)
    row = lax.broadcasted_iota(jnp.int32, (LANES, 1), 0)
    if banded:
        assert n_p == WINDOW == LANES
        tiles = [(t * LANES, LANES, t * LANES, 2 * LANES) for t in range(tq // LANES)]
    else:
        tiles = [(0, tq, 0, n_k)]
    biases = []
    for q0, qn, k0, kn in tiles:
        if not banded:
            biases.append(None)
            continue
        win = WINDOW // CHUNK
        kc = (lax.broadcasted_iota(jnp.int32, (kn, qn), 0) + k0) // CHUNK - n_p // CHUNK
        qc = (lax.broadcasted_iota(jnp.int32, (kn, qn), 1) + q0) // CHUNK
        ok = (kc <= qc) & (kc >= qc - win) & ((kc >= 0) | (pl.program_id(1) > 0))
        biases.append(jnp.where(ok, 0.0, NEG).astype(F32))
    zero = jnp.zeros((), BF16)
    half = group // 2
    for kh in range(kv_heads):
        k_all = jnp.concatenate([kp_ref[kh], kc_ref[kh]], axis=0)
        v_t = jnp.concatenate([vp_ref[kh], vc_ref[kh]], axis=0).T
        ks = (jnp.where(lane < SWA_HEAD_DIM, k_all, zero), jnp.where(lane < SWA_HEAD_DIM, zero, k_all))
        vs = (jnp.where(row < SWA_HEAD_DIM, v_t, zero), jnp.where(row < SWA_HEAD_DIM, zero, v_t))
        cols = [q_ref[:, (kh * half + c) * LANES:(kh * half + c + 1) * LANES] for c in range(half)]
        for (q0, qn, k0, kn), bias in zip(tiles, biases):
            ss = [_dot_t(ks[u][k0:k0 + kn], cols[c][q0:q0 + qn])
                  for c in range(half) for u in range(2)]
            es, rdens = [], []
            for c in range(half):
                for u in range(2):
                    s = ss[2 * c + u]
                    if bias is not None:
                        s = s + bias
                    sink = sink_ref[2 * (kh * half + c) + u] * LOG2E
                    m = jnp.maximum(jnp.max(s, axis=0, keepdims=True), sink)
                    e = jnp.exp2(s - m)
                    rdens.append(1.0 / (jnp.sum(e, axis=0, keepdims=True) + jnp.exp2(sink - m)))
                    es.append(e.astype(BF16))
            for c in range(half):
                o_t = (_dot(vs[0][:, k0:k0 + kn], es[2 * c]) * rdens[2 * c]
                       + _dot(vs[1][:, k0:k0 + kn], es[2 * c + 1]) * rdens[2 * c + 1])
                col = kh * half + c
                o_ref[q0:q0 + qn, col * LANES:(col + 1) * LANES] = o_t.T.astype(BF16)


def _swa_attn(sinks, q, kd, vd, batch, seq, tq):
    n, dq = q.shape
    kv_heads = kd.shape[0]
    group = (dq // SWA_HEAD_DIM) // kv_heads
    nq = seq // tq
    per = tq // WINDOW
    prev = lambda b, i: (0, jnp.maximum(i * per - 1, 0) + b * nq * per, 0)
    cur = lambda b, i: (0, b * nq + i, 0)
    return pl.pallas_call(
        functools.partial(_swa_attn_kernel, kv_heads=kv_heads, group=group, banded=True),
        grid=(batch, nq),
        in_specs=[pl.BlockSpec(memory_space=pltpu.SMEM),
                  pl.BlockSpec((tq, dq), lambda b, i: (b * nq + i, 0)),
                  pl.BlockSpec((kv_heads, WINDOW, LANES), prev),
                  pl.BlockSpec((kv_heads, tq, LANES), cur),
                  pl.BlockSpec((kv_heads, WINDOW, LANES), prev),
                  pl.BlockSpec((kv_heads, tq, LANES), cur)],
        out_specs=pl.BlockSpec((tq, dq), lambda b, i: (b * nq + i, 0)),
        out_shape=jax.ShapeDtypeStruct((n, dq), BF16),
        compiler_params=_params("parallel", "arbitrary"),
        name="swa_attn",
    )(sinks, q, kd, kd, vd, vd)


def _swa_attn_sample(sinks, q, kd_cache, kd_new, vd_cache, vd_new, batch, keep, t):
    n, dq = q.shape
    kv_heads = kd_new.shape[0]
    group = (dq // SWA_HEAD_DIM) // kv_heads
    blk = lambda b: (0, b, 0)
    return pl.pallas_call(
        functools.partial(_swa_attn_kernel, kv_heads=kv_heads, group=group, banded=False),
        grid=(batch,),
        in_specs=[pl.BlockSpec(memory_space=pltpu.SMEM),
                  pl.BlockSpec((t, dq), lambda b: (b, 0)),
                  pl.BlockSpec((kv_heads, keep, LANES), blk),
                  pl.BlockSpec((kv_heads, t, LANES), blk),
                  pl.BlockSpec((kv_heads, keep, LANES), blk),
                  pl.BlockSpec((kv_heads, t, LANES), blk)],
        out_specs=pl.BlockSpec((t, dq), lambda b: (b, 0)),
        out_shape=jax.ShapeDtypeStruct((n, dq), BF16),
        compiler_params=_params("parallel"),
        name="swa_attn_sample",
    )(sinks, q, kd_cache, kd_new, vd_cache, vd_new)


def _oproj_kernel(h_ref, o_ref, w_ref, out_ref):
    out_ref[...] = h_ref[...] + _dot(o_ref[...], w_ref[...])


def _oproj(h, o, w, layer, tm):
    n, d = h.shape
    row = lambda i: (i, 0)
    return pl.pallas_call(
        _oproj_kernel,
        grid=(n // tm,),
        in_specs=[pl.BlockSpec((tm, d), row), pl.BlockSpec((tm, o.shape[1]), row),
                  _resident(w.shape, layer)],
        out_specs=pl.BlockSpec((tm, d), row),
        out_shape=jax.ShapeDtypeStruct((n, d), F32),
        compiler_params=_params("parallel"),
        name="attn_out_proj",
    )(h, o, w)


def _ffn_kernel(h_ref, g_ref, wg_ref, wu_ref, wd_ref, out_ref, hn_ref):
    @pl.when(pl.program_id(1) == 0)
    def _():
        x = h_ref[...]
        hn_ref[...] = _rms(x, g_ref[...]).astype(BF16)
        out_ref[...] = x

    hn = hn_ref[...]
    w = wg_ref.shape[1] // 2
    acts = []
    for c in range(2):
        a = _dot(hn, wg_ref[:, c * w:(c + 1) * w])
        b = _dot(hn, wu_ref[:, c * w:(c + 1) * w])
        acts.append((a * jax.nn.sigmoid(a) * b).astype(BF16))
    out_ref[...] += _dot(acts[0], wd_ref[0:w, :]) + _dot(acts[1], wd_ref[w:2 * w, :])


def _ffn(h, g, wg, wu, wd, layer, tm, tf):
    n, d = h.shape
    f = wg.shape[2]
    row = lambda i, j: (i, 0)
    return pl.pallas_call(
        _ffn_kernel,
        grid=(n // tm, f // tf),
        in_specs=[pl.BlockSpec((tm, d), row), pl.BlockSpec((None, 1, d), lambda i, j: (layer, 0, 0)),
                  pl.BlockSpec((None, d, tf), lambda i, j: (layer, 0, j)),
                  pl.BlockSpec((None, d, tf), lambda i, j: (layer, 0, j)),
                  pl.BlockSpec((None, tf, d), lambda i, j: (layer, j, 0))],
        out_specs=pl.BlockSpec((tm, d), row),
        out_shape=jax.ShapeDtypeStruct((n, d), F32),
        scratch_shapes=[pltpu.VMEM((tm, d), BF16)],
        compiler_params=_params("parallel", "arbitrary"),
        name="swiglu_ffn",
    )(h, g, wg, wu, wd)


def _ple_kernel(h_ref, g_ref, wgate_ref, p_ref, wproj_ref, gfin_ref, *rest, final, n_cast):
    cast_in, out_ref, cast_out = rest[:n_cast], rest[n_cast], rest[n_cast + 1:]
    x = h_ref[...]
    gate = jax.nn.sigmoid(_dot(_rms(x, g_ref[...]).astype(BF16), wgate_ref[...]))
    y = x + gate * _dot(p_ref[...].astype(BF16), wproj_ref[...])
    if final:
        y = _rms(y, gfin_ref[...])
    out_ref[...] = y
    _cast_chunks(cast_in, cast_out)


def _ple(h, g, wgate, p, layer, wproj, gfin, final, tm, cast=(), cast_layer=0):
    n, d = h.shape
    steps = n // tm
    row = lambda i: (i, 0)
    items = [_cast_item(a, steps, cast_layer) for a in cast]
    cast_specs, cast_out_specs, cast_shapes = _cast_specs(items, lambda i: i)
    out = pl.pallas_call(
        functools.partial(_ple_kernel, final=final, n_cast=len(cast)),
        grid=(steps,),
        in_specs=[pl.BlockSpec((tm, d), row), _resident(g.shape, layer), _resident(wgate.shape, layer),
                  pl.BlockSpec((None, tm, p.shape[2]), lambda i: (layer, i, 0)),
                  _resident(wproj.shape, layer), _resident(gfin.shape)] + cast_specs,
        out_specs=[pl.BlockSpec((tm, d), row)] + cast_out_specs,
        out_shape=[jax.ShapeDtypeStruct((n, d), F32)] + cast_shapes,
        compiler_params=_params("parallel"),
        name="ple_embed",
    )(h, g, wgate, p, wproj, gfin, *cast)
    return out[0], tuple(out[1:])


def _angles(pos, r, theta):
    inv = np.power(np.float64(theta), -np.arange(0, r, 2, dtype=np.float64) / r)
    ang = pos.astype(np.float64)[:, None] * inv[None, :]
    return np.cos(ang).astype(np.float32), np.sin(ang).astype(np.float32)


def _mla_table(pos):
    cos, sin = _angles(pos, MLA_ROPE, MLA_THETA)
    return np.concatenate([cos, cos, -sin, sin], axis=-1)


def _swa_tables(pos):
    cos, sin = _angles(pos, SWA_ROT, SWA_THETA)
    s = cos.shape[0]
    rest = SWA_HEAD_DIM - SWA_ROT
    one = np.ones((s, rest), np.float32)
    zero = np.zeros((s, rest), np.float32)
    zh = np.zeros_like(sin)
    tc = np.concatenate([cos, cos, one], axis=-1)
    ts1 = np.concatenate([-sin, zh, zero], axis=-1)
    ts2 = np.concatenate([zh, sin, zero], axis=-1)
    return tuple(np.concatenate([t, t], axis=-1) for t in (tc, ts1, ts2))


def _tile_rows(tab, rows):
    tab = tab if tab.shape[0] >= rows else np.tile(tab, (rows // tab.shape[0], 1))
    return jnp.asarray(tab)


def _swap_halves(x):
    half = x.shape[-1] // 2
    return jnp.concatenate([x[..., half:], x[..., :half]], axis=-1)


def _prep_mla(j, g_attn_norm_i, w_mla_in, g_mla_q_a, w_mla_q_up, g_mla_kv_a, w_mla_kv_up, g_mla_q_nope,
              g_mla_q_rope, g_mla_k_nope, g_mla_k_rope):
    q_lora, kv_lora = g_mla_q_a.shape[1], g_mla_kv_a.shape[1]
    qk = MLA_NOPE + MLA_ROPE
    heads = w_mla_q_up.shape[2] // qk
    w_in = w_mla_in[j]
    k_r = w_in[:, q_lora + kv_lora:]
    w_in = jnp.concatenate([w_in, _swap_halves(k_r)], axis=-1)
    wq = w_mla_q_up[j].reshape(q_lora, heads, qk)
    rope = wq[:, :, MLA_NOPE:]
    wq = jnp.concatenate([wq[:, :, :MLA_NOPE].reshape(q_lora, heads // 2, 2 * MLA_NOPE),
                          jnp.concatenate([rope, _swap_halves(rope)], axis=-1).reshape(q_lora, heads // 2, 2 * LANES)],
                         axis=-1).reshape(q_lora, heads * 2 * LANES)
    wkv = w_mla_kv_up[j].reshape(kv_lora, heads, MLA_NOPE + MLA_V)
    dup = lambda g: jnp.concatenate([g, _swap_halves(g)], axis=-1)[None, :]
    two = lambda g: jnp.concatenate([g, g], axis=-1)
    gsum = np.kron(np.eye(2), np.ones((LANES, LANES)))
    return dict(
        heads=heads, gsum=jnp.asarray(gsum, BF16),
        g_attn=g_attn_norm_i[None, :],
        w_in=w_in.astype(BF16),
        g_q_a=g_mla_q_a[j][None, :], g_kv_a=g_mla_kv_a[j][None, :],
        g_k_rope=dup(g_mla_k_rope[j]), g_q_rope=two(dup(g_mla_q_rope[j])),
        g_q_nope=two(g_mla_q_nope[j][None, :]), g_k_nope=two(g_mla_k_nope[j][None, :]),
        w_q=wq.astype(BF16),
        w_kn=wkv[:, :, :MLA_NOPE].reshape(kv_lora, heads * MLA_NOPE).astype(BF16),
        w_v=wkv[:, :, MLA_NOPE:].reshape(kv_lora, heads * MLA_V).astype(BF16),
    )


def _prep_swa(j, g_attn_norm_i, w_swa_qkv, g_swa_q, g_swa_k, swa_sinks):
    q_heads = swa_sinks.shape[1]
    w = w_swa_qkv[j]
    d = w.shape[0]
    nq = q_heads * SWA_HEAD_DIM
    kv_heads = (w.shape[1] - nq) // (2 * SWA_HEAD_DIM)
    nk = kv_heads * SWA_HEAD_DIM
    wk = w[:, nq:nq + nk].reshape(d, kv_heads, SWA_HEAD_DIM)
    wv = w[:, nq + nk:].reshape(d, kv_heads, SWA_HEAD_DIM)
    wkv = jnp.concatenate([wk, wk, wv, wv], axis=-1).reshape(d, kv_heads * 4 * SWA_HEAD_DIM)
    gmat = np.kron(np.eye(2 * LANES // SWA_HEAD_DIM), np.ones((SWA_HEAD_DIM, SWA_HEAD_DIM)))
    dup = lambda g: jnp.concatenate([g, g], axis=-1)[None, :]
    return dict(
        q_heads=q_heads, kv_heads=kv_heads,
        g_attn=g_attn_norm_i[None, :],
        w_qkv=jnp.concatenate([w[:, :nq], wkv], axis=-1).astype(BF16),
        g_q=dup(g_swa_q[j]), g_k=dup(g_swa_k[j]),
        gmat=jnp.asarray(gmat, BF16),
        sinks=swa_sinks[j],
    )


def _kv_dup(t):
    b, l, kv, d = t.shape
    t = jnp.transpose(t, (2, 0, 1, 3)).reshape(kv, b * l, d)
    return jnp.concatenate([t, t], axis=-1).astype(BF16)


def _first_casts(shared, steps):
    raw = shared['raw']
    wg, wu, wd = raw['ffn']
    return ([_cast_item(wg, steps, 0), _cast_item(wu, steps, 0), _cast_item(wd, steps, 0, col_split=8),
             _cast_item(raw['ple_gate'], steps), _cast_item(raw['ple_proj'], steps, col_split=8),
             _cast_item(raw['mla_out'], steps), _cast_item(raw['swa_out'], steps)])


def _trunk(x, p, pos, caches, layers, shared, g_final, tm, tf):
    batch, seq, d = x.shape
    n = batch * seq
    h = x.reshape(n, d)
    rows = max(tm, seq)
    mla_tab = _tile_rows(_mla_table(pos), rows)
    swa_tabs = tuple(_tile_rows(t, rows) for t in _swa_tables(pos))
    depth = len(layers)
    n_mla = sum(kind == 'mla' for kind, _ in layers)
    p_rows = p.reshape(depth, n, -1)
    lat_kr, sks, svs = None, [], []
    for i, (kind, w) in enumerate(layers):
        j = i // 2
        if kind == 'mla':
            *lat_kr, q, k, v = _mla_proj(h, w, mla_tab, tm, j, n_mla, lat_kr)
            if caches is None:
                tq, hb = 512, 4
                first = 'ple' not in shared
                cast = _first_casts(shared, _mla_attn_steps(batch, w['heads'], seq, tq, hb)) if first else ()
                o, casted = _mla_attn(q, k, v, batch, seq, tq=tq, hb=hb, cast=cast)
                if first:
                    shared['ffn'] = {0: casted[:3]}
                    shared['ple'] = casted[3:5]
                    shared['out'] = dict(mla=casted[5], swa=casted[6])
            else:
                lat_c, kr_c = caches[0], caches[1]
                past = lat_c.shape[2]
                k_c, v_c = _mla_expand_call(lat_c.reshape(lat_c.shape[0], batch * past, -1),
                                            kr_c.reshape(kr_c.shape[0], batch * past, -1), j, w, tm=512)
                o = _mla_attn_sample(q, k_c, k, v_c, v, batch, past, seq, hb=w['heads'])
        else:
            q, kd, vd, kf, vf = _swa_proj(h, w, swa_tabs, tm)
            kv_heads = w['kv_heads']
            unhead = lambda t, rows: jnp.transpose(
                t.reshape(kv_heads, batch, seq, SWA_HEAD_DIM)[:, :, seq - rows:], (1, 2, 0, 3))
            if caches is None:
                o = _swa_attn(w['sinks'], q, kd, vd, batch, seq, tq=256)
                keep = min(WINDOW, seq)
                sks.append(unhead(kf, keep))
                svs.append(unhead(vf, keep))
            else:
                ck, cv = caches[2][j], caches[3][j]
                keep = ck.shape[1]
                o = _swa_attn_sample(w['sinks'], q, _kv_dup(ck), kd, _kv_dup(cv), vd, batch, keep, seq)
                sks.append(jnp.concatenate([ck, unhead(kf, seq)], axis=1)[:, -keep:])
                svs.append(jnp.concatenate([cv, unhead(vf, seq)], axis=1)[:, -keep:])
        h = _oproj(h, o, shared['out'][kind], j, tm)
        ffn_cast = shared['ffn']
        h = _ffn(h, shared['raw']['ffn_g'][i:i + 1], *ffn_cast[i], 0, min(n, 2 * tm), tf)
        cast = shared['raw']['ffn'] if (i + 1 not in ffn_cast and i + 1 < depth) else ()
        h, casted = _ple(h, shared['raw']['ple_g'], shared['ple'][0], p_rows, i, shared['ple'][1], g_final,
                         i == depth - 1, tm, cast, i + 1)
        if casted:
            ffn_cast[i + 1] = casted
    lat, kr = (t.reshape(n_mla, batch, seq, -1) for t in lat_kr)
    return h.reshape(batch, seq, d), lat, kr, jnp.stack(sks), jnp.stack(svs)


def kernel(x_prompt, x_sample, p_prompt, p_sample, cache_mla_latent, cache_mla_krope, state_swa_k, state_swa_v, g_attn_norm, w_mla_in, g_mla_q_a, w_mla_q_up, g_mla_kv_a, w_mla_kv_up, g_mla_q_nope, g_mla_q_rope, g_mla_k_nope, g_mla_k_rope, w_mla_out, w_swa_qkv, g_swa_q, g_swa_k, swa_sinks, w_swa_out, g_ffn_norm, w_ffn_gate, w_ffn_up, w_ffn_down, g_ple_norm, w_ple_gate, w_ple_proj, g_final):
    depth = g_attn_norm.shape[0]
    layers = []
    for i in range(depth):
        j = i // 2
        if i % 2 == 0:
            layers.append(('mla', _prep_mla(j, g_attn_norm[i], w_mla_in, g_mla_q_a, w_mla_q_up,
                                            g_mla_kv_a, w_mla_kv_up, g_mla_q_nope, g_mla_q_rope,
                                            g_mla_k_nope, g_mla_k_rope)))
        else:
            layers.append(('swa', _prep_swa(j, g_attn_norm[i], w_swa_qkv, g_swa_q, g_swa_k, swa_sinks)))
    shared = dict(raw=dict(ffn=(w_ffn_gate, w_ffn_up, w_ffn_down), ffn_g=g_ffn_norm[:, None, :],
                           ple_gate=w_ple_gate, ple_proj=w_ple_proj, ple_g=g_ple_norm[:, None, :],
                           mla_out=w_mla_out, swa_out=w_swa_out))
    gfin = g_final[None, :]

    seq = x_prompt.shape[1]
    t = x_sample.shape[1]
    past = cache_mla_latent.shape[2]
    pos_p = np.arange(seq)
    pos_s = past + np.arange(t)
    y_p, lat_p, kr_p, sk_p, sv_p = _trunk(x_prompt, p_prompt, pos_p, None, layers, shared,
                                          gfin, tm=512, tf=512)
    caches = (cache_mla_latent, cache_mla_krope, state_swa_k, state_swa_v)
    n_s = x_sample.shape[0] * t
    y_s, lat_s, kr_s, sk_s, sv_s = _trunk(x_sample, p_sample, pos_s, caches, layers, shared,
                                          gfin, tm=n_s, tf=512)
    return (y_p, y_s, lat_p, kr_p, sk_p, sv_p, lat_s, kr_s, sk_s, sv_s)
```

```python
import functools

import numpy as np
import jax
import jax.numpy as jnp
from jax import lax
from jax.experimental import pallas as pl
from jax.experimental.pallas import tpu as pltpu

F32 = jnp.float32
BF16 = jnp.bfloat16

EPS = 1e-6
CHUNK = 64
WINDOW = 128
MLA_THETA = 10000.0
SWA_THETA = 500000.0
MLA_NOPE = 128
MLA_ROPE = 64
MLA_V = 128
SWA_HEAD_DIM = 64
SWA_ROT = SWA_HEAD_DIM // 4

LANES = 128
VMEM_LIMIT = 56 * 1024 * 1024
NEG = -1e30
LOG2E = 1.4426950408889634


def _params(*sem):
    return pltpu.CompilerParams(dimension_semantics=sem, vmem_limit_bytes=VMEM_LIMIT)


def _resident(shape, layer=None):
    if layer is None:
        zeros = (0,) * len(shape)
        return pl.BlockSpec(shape, lambda *_: zeros, pipeline_mode=pl.Buffered(1))
    index = (layer,) + (0,) * (len(shape) - 1)
    return pl.BlockSpec((None,) + tuple(shape[1:]), lambda *_: index, pipeline_mode=pl.Buffered(1))


def _cast_item(a, steps, layer=None, col_split=1):
    nl, r, c = a.shape
    lead, first = (nl, 0) if layer is None else (1, layer)
    return dict(a=a, block=(lead, r * col_split // steps, c // col_split),
                in_idx=lambda s: (first, s // col_split, s % col_split),
                out_idx=lambda s: (0, s // col_split, s % col_split),
                out_shape=jax.ShapeDtypeStruct((lead, r, c), BF16))


def _cast_specs(items, step_of):
    ins = [pl.BlockSpec(it['block'], lambda *g, it=it: it['in_idx'](step_of(*g))) for it in items]
    outs = [pl.BlockSpec(it['block'], lambda *g, it=it: it['out_idx'](step_of(*g))) for it in items]
    return ins, outs, [it['out_shape'] for it in items]


def _cast_chunks(srcs, dsts):
    for src, dst in zip(srcs, dsts):
        dst[...] = src[...].astype(BF16)


def _rms(x, g):
    return x * lax.rsqrt(jnp.mean(x * x, axis=-1, keepdims=True) + EPS) * g


def _dot(a, b):
    return jnp.dot(a, b, preferred_element_type=F32)


def _dot_t(a, b):
    return lax.dot_general(a, b, (((1,), (1,)), ((), ())), preferred_element_type=F32)


def _rope_dup(x, g, tab):
    ss = jnp.sum(x * x, axis=-1, keepdims=True)
    y = x * lax.rsqrt(ss * (1.0 / LANES) + EPS) * g
    t = y * tab
    return t + pltpu.roll(t, LANES // 2, axis=1)


def _mla_heads(c_q, latb, krb, tab, wq_ref, wkn_ref, wv_ref, gqn, gqr, gkn, gsum_ref,
               q_ref, k_ref, v_ref, heads):
    gsum = gsum_ref[...]
    pair = 2 * LANES

    def inv_rms(x):
        return lax.rsqrt(_dot((x * x).astype(BF16), gsum) * (1.0 / LANES) + EPS)

    def finish(p, xq, xk, xv):
        if xq is None:
            kn = jnp.concatenate([_rms(xk[:, :LANES], gkn[:, :LANES]),
                                  _rms(xk[:, LANES:], gkn[:, LANES:])], axis=1)
        else:
            kn = xk * inv_rms(xk) * gkn
        if xq is not None:
            qn = xq[:, :pair] * inv_rms(xq[:, :pair]) * gqn
            t = xq[:, pair:] * inv_rms(xq[:, pair:]) * gqr * tab
        for u in range(2):
            h = 2 * p + u
            sl = slice(u * LANES, (u + 1) * LANES)
            k_ref[h, :, 0:MLA_NOPE] = kn[:, sl].astype(BF16)
            k_ref[h, :, MLA_NOPE:MLA_NOPE + MLA_ROPE] = krb
            v_ref[h] = xv[:, sl].astype(BF16)
            if xq is not None:
                q_ref[h, :, 0:MLA_NOPE] = qn[:, sl].astype(BF16)
                tu = t[:, sl]
                qr = tu + pltpu.roll(tu, LANES // 2, axis=1)
                q_ref[h, :, MLA_NOPE:MLA_NOPE + MLA_ROPE] = qr[:, :MLA_ROPE].astype(BF16)

    pending = None
    for p in range(heads // 2):
        xq = None if c_q is None else _dot(c_q, wq_ref[:, p * 2 * pair:(p + 1) * 2 * pair])
        xk = _dot(latb, wkn_ref[:, p * pair:(p + 1) * pair])
        xv = _dot(latb, wv_ref[:, p * pair:(p + 1) * pair])
        if pending is not None:
            finish(*pending)
        pending = (p, xq, xk, xv)
    finish(*pending)


def _mla_proj_kernel(h_ref, gattn_ref, win_ref, gqa_ref, gkva_ref, gkr_ref, tab_ref, wq_ref,
                     gqn_ref, gqr_ref, wkn_ref, wv_ref, gkn_ref, gsum_ref,
                     lat_ref, kr_ref, q_ref, k_ref, v_ref, *, heads, q_lora, kv_lora, scale):
    hn = _rms(h_ref[...], gattn_ref[...]).astype(BF16)
    a = _dot(hn, win_ref[...])
    c_q = _rms(a[:, :q_lora], gqa_ref[...]).astype(BF16)
    lat = _rms(a[:, q_lora:q_lora + kv_lora], gkva_ref[...])
    lat_ref[...] = lat
    tab = tab_ref[...]
    kr = _rope_dup(a[:, q_lora + kv_lora:], gkr_ref[...], tab)
    kr_ref[...] = kr[:, :MLA_ROPE]
    _mla_heads(c_q, lat.astype(BF16), kr[:, :MLA_ROPE].astype(BF16), jnp.concatenate([tab, tab], axis=1),
               wq_ref, wkn_ref, wv_ref, gqn_ref[...] * scale, gqr_ref[...] * scale, gkn_ref[...],
               gsum_ref, q_ref, k_ref, v_ref, heads)


def _mla_expand_kernel(lat_ref, kr_ref, wkn_ref, wv_ref, gkn_ref, gsum_ref, k_ref, v_ref, *, heads):
    _mla_heads(None, lat_ref[...].astype(BF16), kr_ref[...].astype(BF16), None, None, wkn_ref, wv_ref,
               None, None, gkn_ref[...], gsum_ref, None, k_ref, v_ref, heads)


def _mla_proj(h, w, tab, tm):
    n, d = h.shape
    heads = w['heads']
    q_lora, kv_lora = w['g_q_a'].shape[1], w['g_kv_a'].shape[1]
    qk = MLA_NOPE + MLA_ROPE
    nt = tab.shape[0] // tm
    row = lambda i: (i, 0)
    hrow = lambda i: (0, i, 0)
    kern = functools.partial(_mla_proj_kernel, heads=heads, q_lora=q_lora, kv_lora=kv_lora,
                             scale=qk ** -0.5 * LOG2E)
    weights = (w['g_attn'], w['w_in'], w['g_q_a'], w['g_kv_a'], w['g_k_rope'], tab, w['w_q'],
               w['g_q_nope'], w['g_q_rope'], w['w_kn'], w['w_v'], w['g_k_nope'], w['gsum'])
    return pl.pallas_call(
        kern,
        grid=(n // tm,),
        in_specs=[
            pl.BlockSpec((tm, d), row),
            _resident(w['g_attn'].shape), _resident(w['w_in'].shape),
            _resident(w['g_q_a'].shape), _resident(w['g_kv_a'].shape), _resident(w['g_k_rope'].shape),
            pl.BlockSpec((tm, LANES), lambda i: (i % nt, 0)),
            _resident(w['w_q'].shape), _resident(w['g_q_nope'].shape), _resident(w['g_q_rope'].shape),
            _resident(w['w_kn'].shape), _resident(w['w_v'].shape), _resident(w['g_k_nope'].shape),
            _resident(w['gsum'].shape),
        ],
        out_specs=[
            pl.BlockSpec((tm, kv_lora), row),
            pl.BlockSpec((tm, MLA_ROPE), row),
            pl.BlockSpec((heads, tm, qk), hrow),
            pl.BlockSpec((heads, tm, qk), hrow),
            pl.BlockSpec((heads, tm, MLA_V), hrow),
        ],
        out_shape=[
            jax.ShapeDtypeStruct((n, kv_lora), F32),
            jax.ShapeDtypeStruct((n, MLA_ROPE), F32),
            jax.ShapeDtypeStruct((heads, n, qk), BF16),
            jax.ShapeDtypeStruct((heads, n, qk), BF16),
            jax.ShapeDtypeStruct((heads, n, MLA_V), BF16),
        ],
        compiler_params=_params("parallel"),
        name="mla_proj",
    )(h, *weights)


def _mla_expand_call(lat, kr, layer, w, tm):
    _, n, kv_lora = lat.shape
    heads = w['heads']
    qk = MLA_NOPE + MLA_ROPE
    hrow = lambda i: (0, i, 0)
    return pl.pallas_call(
        functools.partial(_mla_expand_kernel, heads=heads),
        grid=(n // tm,),
        in_specs=[pl.BlockSpec((None, tm, kv_lora), lambda i: (layer, i, 0)),
                  pl.BlockSpec((None, tm, MLA_ROPE), lambda i: (layer, i, 0)),
                  _resident(w['w_kn'].shape), _resident(w['w_v'].shape),
                  _resident(w['g_k_nope'].shape), _resident(w['gsum'].shape)],
        out_specs=[pl.BlockSpec((heads, tm, qk), hrow), pl.BlockSpec((heads, tm, MLA_V), hrow)],
        out_shape=[jax.ShapeDtypeStruct((heads, n, qk), BF16),
                   jax.ShapeDtypeStruct((heads, n, MLA_V), BF16)],
        compiler_params=_params("parallel"),
        name="mla_expand",
    )(lat, kr, w['w_kn'], w['w_v'], w['g_k_nope'], w['gsum'])


def _mla_attn_kernel(q_ref, k_ref, v_ref, *rest, hb, tq, n_cast):
    cast_in, o_ref, cast_out, s_ref = rest[:n_cast], rest[n_cast], rest[n_cast + 1:-1], rest[-1]
    _cast_chunks(cast_in, cast_out)
    i = pl.program_id(2)
    kc = lax.broadcasted_iota(jnp.int32, (tq, tq), 0) // CHUNK
    qc = lax.broadcasted_iota(jnp.int32, (tq, tq), 1) // CHUNK
    diag_bias = jnp.where(kc <= qc, 0.0, NEG).astype(F32)
    qs = [q_ref[h] for h in range(hb)]

    def scores(j, slot):
        start = pl.multiple_of(j * tq, tq)
        mx = []
        for h in range(hb):
            s = _dot_t(k_ref[h, pl.ds(start, tq), :], qs[h])
            s_ref[slot, h] = s
            mx.append(jnp.max(s, axis=0, keepdims=True))
        return tuple(mx)

    def update(j, slot, carry, mx, bias=None):
        start = pl.multiple_of(j * tq, tq)
        ps, stats = [], []
        for h in range(hb):
            m, l, _ = carry[h]
            if bias is None:
                s, blk_max = s_ref[slot, h], mx[h]
            else:
                s = s_ref[slot, h] + bias
                blk_max = jnp.max(s, axis=0, keepdims=True)
            m_new = jnp.maximum(m, blk_max)
            alpha = jnp.exp2(m - m_new)
            p = jnp.exp2(s - m_new)
            stats.append((m_new, alpha * l + jnp.sum(p, axis=0, keepdims=True), alpha))
            ps.append(p.astype(BF16))
        out = []
        for h in range(hb):
            pv = lax.dot_general(v_ref[h, pl.ds(start, tq), :], ps[h],
                                 (((0,), (0,)), ((), ())), preferred_element_type=F32)
            m_new, l, alpha = stats[h]
            out.append((m_new, l, alpha * carry[h][2] + pv))
        return tuple(out)

    def finish(carry):
        for h in range(hb):
            _, l, acc = carry[h]
            o_ref[:, h * MLA_V:(h + 1) * MLA_V] = (acc / l).T.astype(BF16)

    def pair(t, state):
        carry, mx0 = state
        mx1 = scores(2 * t + 1, 1)
        carry = update(2 * t, 0, carry, mx0)
        mx0 = scores(2 * t + 2, 0)
        return update(2 * t + 1, 1, carry, mx1), mx0

    init = tuple((jnp.full((1, tq), NEG, F32), jnp.zeros((1, tq), F32), jnp.zeros((MLA_V, tq), F32))
                 for _ in range(hb))
    carry, mx0 = lax.fori_loop(0, i // 2, pair, (init, scores(0, 0)))

    @pl.when(i % 2 == 0)
    def _():
        finish(update(i, 0, carry, None, diag_bias))

    @pl.when(i % 2 == 1)
    def _():
        scores(i, 1)
        finish(update(i, 1, update(i - 1, 0, carry, mx0), None, diag_bias))


def _mla_attn_steps(batch, heads, seq, tq, hb):
    return batch * (heads // hb) * (seq // tq)


def _mla_attn(q, k, v, batch, seq, tq, hb, cast=()):
    heads, n, qk = q.shape
    nq = seq // tq
    ng = heads // hb
    kv_map = lambda b, g, i: (g, b, 0)
    cast_in, cast_out, cast_shapes = _cast_specs(cast, lambda b, g, i: (b * ng + g) * nq + i)
    out = pl.pallas_call(
        functools.partial(_mla_attn_kernel, hb=hb, tq=tq, n_cast=len(cast)),
        grid=(batch, ng, nq),
        in_specs=[pl.BlockSpec((hb, tq, qk), lambda b, g, i: (g, b * nq + i, 0)),
                  pl.BlockSpec((hb, seq, qk), kv_map),
                  pl.BlockSpec((hb, seq, MLA_V), kv_map)] + cast_in,
        out_specs=[pl.BlockSpec((tq, hb * MLA_V), lambda b, g, i: (b * nq + i, g))] + cast_out,
        out_shape=[jax.ShapeDtypeStruct((n, heads * MLA_V), BF16)] + cast_shapes,
        scratch_shapes=[pltpu.VMEM((2, hb, tq, tq), F32)],
        compiler_params=_params("parallel", "parallel", "arbitrary"),
        name="mla_attn",
    )(q, k, v, *[it['a'] for it in cast])
    return out[0], tuple(out[1:])


def _mla_attn_sample_kernel(q_ref, kc_ref, kn_ref, vc_ref, vn_ref, o_ref, *, hb):
    ss = [(_dot_t(kc_ref[h], q_ref[h]), _dot_t(kn_ref[h], q_ref[h])) for h in range(hb)]
    ps = []
    for s_c, s_n in ss:
        m = jnp.maximum(jnp.max(s_c, axis=0, keepdims=True), jnp.max(s_n, axis=0, keepdims=True))
        p_c = jnp.exp2(s_c - m)
        p_n = jnp.exp2(s_n - m)
        l = jnp.sum(p_c, axis=0, keepdims=True) + jnp.sum(p_n, axis=0, keepdims=True)
        ps.append((p_c.astype(BF16), p_n.astype(BF16), l))
    tn = (((0,), (0,)), ((), ()))
    for h, (p_c, p_n, l) in enumerate(ps):
        acc = (lax.dot_general(vc_ref[h], p_c, tn, preferred_element_type=F32)
               + lax.dot_general(vn_ref[h], p_n, tn, preferred_element_type=F32))
        o_ref[:, h * MLA_V:(h + 1) * MLA_V] = (acc / l).T.astype(BF16)


def _mla_attn_sample(q, k_cache, k_new, v_cache, v_new, batch, past, t, hb):
    heads, n, qk = q.shape
    q_pos = past + np.arange(t)
    k_pos = np.arange(past + t)
    assert np.all((k_pos // CHUNK)[None, :] <= (q_pos // CHUNK)[:, None])
    m3 = lambda b, g: (g, b, 0)
    return pl.pallas_call(
        functools.partial(_mla_attn_sample_kernel, hb=hb),
        grid=(batch, heads // hb),
        in_specs=[pl.BlockSpec((hb, t, qk), m3), pl.BlockSpec((hb, past, qk), m3),
                  pl.BlockSpec((hb, t, qk), m3), pl.BlockSpec((hb, past, MLA_V), m3),
                  pl.BlockSpec((hb, t, MLA_V), m3)],
        out_specs=pl.BlockSpec((t, hb * MLA_V), lambda b, g: (b, g)),
        out_shape=jax.ShapeDtypeStruct((n, heads * MLA_V), BF16),
        compiler_params=_params("parallel", "parallel"),
        name="mla_attn_sample",
    )(q, k_cache, k_new, v_cache, v_new)


def _swa_proj_kernel(h_ref, gattn_ref, w_ref, gq_ref, gk_ref, tc_ref, ts1_ref, ts2_ref, gmat_ref,
                     q_ref, kd_ref, vd_ref, kf_ref, vf_ref, *, q_cols, kv_heads, scale):
    hn = _rms(h_ref[...], gattn_ref[...]).astype(BF16)
    tc, ts1, ts2 = tc_ref[...], ts1_ref[...], ts2_ref[...]
    gmat = gmat_ref[...]

    def rope(x, ss, g):
        y = x * lax.rsqrt(ss * (1.0 / SWA_HEAD_DIM) + EPS) * g
        return (y * tc + pltpu.roll(y, LANES - SWA_ROT // 2, axis=1) * ts1
                + pltpu.roll(y, SWA_ROT // 2, axis=1) * ts2)

    gq = gq_ref[...] * scale
    gk = gk_ref[...]
    n_q = q_cols // 2

    def finish(c, x2):
        ss = _dot((x2 * x2).astype(BF16), gmat)
        if c < n_q:
            for u in range(2):
                col = 2 * c + u
                sl = slice(u * LANES, (u + 1) * LANES)
                q_ref[:, col * LANES:(col + 1) * LANES] = rope(x2[:, sl], ss[:, sl], gq).astype(BF16)
        else:
            kh = c - n_q
            kd = rope(x2[:, :LANES], ss[:, :LANES], gk)
            kd_ref[kh] = kd.astype(BF16)
            kf_ref[kh] = kd[:, :SWA_HEAD_DIM]
            vd_ref[kh] = x2[:, LANES:].astype(BF16)
            vf_ref[kh] = x2[:, LANES:LANES + SWA_HEAD_DIM]

    pending = None
    for c in range(n_q + kv_heads):
        x2 = _dot(hn, w_ref[:, c * 256:(c + 1) * 256])
        if pending is not None:
            finish(*pending)
        pending = (c, x2)
    finish(*pending)


def _swa_proj(h, w, tabs, tm):
    n, d = h.shape
    q_heads, kv_heads = w['q_heads'], w['kv_heads']
    q_cols = q_heads * SWA_HEAD_DIM // LANES
    nt = tabs[0].shape[0] // tm
    row = lambda i: (i, 0)
    hrow = lambda i: (0, i, 0)
    tspec = pl.BlockSpec((tm, LANES), lambda i: (i % nt, 0))
    kern = functools.partial(_swa_proj_kernel, q_cols=q_cols, kv_heads=kv_heads,
                             scale=SWA_HEAD_DIM ** -0.5 * LOG2E)
    return pl.pallas_call(
        kern,
        grid=(n // tm,),
        in_specs=[pl.BlockSpec((tm, d), row), _resident(w['g_attn'].shape),
                  _resident(w['w_qkv'].shape), _resident(w['g_q'].shape), _resident(w['g_k'].shape),
                  tspec, tspec, tspec, _resident(w['gmat'].shape)],
        out_specs=[pl.BlockSpec((tm, q_cols * LANES), row),
                   pl.BlockSpec((kv_heads, tm, LANES), hrow),
                   pl.BlockSpec((kv_heads, tm, LANES), hrow),
                   pl.BlockSpec((kv_heads, tm, SWA_HEAD_DIM), hrow),
                   pl.BlockSpec((kv_heads, tm, SWA_HEAD_DIM), hrow)],
        out_shape=[jax.ShapeDtypeStruct((n, q_cols * LANES), BF16),
                   jax.ShapeDtypeStruct((kv_heads, n, LANES), BF16),
                   jax.ShapeDtypeStruct((kv_heads, n, LANES), BF16),
                   jax.ShapeDtypeStruct((kv_heads, n, SWA_HEAD_DIM), F32),
                   jax.ShapeDtypeStruct((kv_heads, n, SWA_HEAD_DIM), F32)],
        compiler_params=_params("parallel"),
        name="swa_proj",
    )(h, w['g_attn'], w['w_qkv'], w['g_q'], w['g_k'], *tabs, w['gmat'])


def _swa_attn_kernel(sink_ref, q_ref, kp_ref, kc_ref, vp_ref, vc_ref, o_ref, *,
                     kv_heads, group, banded):
    tq = q_ref.shape[0]
    n_p, n_c = kp_ref.shape[1], kc_ref.shape[1]
    n_k = n_p + n_c
    lane = lax.broadcasted_iota(jnp.int32, (1, LANES), 1)
    row = lax.broadcasted_iota(jnp.int32, (LANES, 1), 0)
    if banded:
        assert n_p == WINDOW == LANES
        tiles = [(t * LANES, LANES, t * LANES, 2 * LANES) for t in range(tq // LANES)]
    else:
        tiles = [(0, tq, 0, n_k)]
    biases = []
    for q0, qn, k0, kn in tiles:
        if not banded:
            biases.append(None)
            continue
        win = WINDOW // CHUNK
        kc = (lax.broadcasted_iota(jnp.int32, (kn, qn), 0) + k0) // CHUNK - n_p // CHUNK
        qc = (lax.broadcasted_iota(jnp.int32, (kn, qn), 1) + q0) // CHUNK
        ok = (kc <= qc) & (kc >= qc - win) & ((kc >= 0) | (pl.program_id(1) > 0))
        biases.append(jnp.where(ok, 0.0, NEG).astype(F32))
    zero = jnp.zeros((), BF16)
    half = group // 2
    for kh in range(kv_heads):
        k_all = jnp.concatenate([kp_ref[kh], kc_ref[kh]], axis=0)
        v_t = jnp.concatenate([vp_ref[kh], vc_ref[kh]], axis=0).T
        ks = (jnp.where(lane < SWA_HEAD_DIM, k_all, zero), jnp.where(lane < SWA_HEAD_DIM, zero, k_all))
        vs = (jnp.where(row < SWA_HEAD_DIM, v_t, zero), jnp.where(row < SWA_HEAD_DIM, zero, v_t))
        cols = [q_ref[:, (kh * half + c) * LANES:(kh * half + c + 1) * LANES] for c in range(half)]
        for (q0, qn, k0, kn), bias in zip(tiles, biases):
            ss = [_dot_t(ks[u][k0:k0 + kn], cols[c][q0:q0 + qn])
                  for c in range(half) for u in range(2)]
            es, rdens = [], []
            for c in range(half):
                for u in range(2):
                    s = ss[2 * c + u]
                    if bias is not None:
                        s = s + bias
                    sink = sink_ref[2 * (kh * half + c) + u] * LOG2E
                    m = jnp.maximum(jnp.max(s, axis=0, keepdims=True), sink)
                    e = jnp.exp2(s - m)
                    rdens.append(1.0 / (jnp.sum(e, axis=0, keepdims=True) + jnp.exp2(sink - m)))
                    es.append(e.astype(BF16))
            for c in range(half):
                o_t = (_dot(vs[0][:, k0:k0 + kn], es[2 * c]) * rdens[2 * c]
                       + _dot(vs[1][:, k0:k0 + kn], es[2 * c + 1]) * rdens[2 * c + 1])
                col = kh * half + c
                o_ref[q0:q0 + qn, col * LANES:(col + 1) * LANES] = o_t.T.astype(BF16)


def _swa_attn(sinks, q, kd, vd, batch, seq, tq):
    n, dq = q.shape
    kv_heads = kd.shape[0]
    group = (dq // SWA_HEAD_DIM) // kv_heads
    nq = seq // tq
    per = tq // WINDOW
    prev = lambda b, i: (0, jnp.maximum(i * per - 1, 0) + b * nq * per, 0)
    cur = lambda b, i: (0, b * nq + i, 0)
    return pl.pallas_call(
        functools.partial(_swa_attn_kernel, kv_heads=kv_heads, group=group, banded=True),
        grid=(batch, nq),
        in_specs=[pl.BlockSpec(memory_space=pltpu.SMEM),
                  pl.BlockSpec((tq, dq), lambda b, i: (b * nq + i, 0)),
                  pl.BlockSpec((kv_heads, WINDOW, LANES), prev),
                  pl.BlockSpec((kv_heads, tq, LANES), cur),
                  pl.BlockSpec((kv_heads, WINDOW, LANES), prev),
                  pl.BlockSpec((kv_heads, tq, LANES), cur)],
        out_specs=pl.BlockSpec((tq, dq), lambda b, i: (b * nq + i, 0)),
        out_shape=jax.ShapeDtypeStruct((n, dq), BF16),
        compiler_params=_params("parallel", "arbitrary"),
        name="swa_attn",
    )(sinks, q, kd, kd, vd, vd)


def _swa_attn_sample(sinks, q, kd_cache, kd_new, vd_cache, vd_new, batch, keep, t):
    n, dq = q.shape
    kv_heads = kd_new.shape[0]
    group = (dq // SWA_HEAD_DIM) // kv_heads
    blk = lambda b: (0, b, 0)
    return pl.pallas_call(
        functools.partial(_swa_attn_kernel, kv_heads=kv_heads, group=group, banded=False),
        grid=(batch,),
        in_specs=[pl.BlockSpec(memory_space=pltpu.SMEM),
                  pl.BlockSpec((t, dq), lambda b: (b, 0)),
                  pl.BlockSpec((kv_heads, keep, LANES), blk),
                  pl.BlockSpec((kv_heads, t, LANES), blk),
                  pl.BlockSpec((kv_heads, keep, LANES), blk),
                  pl.BlockSpec((kv_heads, t, LANES), blk)],
        out_specs=pl.BlockSpec((t, dq), lambda b: (b, 0)),
        out_shape=jax.ShapeDtypeStruct((n, dq), BF16),
        compiler_params=_params("parallel"),
        name="swa_attn_sample",
    )(sinks, q, kd_cache, kd_new, vd_cache, vd_new)


def _oproj_kernel(h_ref, o_ref, w_ref, out_ref):
    out_ref[...] = h_ref[...] + _dot(o_ref[...], w_ref[...])


def _oproj(h, o, w, layer, tm):
    n, d = h.shape
    row = lambda i: (i, 0)
    return pl.pallas_call(
        _oproj_kernel,
        grid=(n // tm,),
        in_specs=[pl.BlockSpec((tm, d), row), pl.BlockSpec((tm, o.shape[1]), row),
                  _resident(w.shape, layer)],
        out_specs=pl.BlockSpec((tm, d), row),
        out_shape=jax.ShapeDtypeStruct((n, d), F32),
        compiler_params=_params("parallel"),
        name="attn_out_proj",
    )(h, o, w)


def _ffn_kernel(h_ref, g_ref, wg_ref, wu_ref, wd_ref, out_ref, hn_ref):
    @pl.when(pl.program_id(1) == 0)
    def _():
        x = h_ref[...]
        hn_ref[...] = _rms(x, g_ref[...]).astype(BF16)
        out_ref[...] = x

    hn = hn_ref[...]
    w = wg_ref.shape[1] // 2
    acts = []
    for c in range(2):
        a = _dot(hn, wg_ref[:, c * w:(c + 1) * w])
        b = _dot(hn, wu_ref[:, c * w:(c + 1) * w])
        acts.append((a * jax.nn.sigmoid(a) * b).astype(BF16))
    out_ref[...] += _dot(acts[0], wd_ref[0:w, :]) + _dot(acts[1], wd_ref[w:2 * w, :])


def _ffn(h, g, wg, wu, wd, layer, tm, tf):
    n, d = h.shape
    f = wg.shape[2]
    row = lambda i, j: (i, 0)
    return pl.pallas_call(
        _ffn_kernel,
        grid=(n // tm, f // tf),
        in_specs=[pl.BlockSpec((tm, d), row), pl.BlockSpec((None, 1, d), lambda i, j: (layer, 0, 0)),
                  pl.BlockSpec((None, d, tf), lambda i, j: (layer, 0, j)),
                  pl.BlockSpec((None, d, tf), lambda i, j: (layer, 0, j)),
                  pl.BlockSpec((None, tf, d), lambda i, j: (layer, j, 0))],
        out_specs=pl.BlockSpec((tm, d), row),
        out_shape=jax.ShapeDtypeStruct((n, d), F32),
        scratch_shapes=[pltpu.VMEM((tm, d), BF16)],
        compiler_params=_params("parallel", "arbitrary"),
        name="swiglu_ffn",
    )(h, g, wg, wu, wd)


def _ple_kernel(h_ref, g_ref, wgate_ref, p_ref, wproj_ref, gfin_ref, *rest, final, n_cast, stack):
    n_rows = sum(stack)
    cast_in, rows_in = rest[:n_cast], rest[n_cast:n_cast + n_rows]
    out_ref = rest[n_cast + n_rows]
    cast_out = rest[n_cast + n_rows + 1:2 * n_cast + n_rows + 1]
    stacked = rest[2 * n_cast + n_rows + 1:]
    first = 0
    for dst, count in zip(stacked, stack):
        for s in range(count):
            dst[s] = rows_in[first + s][...]
        first += count
    x = h_ref[...]
    gate = jax.nn.sigmoid(_dot(_rms(x, g_ref[...]).astype(BF16), wgate_ref[...]))
    y = x + gate * _dot(p_ref[...].astype(BF16), wproj_ref[...])
    if final:
        y = _rms(y, gfin_ref[...])
    out_ref[...] = y
    _cast_chunks(cast_in, cast_out)


def _ple(h, g, wgate, p, layer, wproj, gfin, final, tm, cast=(), cast_layer=0, stack=()):
    n, d = h.shape
    steps = n // tm
    row = lambda i: (i, 0)
    items = [_cast_item(a, steps, cast_layer) for a in cast]
    cast_specs, cast_out_specs, cast_shapes = _cast_specs(items, lambda i: i)
    rows = [a for group in stack for a in group]
    row_specs = [pl.BlockSpec((tm, a.shape[1]), row) for a in rows]
    stack_specs = [pl.BlockSpec((len(gr), tm, gr[0].shape[1]), lambda i: (0, i, 0)) for gr in stack]
    stack_shapes = [jax.ShapeDtypeStruct((len(gr), n, gr[0].shape[1]), gr[0].dtype) for gr in stack]
    out = pl.pallas_call(
        functools.partial(_ple_kernel, final=final, n_cast=len(cast), stack=tuple(len(gr) for gr in stack)),
        grid=(steps,),
        in_specs=[pl.BlockSpec((tm, d), row), _resident(g.shape, layer), _resident(wgate.shape, layer),
                  pl.BlockSpec((None, tm, p.shape[2]), lambda i: (layer, i, 0)),
                  _resident(wproj.shape, layer), _resident(gfin.shape)] + cast_specs + row_specs,
        out_specs=[pl.BlockSpec((tm, d), row)] + cast_out_specs + stack_specs,
        out_shape=[jax.ShapeDtypeStruct((n, d), F32)] + cast_shapes + stack_shapes,
        compiler_params=_params("parallel"),
        name="ple_embed",
    )(h, g, wgate, p, wproj, gfin, *cast, *rows)
    n_c = len(cast)
    return out[0], tuple(out[1:1 + n_c]), tuple(out[1 + n_c:])


def _angles(pos, r, theta):
    inv = np.power(np.float64(theta), -np.arange(0, r, 2, dtype=np.float64) / r)
    ang = pos.astype(np.float64)[:, None] * inv[None, :]
    return np.cos(ang).astype(np.float32), np.sin(ang).astype(np.float32)


def _mla_table(pos):
    cos, sin = _angles(pos, MLA_ROPE, MLA_THETA)
    return np.concatenate([cos, cos, -sin, sin], axis=-1)


def _swa_tables(pos):
    cos, sin = _angles(pos, SWA_ROT, SWA_THETA)
    s = cos.shape[0]
    rest = SWA_HEAD_DIM - SWA_ROT
    one = np.ones((s, rest), np.float32)
    zero = np.zeros((s, rest), np.float32)
    zh = np.zeros_like(sin)
    tc = np.concatenate([cos, cos, one], axis=-1)
    ts1 = np.concatenate([-sin, zh, zero], axis=-1)
    ts2 = np.concatenate([zh, sin, zero], axis=-1)
    return tuple(np.concatenate([t, t], axis=-1) for t in (tc, ts1, ts2))


def _tile_rows(tab, rows):
    tab = tab if tab.shape[0] >= rows else np.tile(tab, (rows // tab.shape[0], 1))
    return jnp.asarray(tab)


def _swap_halves(x):
    half = x.shape[-1] // 2
    return jnp.concatenate([x[..., half:], x[..., :half]], axis=-1)


def _prep_mla(j, g_attn_norm_i, w_mla_in, g_mla_q_a, w_mla_q_up, g_mla_kv_a, w_mla_kv_up, g_mla_q_nope,
              g_mla_q_rope, g_mla_k_nope, g_mla_k_rope):
    q_lora, kv_lora = g_mla_q_a.shape[1], g_mla_kv_a.shape[1]
    qk = MLA_NOPE + MLA_ROPE
    heads = w_mla_q_up.shape[2] // qk
    w_in = w_mla_in[j]
    k_r = w_in[:, q_lora + kv_lora:]
    w_in = jnp.concatenate([w_in, _swap_halves(k_r)], axis=-1)
    wq = w_mla_q_up[j].reshape(q_lora, heads, qk)
    rope = wq[:, :, MLA_NOPE:]
    wq = jnp.concatenate([wq[:, :, :MLA_NOPE].reshape(q_lora, heads // 2, 2 * MLA_NOPE),
                          jnp.concatenate([rope, _swap_halves(rope)], axis=-1).reshape(q_lora, heads // 2, 2 * LANES)],
                         axis=-1).reshape(q_lora, heads * 2 * LANES)
    wkv = w_mla_kv_up[j].reshape(kv_lora, heads, MLA_NOPE + MLA_V)
    dup = lambda g: jnp.concatenate([g, _swap_halves(g)], axis=-1)[None, :]
    two = lambda g: jnp.concatenate([g, g], axis=-1)
    gsum = np.kron(np.eye(2), np.ones((LANES, LANES)))
    return dict(
        heads=heads, gsum=jnp.asarray(gsum, BF16),
        g_attn=g_attn_norm_i[None, :],
        w_in=w_in.astype(BF16),
        g_q_a=g_mla_q_a[j][None, :], g_kv_a=g_mla_kv_a[j][None, :],
        g_k_rope=dup(g_mla_k_rope[j]), g_q_rope=two(dup(g_mla_q_rope[j])),
        g_q_nope=two(g_mla_q_nope[j][None, :]), g_k_nope=two(g_mla_k_nope[j][None, :]),
        w_q=wq.astype(BF16),
        w_kn=wkv[:, :, :MLA_NOPE].reshape(kv_lora, heads * MLA_NOPE).astype(BF16),
        w_v=wkv[:, :, MLA_NOPE:].reshape(kv_lora, heads * MLA_V).astype(BF16),
    )


def _prep_swa(j, g_attn_norm_i, w_swa_qkv, g_swa_q, g_swa_k, swa_sinks):
    q_heads = swa_sinks.shape[1]
    w = w_swa_qkv[j]
    d = w.shape[0]
    nq = q_heads * SWA_HEAD_DIM
    kv_heads = (w.shape[1] - nq) // (2 * SWA_HEAD_DIM)
    nk = kv_heads * SWA_HEAD_DIM
    wk = w[:, nq:nq + nk].reshape(d, kv_heads, SWA_HEAD_DIM)
    wv = w[:, nq + nk:].reshape(d, kv_heads, SWA_HEAD_DIM)
    wkv = jnp.concatenate([wk, wk, wv, wv], axis=-1).reshape(d, kv_heads * 4 * SWA_HEAD_DIM)
    gmat = np.kron(np.eye(2 * LANES // SWA_HEAD_DIM), np.ones((SWA_HEAD_DIM, SWA_HEAD_DIM)))
    dup = lambda g: jnp.concatenate([g, g], axis=-1)[None, :]
    return dict(
        q_heads=q_heads, kv_heads=kv_heads,
        g_attn=g_attn_norm_i[None, :],
        w_qkv=jnp.concatenate([w[:, :nq], wkv], axis=-1).astype(BF16),
        g_q=dup(g_swa_q[j]), g_k=dup(g_swa_k[j]),
        gmat=jnp.asarray(gmat, BF16),
        sinks=swa_sinks[j],
    )


def _kv_dup(t):
    b, l, kv, d = t.shape
    t = jnp.transpose(t, (2, 0, 1, 3)).reshape(kv, b * l, d)
    return jnp.concatenate([t, t], axis=-1).astype(BF16)


def _first_casts(shared, steps):
    raw = shared['raw']
    wg, wu, wd = raw['ffn']
    return ([_cast_item(wg, steps, 0), _cast_item(wu, steps, 0), _cast_item(wd, steps, 0, col_split=8),
             _cast_item(raw['ple_gate'], steps), _cast_item(raw['ple_proj'], steps, col_split=8),
             _cast_item(raw['mla_out'], steps), _cast_item(raw['swa_out'], steps)])


def _trunk(x, p, pos, caches, layers, shared, g_final, tm, tf):
    batch, seq, d = x.shape
    n = batch * seq
    h = x.reshape(n, d)
    rows = max(tm, seq)
    mla_tab = _tile_rows(_mla_table(pos), rows)
    swa_tabs = tuple(_tile_rows(t, rows) for t in _swa_tables(pos))
    depth = len(layers)
    n_mla = sum(kind == 'mla' for kind, _ in layers)
    p_rows = p.reshape(depth, n, -1)
    lats, krs, sks, svs = [], [], [], []
    for i, (kind, w) in enumerate(layers):
        j = i // 2
        if kind == 'mla':
            lat, kr, q, k, v = _mla_proj(h, w, mla_tab, tm)
            lats.append(lat)
            krs.append(kr)
            if caches is None:
                tq, hb = 512, 4
                first = 'ple' not in shared
                cast = _first_casts(shared, _mla_attn_steps(batch, w['heads'], seq, tq, hb)) if first else ()
                o, casted = _mla_attn(q, k, v, batch, seq, tq=tq, hb=hb, cast=cast)
                if first:
                    shared['ffn'] = {0: casted[:3]}
                    shared['ple'] = casted[3:5]
                    shared['out'] = dict(mla=casted[5], swa=casted[6])
            else:
                lat_c, kr_c = caches[0], caches[1]
                past = lat_c.shape[2]
                k_c, v_c = _mla_expand_call(lat_c.reshape(lat_c.shape[0], batch * past, -1),
                                            kr_c.reshape(kr_c.shape[0], batch * past, -1), j, w, tm=512)
                o = _mla_attn_sample(q, k_c, k, v_c, v, batch, past, seq, hb=w['heads'])
        else:
            q, kd, vd, kf, vf = _swa_proj(h, w, swa_tabs, tm)
            kv_heads = w['kv_heads']
            unhead = lambda t, rows: jnp.transpose(
                t.reshape(kv_heads, batch, seq, SWA_HEAD_DIM)[:, :, seq - rows:], (1, 2, 0, 3))
            if caches is None:
                o = _swa_attn(w['sinks'], q, kd, vd, batch, seq, tq=256)
                keep = min(WINDOW, seq)
                sks.append(unhead(kf, keep))
                svs.append(unhead(vf, keep))
            else:
                ck, cv = caches[2][j], caches[3][j]
                keep = ck.shape[1]
                o = _swa_attn_sample(w['sinks'], q, _kv_dup(ck), kd, _kv_dup(cv), vd, batch, keep, seq)
                sks.append(jnp.concatenate([ck, unhead(kf, seq)], axis=1)[:, -keep:])
                svs.append(jnp.concatenate([cv, unhead(vf, seq)], axis=1)[:, -keep:])
        h = _oproj(h, o, shared['out'][kind], j, tm)
        ffn_cast = shared['ffn']
        h = _ffn(h, shared['raw']['ffn_g'][i:i + 1], *ffn_cast[i], 0, min(n, 2 * tm), tf)
        cast = shared['raw']['ffn'] if (i + 1 not in ffn_cast and i + 1 < depth) else ()
        last = i == depth - 1
        h, casted, stacked = _ple(h, shared['raw']['ple_g'], shared['ple'][0], p_rows, i, shared['ple'][1],
                                  g_final, last, tm, cast, i + 1, (lats, krs) if last else ())
        if casted:
            ffn_cast[i + 1] = casted
    lat, kr = (t.reshape(n_mla, batch, seq, -1) for t in stacked)
    return h.reshape(batch, seq, d), lat, kr, jnp.stack(sks), jnp.stack(svs)


def kernel(x_prompt, x_sample, p_prompt, p_sample, cache_mla_latent, cache_mla_krope, state_swa_k, state_swa_v, g_attn_norm, w_mla_in, g_mla_q_a, w_mla_q_up, g_mla_kv_a, w_mla_kv_up, g_mla_q_nope, g_mla_q_rope, g_mla_k_nope, g_mla_k_rope, w_mla_out, w_swa_qkv, g_swa_q, g_swa_k, swa_sinks, w_swa_out, g_ffn_norm, w_ffn_gate, w_ffn_up, w_ffn_down, g_ple_norm, w_ple_gate, w_ple_proj, g_final):
    depth = g_attn_norm.shape[0]
    layers = []
    for i in range(depth):
        j = i // 2
        if i % 2 == 0:
            layers.append(('mla', _prep_mla(j, g_attn_norm[i], w_mla_in, g_mla_q_a, w_mla_q_up,
                                            g_mla_kv_a, w_mla_kv_up, g_mla_q_nope, g_mla_q_rope,
                                            g_mla_k_nope, g_mla_k_rope)))
        else:
            layers.append(('swa', _prep_swa(j, g_attn_norm[i], w_swa_qkv, g_swa_q, g_swa_k, swa_sinks)))
    shared = dict(raw=dict(ffn=(w_ffn_gate, w_ffn_up, w_ffn_down), ffn_g=g_ffn_norm[:, None, :],
                           ple_gate=w_ple_gate, ple_proj=w_ple_proj, ple_g=g_ple_norm[:, None, :],
                           mla_out=w_mla_out, swa_out=w_swa_out))
    gfin = g_final[None, :]

    seq = x_prompt.shape[1]
    t = x_sample.shape[1]
    past = cache_mla_latent.shape[2]
    pos_p = np.arange(seq)
    pos_s = past + np.arange(t)
    y_p, lat_p, kr_p, sk_p, sv_p = _trunk(x_prompt, p_prompt, pos_p, None, layers, shared,
                                          gfin, tm=512, tf=512)
    caches = (cache_mla_latent, cache_mla_krope, state_swa_k, state_swa_v)
    n_s = x_sample.shape[0] * t
    y_s, lat_s, kr_s, sk_s, sv_s = _trunk(x_sample, p_sample, pos_s, caches, layers, shared,
                                          gfin, tm=n_s, tf=512)
    return (y_p, y_s, lat_p, kr_p, sk_p, sv_p, lat_s, kr_s, sk_s, sv_s)
```

```python
import functools

import numpy as np
import jax
import jax.numpy as jnp
from jax import lax
from jax.experimental import pallas as pl
from jax.experimental.pallas import tpu as pltpu

F32 = jnp.float32
BF16 = jnp.bfloat16

EPS = 1e-6
CHUNK = 64
WINDOW = 128
MLA_THETA = 10000.0
SWA_THETA = 500000.0
MLA_NOPE = 128
MLA_ROPE = 64
MLA_V = 128
SWA_HEAD_DIM = 64
SWA_ROT = SWA_HEAD_DIM // 4

LANES = 128
VMEM_LIMIT = 56 * 1024 * 1024
NEG = -1e30
LOG2E = 1.4426950408889634


def _params(*sem):
    return pltpu.CompilerParams(dimension_semantics=sem, vmem_limit_bytes=VMEM_LIMIT)


def _resident(shape, layer=None):
    if layer is None:
        zeros = (0,) * len(shape)
        return pl.BlockSpec(shape, lambda *_: zeros, pipeline_mode=pl.Buffered(1))
    index = (layer,) + (0,) * (len(shape) - 1)
    return pl.BlockSpec((None,) + tuple(shape[1:]), lambda *_: index, pipeline_mode=pl.Buffered(1))


def _cast_item(a, steps, layer=None, col_split=1):
    nl, r, c = a.shape
    lead, first = (nl, 0) if layer is None else (1, layer)
    return dict(a=a, block=(lead, r * col_split // steps, c // col_split),
                in_idx=lambda s: (first, s // col_split, s % col_split),
                out_idx=lambda s: (0, s // col_split, s % col_split),
                out_shape=jax.ShapeDtypeStruct((lead, r, c), BF16))


def _cast_specs(items, step_of):
    ins = [pl.BlockSpec(it['block'], lambda *g, it=it: it['in_idx'](step_of(*g))) for it in items]
    outs = [pl.BlockSpec(it['block'], lambda *g, it=it: it['out_idx'](step_of(*g))) for it in items]
    return ins, outs, [it['out_shape'] for it in items]


def _cast_chunks(srcs, dsts):
    for src, dst in zip(srcs, dsts):
        dst[...] = src[...].astype(BF16)


def _rms(x, g):
    return x * lax.rsqrt(jnp.mean(x * x, axis=-1, keepdims=True) + EPS) * g


def _dot(a, b):
    return jnp.dot(a, b, preferred_element_type=F32)


def _dot_t(a, b):
    return lax.dot_general(a, b, (((1,), (1,)), ((), ())), preferred_element_type=F32)


def _rope_dup(x, g, tab):
    ss = jnp.sum(x * x, axis=-1, keepdims=True)
    y = x * lax.rsqrt(ss * (1.0 / LANES) + EPS) * g
    t = y * tab
    return t + pltpu.roll(t, LANES // 2, axis=1)


def _mla_heads(c_q, latb, krb, tab, wq_ref, wkn_ref, wv_ref, gqn, gqr, gkn, gsum_ref,
               q_ref, k_ref, v_ref, heads):
    gsum = gsum_ref[...]
    pair = 2 * LANES

    def inv_rms(x):
        return lax.rsqrt(_dot((x * x).astype(BF16), gsum) * (1.0 / LANES) + EPS)

    def finish(p, xq, xk, xv):
        if xq is None:
            kn = jnp.concatenate([_rms(xk[:, :LANES], gkn[:, :LANES]),
                                  _rms(xk[:, LANES:], gkn[:, LANES:])], axis=1)
        else:
            kn = xk * inv_rms(xk) * gkn
        if xq is not None:
            qn = xq[:, :pair] * inv_rms(xq[:, :pair]) * gqn
            t = xq[:, pair:] * inv_rms(xq[:, pair:]) * gqr * tab
        for u in range(2):
            h = 2 * p + u
            sl = slice(u * LANES, (u + 1) * LANES)
            k_ref[h, :, 0:MLA_NOPE] = kn[:, sl].astype(BF16)
            k_ref[h, :, MLA_NOPE:MLA_NOPE + MLA_ROPE] = krb
            v_ref[h] = xv[:, sl].astype(BF16)
            if xq is not None:
                q_ref[h, :, 0:MLA_NOPE] = qn[:, sl].astype(BF16)
                tu = t[:, sl]
                qr = tu + pltpu.roll(tu, LANES // 2, axis=1)
                q_ref[h, :, MLA_NOPE:MLA_NOPE + MLA_ROPE] = qr[:, :MLA_ROPE].astype(BF16)

    pending = None
    for p in range(heads // 2):
        xq = None if c_q is None else _dot(c_q, wq_ref[:, p * 2 * pair:(p + 1) * 2 * pair])
        xk = _dot(latb, wkn_ref[:, p * pair:(p + 1) * pair])
        xv = _dot(latb, wv_ref[:, p * pair:(p + 1) * pair])
        if pending is not None:
            finish(*pending)
        pending = (p, xq, xk, xv)
    finish(*pending)


def _mla_proj_kernel(h_ref, gattn_ref, win_ref, gqa_ref, gkva_ref, gkr_ref, tab_ref, wq_ref,
                     gqn_ref, gqr_ref, wkn_ref, wv_ref, gkn_ref, gsum_ref,
                     lat_ref, kr_ref, q_ref, k_ref, v_ref, *, heads, q_lora, kv_lora, scale):
    hn = _rms(h_ref[...], gattn_ref[...]).astype(BF16)
    a = _dot(hn, win_ref[...])
    c_q = _rms(a[:, :q_lora], gqa_ref[...]).astype(BF16)
    lat = _rms(a[:, q_lora:q_lora + kv_lora], gkva_ref[...])
    lat_ref[...] = lat
    tab = tab_ref[...]
    kr = _rope_dup(a[:, q_lora + kv_lora:], gkr_ref[...], tab)
    kr_ref[...] = kr[:, :MLA_ROPE]
    _mla_heads(c_q, lat.astype(BF16), kr[:, :MLA_ROPE].astype(BF16), jnp.concatenate([tab, tab], axis=1),
               wq_ref, wkn_ref, wv_ref, gqn_ref[...] * scale, gqr_ref[...] * scale, gkn_ref[...],
               gsum_ref, q_ref, k_ref, v_ref, heads)


def _mla_expand_kernel(lat_ref, kr_ref, wkn_ref, wv_ref, gkn_ref, gsum_ref, k_ref, v_ref, *, heads):
    _mla_heads(None, lat_ref[...].astype(BF16), kr_ref[...].astype(BF16), None, None, wkn_ref, wv_ref,
               None, None, gkn_ref[...], gsum_ref, None, k_ref, v_ref, heads)


def _mla_proj(h, w, tab, tm):
    n, d = h.shape
    heads = w['heads']
    q_lora, kv_lora = w['g_q_a'].shape[1], w['g_kv_a'].shape[1]
    qk = MLA_NOPE + MLA_ROPE
    nt = tab.shape[0] // tm
    row = lambda i: (i, 0)
    hrow = lambda i: (0, i, 0)
    kern = functools.partial(_mla_proj_kernel, heads=heads, q_lora=q_lora, kv_lora=kv_lora,
                             scale=qk ** -0.5 * LOG2E)
    weights = (w['g_attn'], w['w_in'], w['g_q_a'], w['g_kv_a'], w['g_k_rope'], tab, w['w_q'],
               w['g_q_nope'], w['g_q_rope'], w['w_kn'], w['w_v'], w['g_k_nope'], w['gsum'])
    return pl.pallas_call(
        kern,
        grid=(n // tm,),
        in_specs=[
            pl.BlockSpec((tm, d), row),
            _resident(w['g_attn'].shape), _resident(w['w_in'].shape),
            _resident(w['g_q_a'].shape), _resident(w['g_kv_a'].shape), _resident(w['g_k_rope'].shape),
            pl.BlockSpec((tm, LANES), lambda i: (i % nt, 0)),
            _resident(w['w_q'].shape), _resident(w['g_q_nope'].shape), _resident(w['g_q_rope'].shape),
            _resident(w['w_kn'].shape), _resident(w['w_v'].shape), _resident(w['g_k_nope'].shape),
            _resident(w['gsum'].shape),
        ],
        out_specs=[
            pl.BlockSpec((tm, kv_lora), row),
            pl.BlockSpec((tm, MLA_ROPE), row),
            pl.BlockSpec((heads, tm, qk), hrow),
            pl.BlockSpec((heads, tm, qk), hrow),
            pl.BlockSpec((heads, tm, MLA_V), hrow),
        ],
        out_shape=[
            jax.ShapeDtypeStruct((n, kv_lora), F32),
            jax.ShapeDtypeStruct((n, MLA_ROPE), F32),
            jax.ShapeDtypeStruct((heads, n, qk), BF16),
            jax.ShapeDtypeStruct((heads, n, qk), BF16),
            jax.ShapeDtypeStruct((heads, n, MLA_V), BF16),
        ],
        compiler_params=_params("parallel"),
        name="mla_proj",
    )(h, *weights)


def _mla_expand_call(lat, kr, layer, w, tm):
    _, n, kv_lora = lat.shape
    heads = w['heads']
    qk = MLA_NOPE + MLA_ROPE
    hrow = lambda i: (0, i, 0)
    return pl.pallas_call(
        functools.partial(_mla_expand_kernel, heads=heads),
        grid=(n // tm,),
        in_specs=[pl.BlockSpec((None, tm, kv_lora), lambda i: (layer, i, 0)),
                  pl.BlockSpec((None, tm, MLA_ROPE), lambda i: (layer, i, 0)),
                  _resident(w['w_kn'].shape), _resident(w['w_v'].shape),
                  _resident(w['g_k_nope'].shape), _resident(w['gsum'].shape)],
        out_specs=[pl.BlockSpec((heads, tm, qk), hrow), pl.BlockSpec((heads, tm, MLA_V), hrow)],
        out_shape=[jax.ShapeDtypeStruct((heads, n, qk), BF16),
                   jax.ShapeDtypeStruct((heads, n, MLA_V), BF16)],
        compiler_params=_params("parallel"),
        name="mla_expand",
    )(lat, kr, w['w_kn'], w['w_v'], w['g_k_nope'], w['gsum'])


def _mla_attn_kernel(q_ref, k_ref, v_ref, *rest, hb, tq, n_cast):
    cast_in, o_ref, cast_out, s_ref = rest[:n_cast], rest[n_cast], rest[n_cast + 1:-1], rest[-1]
    _cast_chunks(cast_in, cast_out)
    i = pl.program_id(2)
    kc = lax.broadcasted_iota(jnp.int32, (tq, tq), 0) // CHUNK
    qc = lax.broadcasted_iota(jnp.int32, (tq, tq), 1) // CHUNK
    diag_bias = jnp.where(kc <= qc, 0.0, NEG).astype(F32)
    qs = [q_ref[h] for h in range(hb)]

    def scores(j, slot):
        start = pl.multiple_of(j * tq, tq)
        mx = []
        for h in range(hb):
            s = _dot_t(k_ref[h, pl.ds(start, tq), :], qs[h])
            s_ref[slot, h] = s
            mx.append(jnp.max(s, axis=0, keepdims=True))
        return tuple(mx)

    def update(j, slot, carry, mx, bias=None):
        start = pl.multiple_of(j * tq, tq)
        ps, stats = [], []
        for h in range(hb):
            m, l, _ = carry[h]
            if bias is None:
                s, blk_max = s_ref[slot, h], mx[h]
            else:
                s = s_ref[slot, h] + bias
                blk_max = jnp.max(s, axis=0, keepdims=True)
            m_new = jnp.maximum(m, blk_max)
            alpha = jnp.exp2(m - m_new)
            p = jnp.exp2(s - m_new)
            stats.append((m_new, alpha * l + jnp.sum(p, axis=0, keepdims=True), alpha))
            ps.append(p.astype(BF16))
        out = []
        for h in range(hb):
            pv = lax.dot_general(v_ref[h, pl.ds(start, tq), :], ps[h],
                                 (((0,), (0,)), ((), ())), preferred_element_type=F32)
            m_new, l, alpha = stats[h]
            out.append((m_new, l, alpha * carry[h][2] + pv))
        return tuple(out)

    def finish(carry):
        for h in range(hb):
            _, l, acc = carry[h]
            o_ref[:, h * MLA_V:(h + 1) * MLA_V] = (acc / l).T.astype(BF16)

    def pair(t, state):
        carry, mx0 = state
        mx1 = scores(2 * t + 1, 1)
        carry = update(2 * t, 0, carry, mx0)
        mx0 = scores(2 * t + 2, 0)
        return update(2 * t + 1, 1, carry, mx1), mx0

    init = tuple((jnp.full((1, tq), NEG, F32), jnp.zeros((1, tq), F32), jnp.zeros((MLA_V, tq), F32))
                 for _ in range(hb))
    carry, mx0 = lax.fori_loop(0, i // 2, pair, (init, scores(0, 0)))

    @pl.when(i % 2 == 0)
    def _():
        finish(update(i, 0, carry, None, diag_bias))

    @pl.when(i % 2 == 1)
    def _():
        scores(i, 1)
        finish(update(i, 1, update(i - 1, 0, carry, mx0), None, diag_bias))


def _mla_attn_steps(batch, heads, seq, tq, hb):
    return batch * (heads // hb) * (seq // tq)


def _mla_attn(q, k, v, batch, seq, tq, hb, cast=()):
    heads, n, qk = q.shape
    nq = seq // tq
    ng = heads // hb
    kv_map = lambda b, g, i: (g, b, 0)
    cast_in, cast_out, cast_shapes = _cast_specs(cast, lambda b, g, i: (b * ng + g) * nq + i)
    out = pl.pallas_call(
        functools.partial(_mla_attn_kernel, hb=hb, tq=tq, n_cast=len(cast)),
        grid=(batch, ng, nq),
        in_specs=[pl.BlockSpec((hb, tq, qk), lambda b, g, i: (g, b * nq + i, 0)),
                  pl.BlockSpec((hb, seq, qk), kv_map),
                  pl.BlockSpec((hb, seq, MLA_V), kv_map)] + cast_in,
        out_specs=[pl.BlockSpec((tq, hb * MLA_V), lambda b, g, i: (b * nq + i, g))] + cast_out,
        out_shape=[jax.ShapeDtypeStruct((n, heads * MLA_V), BF16)] + cast_shapes,
        scratch_shapes=[pltpu.VMEM((2, hb, tq, tq), F32)],
        compiler_params=_params("parallel", "parallel", "arbitrary"),
        name="mla_attn",
    )(q, k, v, *[it['a'] for it in cast])
    return out[0], tuple(out[1:])


def _mla_attn_sample_kernel(q_ref, kc_ref, kn_ref, vc_ref, vn_ref, o_ref, *, hb):
    ss = [(_dot_t(kc_ref[h], q_ref[h]), _dot_t(kn_ref[h], q_ref[h])) for h in range(hb)]
    ps = []
    for s_c, s_n in ss:
        m = jnp.maximum(jnp.max(s_c, axis=0, keepdims=True), jnp.max(s_n, axis=0, keepdims=True))
        p_c = jnp.exp2(s_c - m)
        p_n = jnp.exp2(s_n - m)
        l = jnp.sum(p_c, axis=0, keepdims=True) + jnp.sum(p_n, axis=0, keepdims=True)
        ps.append((p_c.astype(BF16), p_n.astype(BF16), l))
    tn = (((0,), (0,)), ((), ()))
    for h, (p_c, p_n, l) in enumerate(ps):
        acc = (lax.dot_general(vc_ref[h], p_c, tn, preferred_element_type=F32)
               + lax.dot_general(vn_ref[h], p_n, tn, preferred_element_type=F32))
        o_ref[:, h * MLA_V:(h + 1) * MLA_V] = (acc / l).T.astype(BF16)


def _mla_attn_sample(q, k_cache, k_new, v_cache, v_new, batch, past, t, hb):
    heads, n, qk = q.shape
    q_pos = past + np.arange(t)
    k_pos = np.arange(past + t)
    assert np.all((k_pos // CHUNK)[None, :] <= (q_pos // CHUNK)[:, None])
    m3 = lambda b, g: (g, b, 0)
    return pl.pallas_call(
        functools.partial(_mla_attn_sample_kernel, hb=hb),
        grid=(batch, heads // hb),
        in_specs=[pl.BlockSpec((hb, t, qk), m3), pl.BlockSpec((hb, past, qk), m3),
                  pl.BlockSpec((hb, t, qk), m3), pl.BlockSpec((hb, past, MLA_V), m3),
                  pl.BlockSpec((hb, t, MLA_V), m3)],
        out_specs=pl.BlockSpec((t, hb * MLA_V), lambda b, g: (b, g)),
        out_shape=jax.ShapeDtypeStruct((n, heads * MLA_V), BF16),
        compiler_params=_params("parallel", "parallel"),
        name="mla_attn_sample",
    )(q, k_cache, k_new, v_cache, v_new)


def _swa_proj_kernel(h_ref, gattn_ref, w_ref, gq_ref, gk_ref, tc_ref, ts1_ref, ts2_ref, gmat_ref,
                     q_ref, kd_ref, vd_ref, kf_ref, vf_ref, *, q_cols, kv_heads, scale):
    hn = _rms(h_ref[...], gattn_ref[...]).astype(BF16)
    tc, ts1, ts2 = tc_ref[...], ts1_ref[...], ts2_ref[...]
    gmat = gmat_ref[...]

    def rope(x, ss, g):
        y = x * lax.rsqrt(ss * (1.0 / SWA_HEAD_DIM) + EPS) * g
        return (y * tc + pltpu.roll(y, LANES - SWA_ROT // 2, axis=1) * ts1
                + pltpu.roll(y, SWA_ROT // 2, axis=1) * ts2)

    gq = gq_ref[...] * scale
    gk = gk_ref[...]
    n_q = q_cols // 2
    n_kv = kv_heads // 4
    lo = lax.broadcasted_iota(jnp.int32, (1, LANES), 1) < SWA_HEAD_DIM

    def put_dup(col, first_head, dup_ref, flat_ref):
        swapped = pltpu.roll(col, SWA_HEAD_DIM, axis=1)
        for u, d in enumerate((jnp.where(lo, col, swapped), jnp.where(lo, swapped, col))):
            dup_ref[first_head + u] = d.astype(BF16)
            flat_ref[first_head + u] = d[:, :SWA_HEAD_DIM]

    def finish(c, x2):
        if c < n_q + n_kv:
            ss = _dot((x2 * x2).astype(BF16), gmat)
        for u in range(2):
            sl = slice(u * LANES, (u + 1) * LANES)
            if c < n_q:
                col = 2 * c + u
                q_ref[:, col * LANES:(col + 1) * LANES] = rope(x2[:, sl], ss[:, sl], gq).astype(BF16)
            elif c < n_q + n_kv:
                put_dup(rope(x2[:, sl], ss[:, sl], gk), 4 * (c - n_q) + 2 * u, kd_ref, kf_ref)
            else:
                put_dup(x2[:, sl], 4 * (c - n_q - n_kv) + 2 * u, vd_ref, vf_ref)

    pending = None
    for c in range(n_q + 2 * n_kv):
        x2 = _dot(hn, w_ref[:, c * 256:(c + 1) * 256])
        if pending is not None:
            finish(*pending)
        pending = (c, x2)
    finish(*pending)


def _swa_proj(h, w, tabs, tm):
    n, d = h.shape
    q_heads, kv_heads = w['q_heads'], w['kv_heads']
    q_cols = q_heads * SWA_HEAD_DIM // LANES
    nt = tabs[0].shape[0] // tm
    row = lambda i: (i, 0)
    hrow = lambda i: (0, i, 0)
    tspec = pl.BlockSpec((tm, LANES), lambda i: (i % nt, 0))
    kern = functools.partial(_swa_proj_kernel, q_cols=q_cols, kv_heads=kv_heads,
                             scale=SWA_HEAD_DIM ** -0.5 * LOG2E)
    return pl.pallas_call(
        kern,
        grid=(n // tm,),
        in_specs=[pl.BlockSpec((tm, d), row), _resident(w['g_attn'].shape),
                  _resident(w['w_qkv'].shape), _resident(w['g_q'].shape), _resident(w['g_k'].shape),
                  tspec, tspec, tspec, _resident(w['gmat'].shape)],
        out_specs=[pl.BlockSpec((tm, q_cols * LANES), row),
                   pl.BlockSpec((kv_heads, tm, LANES), hrow),
                   pl.BlockSpec((kv_heads, tm, LANES), hrow),
                   pl.BlockSpec((kv_heads, tm, SWA_HEAD_DIM), hrow),
                   pl.BlockSpec((kv_heads, tm, SWA_HEAD_DIM), hrow)],
        out_shape=[jax.ShapeDtypeStruct((n, q_cols * LANES), BF16),
                   jax.ShapeDtypeStruct((kv_heads, n, LANES), BF16),
                   jax.ShapeDtypeStruct((kv_heads, n, LANES), BF16),
                   jax.ShapeDtypeStruct((kv_heads, n, SWA_HEAD_DIM), F32),
                   jax.ShapeDtypeStruct((kv_heads, n, SWA_HEAD_DIM), F32)],
        compiler_params=_params("parallel"),
        name="swa_proj",
    )(h, w['g_attn'], w['w_qkv'], w['g_q'], w['g_k'], *tabs, w['gmat'])


def _swa_attn_kernel(sink_ref, q_ref, kp_ref, kc_ref, vp_ref, vc_ref, o_ref, *,
                     kv_heads, group, banded):
    tq = q_ref.shape[0]
    n_p, n_c = kp_ref.shape[1], kc_ref.shape[1]
    n_k = n_p + n_c
    lane = lax.broadcasted_iota(jnp.int32, (1, LANES), 1)
    row = lax.broadcasted_iota(jnp.int32, (LANES, 1), 0)
    if banded:
        assert n_p == WINDOW == LANES
        tiles = [(t * LANES, LANES, t * LANES, 2 * LANES) for t in range(tq // LANES)]
    else:
        tiles = [(0, tq, 0, n_k)]
    biases = []
    for q0, qn, k0, kn in tiles:
        if not banded:
            biases.append(None)
            continue
        win = WINDOW // CHUNK
        kc = (lax.broadcasted_iota(jnp.int32, (kn, qn), 0) + k0) // CHUNK - n_p // CHUNK
        qc = (lax.broadcasted_iota(jnp.int32, (kn, qn), 1) + q0) // CHUNK
        ok = (kc <= qc) & (kc >= qc - win) & ((kc >= 0) | (pl.program_id(1) > 0))
        biases.append(jnp.where(ok, 0.0, NEG).astype(F32))
    zero = jnp.zeros((), BF16)
    half = group // 2
    for kh in range(kv_heads):
        k_all = jnp.concatenate([kp_ref[kh], kc_ref[kh]], axis=0)
        v_t = jnp.concatenate([vp_ref[kh], vc_ref[kh]], axis=0).T
        ks = (jnp.where(lane < SWA_HEAD_DIM, k_all, zero), jnp.where(lane < SWA_HEAD_DIM, zero, k_all))
        vs = (jnp.where(row < SWA_HEAD_DIM, v_t, zero), jnp.where(row < SWA_HEAD_DIM, zero, v_t))
        cols = [q_ref[:, (kh * half + c) * LANES:(kh * half + c + 1) * LANES] for c in range(half)]
        for (q0, qn, k0, kn), bias in zip(tiles, biases):
            ss = [_dot_t(ks[u][k0:k0 + kn], cols[c][q0:q0 + qn])
                  for c in range(half) for u in range(2)]
            es, rdens = [], []
            for c in range(half):
                for u in range(2):
                    s = ss[2 * c + u]
                    if bias is not None:
                        s = s + bias
                    sink = sink_ref[2 * (kh * half + c) + u] * LOG2E
                    m = jnp.maximum(jnp.max(s, axis=0, keepdims=True), sink)
                    e = jnp.exp2(s - m)
                    rdens.append(1.0 / (jnp.sum(e, axis=0, keepdims=True) + jnp.exp2(sink - m)))
                    es.append(e.astype(BF16))
            for c in range(half):
                o_t = (_dot(vs[0][:, k0:k0 + kn], es[2 * c]) * rdens[2 * c]
                       + _dot(vs[1][:, k0:k0 + kn], es[2 * c + 1]) * rdens[2 * c + 1])
                col = kh * half + c
                o_ref[q0:q0 + qn, col * LANES:(col + 1) * LANES] = o_t.T.astype(BF16)


def _swa_attn(sinks, q, kd, vd, batch, seq, tq):
    n, dq = q.shape
    kv_heads = kd.shape[0]
    group = (dq // SWA_HEAD_DIM) // kv_heads
    nq = seq // tq
    per = tq // WINDOW
    prev = lambda b, i: (0, jnp.maximum(i * per - 1, 0) + b * nq * per, 0)
    cur = lambda b, i: (0, b * nq + i, 0)
    return pl.pallas_call(
        functools.partial(_swa_attn_kernel, kv_heads=kv_heads, group=group, banded=True),
        grid=(batch, nq),
        in_specs=[pl.BlockSpec(memory_space=pltpu.SMEM),
                  pl.BlockSpec((tq, dq), lambda b, i: (b * nq + i, 0)),
                  pl.BlockSpec((kv_heads, WINDOW, LANES), prev),
                  pl.BlockSpec((kv_heads, tq, LANES), cur),
                  pl.BlockSpec((kv_heads, WINDOW, LANES), prev),
                  pl.BlockSpec((kv_heads, tq, LANES), cur)],
        out_specs=pl.BlockSpec((tq, dq), lambda b, i: (b * nq + i, 0)),
        out_shape=jax.ShapeDtypeStruct((n, dq), BF16),
        compiler_params=_params("parallel", "arbitrary"),
        name="swa_attn",
    )(sinks, q, kd, kd, vd, vd)


def _swa_attn_sample(sinks, q, kd_cache, kd_new, vd_cache, vd_new, batch, keep, t):
    n, dq = q.shape
    kv_heads = kd_new.shape[0]
    group = (dq // SWA_HEAD_DIM) // kv_heads
    blk = lambda b: (0, b, 0)
    return pl.pallas_call(
        functools.partial(_swa_attn_kernel, kv_heads=kv_heads, group=group, banded=False),
        grid=(batch,),
        in_specs=[pl.BlockSpec(memory_space=pltpu.SMEM),
                  pl.BlockSpec((t, dq), lambda b: (b, 0)),
                  pl.BlockSpec((kv_heads, keep, LANES), blk),
                  pl.BlockSpec((kv_heads, t, LANES), blk),
                  pl.BlockSpec((kv_heads, keep, LANES), blk),
                  pl.BlockSpec((kv_heads, t, LANES), blk)],
        out_specs=pl.BlockSpec((t, dq), lambda b: (b, 0)),
        out_shape=jax.ShapeDtypeStruct((n, dq), BF16),
        compiler_params=_params("parallel"),
        name="swa_attn_sample",
    )(sinks, q, kd_cache, kd_new, vd_cache, vd_new)


def _oproj_kernel(h_ref, o_ref, w_ref, out_ref):
    out_ref[...] = h_ref[...] + _dot(o_ref[...], w_ref[...])


def _oproj(h, o, w, layer, tm):
    n, d = h.shape
    row = lambda i: (i, 0)
    return pl.pallas_call(
        _oproj_kernel,
        grid=(n // tm,),
        in_specs=[pl.BlockSpec((tm, d), row), pl.BlockSpec((tm, o.shape[1]), row),
                  _resident(w.shape, layer)],
        out_specs=pl.BlockSpec((tm, d), row),
        out_shape=jax.ShapeDtypeStruct((n, d), F32),
        compiler_params=_params("parallel"),
        name="attn_out_proj",
    )(h, o, w)


def _ffn_kernel(h_ref, g_ref, wg_ref, wu_ref, wd_ref, out_ref, hn_ref):
    @pl.when(pl.program_id(1) == 0)
    def _():
        x = h_ref[...]
        hn_ref[...] = _rms(x, g_ref[...]).astype(BF16)
        out_ref[...] = x

    hn = hn_ref[...]
    w = wg_ref.shape[1] // 2
    acts = []
    for c in range(2):
        a = _dot(hn, wg_ref[:, c * w:(c + 1) * w])
        b = _dot(hn, wu_ref[:, c * w:(c + 1) * w])
        acts.append((a * jax.nn.sigmoid(a) * b).astype(BF16))
    out_ref[...] += _dot(acts[0], wd_ref[0:w, :]) + _dot(acts[1], wd_ref[w:2 * w, :])


def _ffn(h, g, wg, wu, wd, layer, tm, tf):
    n, d = h.shape
    f = wg.shape[2]
    row = lambda i, j: (i, 0)
    return pl.pallas_call(
        _ffn_kernel,
        grid=(n // tm, f // tf),
        in_specs=[pl.BlockSpec((tm, d), row), pl.BlockSpec((None, 1, d), lambda i, j: (layer, 0, 0)),
                  pl.BlockSpec((None, d, tf), lambda i, j: (layer, 0, j)),
                  pl.BlockSpec((None, d, tf), lambda i, j: (layer, 0, j)),
                  pl.BlockSpec((None, tf, d), lambda i, j: (layer, j, 0))],
        out_specs=pl.BlockSpec((tm, d), row),
        out_shape=jax.ShapeDtypeStruct((n, d), F32),
        scratch_shapes=[pltpu.VMEM((tm, d), BF16)],
        compiler_params=_params("parallel", "arbitrary"),
        name="swiglu_ffn",
    )(h, g, wg, wu, wd)


def _ple_kernel(h_ref, g_ref, wgate_ref, p_ref, wproj_ref, gfin_ref, *rest, final, n_cast, stack):
    n_rows = sum(stack)
    cast_in, rows_in = rest[:n_cast], rest[n_cast:n_cast + n_rows]
    out_ref = rest[n_cast + n_rows]
    cast_out = rest[n_cast + n_rows + 1:2 * n_cast + n_rows + 1]
    stacked = rest[2 * n_cast + n_rows + 1:]
    first = 0
    for dst, count in zip(stacked, stack):
        for s in range(count):
            dst[s] = rows_in[first + s][...]
        first += count
    x = h_ref[...]
    gate = jax.nn.sigmoid(_dot(_rms(x, g_ref[...]).astype(BF16), wgate_ref[...]))
    y = x + gate * _dot(p_ref[...].astype(BF16), wproj_ref[...])
    if final:
        y = _rms(y, gfin_ref[...])
    out_ref[...] = y
    _cast_chunks(cast_in, cast_out)


def _ple(h, g, wgate, p, layer, wproj, gfin, final, tm, cast=(), cast_layer=0, stack=()):
    n, d = h.shape
    steps = n // tm
    row = lambda i: (i, 0)
    items = [_cast_item(a, steps, cast_layer) for a in cast]
    cast_specs, cast_out_specs, cast_shapes = _cast_specs(items, lambda i: i)
    rows = [a for group in stack for a in group]
    row_specs = [pl.BlockSpec((tm, a.shape[1]), row) for a in rows]
    stack_specs = [pl.BlockSpec((len(gr), tm, gr[0].shape[1]), lambda i: (0, i, 0)) for gr in stack]
    stack_shapes = [jax.ShapeDtypeStruct((len(gr), n, gr[0].shape[1]), gr[0].dtype) for gr in stack]
    out = pl.pallas_call(
        functools.partial(_ple_kernel, final=final, n_cast=len(cast), stack=tuple(len(gr) for gr in stack)),
        grid=(steps,),
        in_specs=[pl.BlockSpec((tm, d), row), _resident(g.shape, layer), _resident(wgate.shape, layer),
                  pl.BlockSpec((None, tm, p.shape[2]), lambda i: (layer, i, 0)),
                  _resident(wproj.shape, layer), _resident(gfin.shape)] + cast_specs + row_specs,
        out_specs=[pl.BlockSpec((tm, d), row)] + cast_out_specs + stack_specs,
        out_shape=[jax.ShapeDtypeStruct((n, d), F32)] + cast_shapes + stack_shapes,
        compiler_params=_params("parallel"),
        name="ple_embed",
    )(h, g, wgate, p, wproj, gfin, *cast, *rows)
    n_c = len(cast)
    return out[0], tuple(out[1:1 + n_c]), tuple(out[1 + n_c:])


def _angles(pos, r, theta):
    inv = np.power(np.float64(theta), -np.arange(0, r, 2, dtype=np.float64) / r)
    ang = pos.astype(np.float64)[:, None] * inv[None, :]
    return np.cos(ang).astype(np.float32), np.sin(ang).astype(np.float32)


def _mla_table(pos):
    cos, sin = _angles(pos, MLA_ROPE, MLA_THETA)
    return np.concatenate([cos, cos, -sin, sin], axis=-1)


def _swa_tables(pos):
    cos, sin = _angles(pos, SWA_ROT, SWA_THETA)
    s = cos.shape[0]
    rest = SWA_HEAD_DIM - SWA_ROT
    one = np.ones((s, rest), np.float32)
    zero = np.zeros((s, rest), np.float32)
    zh = np.zeros_like(sin)
    tc = np.concatenate([cos, cos, one], axis=-1)
    ts1 = np.concatenate([-sin, zh, zero], axis=-1)
    ts2 = np.concatenate([zh, sin, zero], axis=-1)
    return tuple(np.concatenate([t, t], axis=-1) for t in (tc, ts1, ts2))


def _tile_rows(tab, rows):
    tab = tab if tab.shape[0] >= rows else np.tile(tab, (rows // tab.shape[0], 1))
    return jnp.asarray(tab)


def _swap_halves(x):
    half = x.shape[-1] // 2
    return jnp.concatenate([x[..., half:], x[..., :half]], axis=-1)


def _prep_mla(j, g_attn_norm_i, w_mla_in, g_mla_q_a, w_mla_q_up, g_mla_kv_a, w_mla_kv_up, g_mla_q_nope,
              g_mla_q_rope, g_mla_k_nope, g_mla_k_rope):
    q_lora, kv_lora = g_mla_q_a.shape[1], g_mla_kv_a.shape[1]
    qk = MLA_NOPE + MLA_ROPE
    heads = w_mla_q_up.shape[2] // qk
    w_in = w_mla_in[j]
    k_r = w_in[:, q_lora + kv_lora:]
    w_in = jnp.concatenate([w_in, _swap_halves(k_r)], axis=-1)
    wq = w_mla_q_up[j].reshape(q_lora, heads, qk)
    rope = wq[:, :, MLA_NOPE:]
    wq = jnp.concatenate([wq[:, :, :MLA_NOPE].reshape(q_lora, heads // 2, 2 * MLA_NOPE),
                          jnp.concatenate([rope, _swap_halves(rope)], axis=-1).reshape(q_lora, heads // 2, 2 * LANES)],
                         axis=-1).reshape(q_lora, heads * 2 * LANES)
    wkv = w_mla_kv_up[j].reshape(kv_lora, heads, MLA_NOPE + MLA_V)
    dup = lambda g: jnp.concatenate([g, _swap_halves(g)], axis=-1)[None, :]
    two = lambda g: jnp.concatenate([g, g], axis=-1)
    gsum = np.kron(np.eye(2), np.ones((LANES, LANES)))
    return dict(
        heads=heads, gsum=jnp.asarray(gsum, BF16),
        g_attn=g_attn_norm_i[None, :],
        w_in=w_in.astype(BF16),
        g_q_a=g_mla_q_a[j][None, :], g_kv_a=g_mla_kv_a[j][None, :],
        g_k_rope=dup(g_mla_k_rope[j]), g_q_rope=two(dup(g_mla_q_rope[j])),
        g_q_nope=two(g_mla_q_nope[j][None, :]), g_k_nope=two(g_mla_k_nope[j][None, :]),
        w_q=wq.astype(BF16),
        w_kn=wkv[:, :, :MLA_NOPE].reshape(kv_lora, heads * MLA_NOPE).astype(BF16),
        w_v=wkv[:, :, MLA_NOPE:].reshape(kv_lora, heads * MLA_V).astype(BF16),
    )


def _prep_swa(j, g_attn_norm_i, w_swa_qkv, g_swa_q, g_swa_k, swa_sinks):
    q_heads = swa_sinks.shape[1]
    w = w_swa_qkv[j]
    kv_heads = (w.shape[1] - q_heads * SWA_HEAD_DIM) // (2 * SWA_HEAD_DIM)
    assert kv_heads % 4 == 0 and q_heads % 4 == 0
    gmat = np.kron(np.eye(2 * LANES // SWA_HEAD_DIM), np.ones((SWA_HEAD_DIM, SWA_HEAD_DIM)))
    dup = lambda g: jnp.concatenate([g, g], axis=-1)[None, :]
    return dict(
        q_heads=q_heads, kv_heads=kv_heads,
        g_attn=g_attn_norm_i[None, :],
        w_qkv=w.astype(BF16),
        g_q=dup(g_swa_q[j]), g_k=dup(g_swa_k[j]),
        gmat=jnp.asarray(gmat, BF16),
        sinks=swa_sinks[j],
    )


def _kv_dup(t):
    b, l, kv, d = t.shape
    t = jnp.transpose(t, (2, 0, 1, 3)).reshape(kv, b * l, d)
    return jnp.concatenate([t, t], axis=-1).astype(BF16)


def _first_casts(shared, steps):
    raw = shared['raw']
    wg, wu, wd = raw['ffn']
    return ([_cast_item(wg, steps, 0), _cast_item(wu, steps, 0), _cast_item(wd, steps, 0, col_split=8),
             _cast_item(raw['ple_gate'], steps), _cast_item(raw['ple_proj'], steps, col_split=8),
             _cast_item(raw['mla_out'], steps), _cast_item(raw['swa_out'], steps)])


def _trunk(x, p, pos, caches, layers, shared, g_final, tm, tf):
    batch, seq, d = x.shape
    n = batch * seq
    h = x.reshape(n, d)
    rows = max(tm, seq)
    mla_tab = _tile_rows(_mla_table(pos), rows)
    swa_tabs = tuple(_tile_rows(t, rows) for t in _swa_tables(pos))
    depth = len(layers)
    n_mla = sum(kind == 'mla' for kind, _ in layers)
    p_rows = p.reshape(depth, n, -1)
    lats, krs, sks, svs = [], [], [], []
    for i, (kind, w) in enumerate(layers):
        j = i // 2
        if kind == 'mla':
            lat, kr, q, k, v = _mla_proj(h, w, mla_tab, tm)
            lats.append(lat)
            krs.append(kr)
            if caches is None:
                tq, hb = 512, 4
                first = 'ple' not in shared
                cast = _first_casts(shared, _mla_attn_steps(batch, w['heads'], seq, tq, hb)) if first else ()
                o, casted = _mla_attn(q, k, v, batch, seq, tq=tq, hb=hb, cast=cast)
                if first:
                    shared['ffn'] = {0: casted[:3]}
                    shared['ple'] = casted[3:5]
                    shared['out'] = dict(mla=casted[5], swa=casted[6])
            else:
                lat_c, kr_c = caches[0], caches[1]
                past = lat_c.shape[2]
                k_c, v_c = _mla_expand_call(lat_c.reshape(lat_c.shape[0], batch * past, -1),
                                            kr_c.reshape(kr_c.shape[0], batch * past, -1), j, w, tm=512)
                o = _mla_attn_sample(q, k_c, k, v_c, v, batch, past, seq, hb=w['heads'])
        else:
            q, kd, vd, kf, vf = _swa_proj(h, w, swa_tabs, tm)
            kv_heads = w['kv_heads']
            unhead = lambda t, rows: jnp.transpose(
                t.reshape(kv_heads, batch, seq, SWA_HEAD_DIM)[:, :, seq - rows:], (1, 2, 0, 3))
            if caches is None:
                o = _swa_attn(w['sinks'], q, kd, vd, batch, seq, tq=256)
                keep = min(WINDOW, seq)
                sks.append(unhead(kf, keep))
                svs.append(unhead(vf, keep))
            else:
                ck, cv = caches[2][j], caches[3][j]
                keep = ck.shape[1]
                o = _swa_attn_sample(w['sinks'], q, _kv_dup(ck), kd, _kv_dup(cv), vd, batch, keep, seq)
                sks.append(jnp.concatenate([ck, unhead(kf, seq)], axis=1)[:, -keep:])
                svs.append(jnp.concatenate([cv, unhead(vf, seq)], axis=1)[:, -keep:])
        h = _oproj(h, o, shared['out'][kind], j, tm)
        ffn_cast = shared['ffn']
        h = _ffn(h, shared['raw']['ffn_g'][i:i + 1], *ffn_cast[i], 0, min(n, 2 * tm), tf)
        cast = shared['raw']['ffn'] if (i + 1 not in ffn_cast and i + 1 < depth) else ()
        last = i == depth - 1
        h, casted, stacked = _ple(h, shared['raw']['ple_g'], shared['ple'][0], p_rows, i, shared['ple'][1],
                                  g_final, last, tm, cast, i + 1, (lats, krs) if last else ())
        if casted:
            ffn_cast[i + 1] = casted
    lat, kr = (t.reshape(n_mla, batch, seq, -1) for t in stacked)
    return h.reshape(batch, seq, d), lat, kr, jnp.stack(sks), jnp.stack(svs)


def kernel(x_prompt, x_sample, p_prompt, p_sample, cache_mla_latent, cache_mla_krope, state_swa_k, state_swa_v, g_attn_norm, w_mla_in, g_mla_q_a, w_mla_q_up, g_mla_kv_a, w_mla_kv_up, g_mla_q_nope, g_mla_q_rope, g_mla_k_nope, g_mla_k_rope, w_mla_out, w_swa_qkv, g_swa_q, g_swa_k, swa_sinks, w_swa_out, g_ffn_norm, w_ffn_gate, w_ffn_up, w_ffn_down, g_ple_norm, w_ple_gate, w_ple_proj, g_final):
    depth = g_attn_norm.shape[0]
    layers = []
    for i in range(depth):
        j = i // 2
        if i % 2 == 0:
            layers.append(('mla', _prep_mla(j, g_attn_norm[i], w_mla_in, g_mla_q_a, w_mla_q_up,
                                            g_mla_kv_a, w_mla_kv_up, g_mla_q_nope, g_mla_q_rope,
                                            g_mla_k_nope, g_mla_k_rope)))
        else:
            layers.append(('swa', _prep_swa(j, g_attn_norm[i], w_swa_qkv, g_swa_q, g_swa_k, swa_sinks)))
    shared = dict(raw=dict(ffn=(w_ffn_gate, w_ffn_up, w_ffn_down), ffn_g=g_ffn_norm[:, None, :],
                           ple_gate=w_ple_gate, ple_proj=w_ple_proj, ple_g=g_ple_norm[:, None, :],
                           mla_out=w_mla_out, swa_out=w_swa_out))
    gfin = g_final[None, :]

    seq = x_prompt.shape[1]
    t = x_sample.shape[1]
    past = cache_mla_latent.shape[2]
    pos_p = np.arange(seq)
    pos_s = past + np.arange(t)
    y_p, lat_p, kr_p, sk_p, sv_p = _trunk(x_prompt, p_prompt, pos_p, None, layers, shared,
                                          gfin, tm=512, tf=512)
    caches = (cache_mla_latent, cache_mla_krope, state_swa_k, state_swa_v)
    n_s = x_sample.shape[0] * t
    y_s, lat_s, kr_s, sk_s, sv_s = _trunk(x_sample, p_sample, pos_s, caches, layers, shared,
                                          gfin, tm=n_s, tf=512)
    return (y_p, y_s, lat_p, kr_p, sk_p, sv_p, lat_s, kr_s, sk_s, sv_s)
```

```python
import functools

import numpy as np
import jax
import jax.numpy as jnp
from jax import lax
from jax.experimental import pallas as pl
from jax.experimental.pallas import tpu as pltpu

F32 = jnp.float32
BF16 = jnp.bfloat16

EPS = 1e-6
CHUNK = 64
WINDOW = 128
MLA_THETA = 10000.0
SWA_THETA = 500000.0
MLA_NOPE = 128
MLA_ROPE = 64
MLA_V = 128
SWA_HEAD_DIM = 64
SWA_ROT = SWA_HEAD_DIM // 4

LANES = 128
MXU_WIDTH = 256
VMEM_LIMIT = 56 * 1024 * 1024

ROWS = 512
ROWS_FFN = 1024
FFN_COLS = 512
MLA_Q = 512
MLA_HEADS_PER_STEP = 4
SWA_Q = 256
NEG = -1e30
LOG2E = 1.4426950408889634


def _params(*sem):
    return pltpu.CompilerParams(dimension_semantics=sem, vmem_limit_bytes=VMEM_LIMIT)


def _resident(shape, layer=None):
    if layer is None:
        zeros = (0,) * len(shape)
        return pl.BlockSpec(shape, lambda *_: zeros, pipeline_mode=pl.Buffered(1))
    index = (layer,) + (0,) * (len(shape) - 1)
    return pl.BlockSpec((None,) + tuple(shape[1:]), lambda *_: index, pipeline_mode=pl.Buffered(1))


def _cast_item(a, steps, layer=None, col_split=1):
    nl, r, c = a.shape
    assert (r * col_split) % (steps * 16) == 0 and c % (col_split * LANES) == 0, (a.shape, steps)
    lead, first = (nl, 0) if layer is None else (1, layer)
    return dict(a=a, block=(lead, r * col_split // steps, c // col_split),
                in_idx=lambda s: (first, s // col_split, s % col_split),
                out_idx=lambda s: (0, s // col_split, s % col_split),
                out_shape=jax.ShapeDtypeStruct((lead, r, c), BF16))


def _cast_specs(items, step_of):
    ins = [pl.BlockSpec(it['block'], lambda *g, it=it: it['in_idx'](step_of(*g))) for it in items]
    outs = [pl.BlockSpec(it['block'], lambda *g, it=it: it['out_idx'](step_of(*g))) for it in items]
    return ins, outs, [it['out_shape'] for it in items]


def _cast_chunks(srcs, dsts):
    for src, dst in zip(srcs, dsts):
        dst[...] = src[...].astype(BF16)


def _rms(x, g):
    return x * lax.rsqrt(jnp.mean(x * x, axis=-1, keepdims=True) + EPS) * g


def _dot(a, b):
    return jnp.dot(a, b, preferred_element_type=F32)


def _dot_t(a, b):
    return lax.dot_general(a, b, (((1,), (1,)), ((), ())), preferred_element_type=F32)


def _rope_dup(x, g, tab):
    ss = jnp.sum(x * x, axis=-1, keepdims=True)
    y = x * lax.rsqrt(ss * (1.0 / LANES) + EPS) * g
    t = y * tab
    return t + pltpu.roll(t, LANES // 2, axis=1)


def _mla_heads(c_q, latb, krb, tab, wq_ref, wkn_ref, wv_ref, gqn, gqr, gkn, gsum_ref,
               q_ref, k_ref, v_ref, heads):
    gsum = gsum_ref[...]
    pair = 2 * LANES

    def inv_rms(x):
        return lax.rsqrt(_dot((x * x).astype(BF16), gsum) * (1.0 / LANES) + EPS)

    def finish(p, xq, xk, xv):
        kn = jnp.concatenate([_rms(xk[:, :LANES], gkn[:, :LANES]),
                              _rms(xk[:, LANES:], gkn[:, LANES:])], axis=1)
        if xq is not None:
            qn = xq[:, :pair] * inv_rms(xq[:, :pair]) * gqn
            t = xq[:, pair:] * inv_rms(xq[:, pair:]) * gqr * tab
        for u in range(2):
            h = 2 * p + u
            sl = slice(u * LANES, (u + 1) * LANES)
            k_ref[h, :, 0:MLA_NOPE] = kn[:, sl].astype(BF16)
            k_ref[h, :, MLA_NOPE:MLA_NOPE + MLA_ROPE] = krb
            v_ref[h] = xv[:, sl].astype(BF16)
            if xq is not None:
                q_ref[h, :, 0:MLA_NOPE] = qn[:, sl].astype(BF16)
                tu = t[:, sl]
                qr = tu + pltpu.roll(tu, LANES // 2, axis=1)
                q_ref[h, :, MLA_NOPE:MLA_NOPE + MLA_ROPE] = qr[:, :MLA_ROPE].astype(BF16)

    pending = None
    for p in range(heads // 2):
        xq = None if c_q is None else _dot(c_q, wq_ref[:, p * 2 * pair:(p + 1) * 2 * pair])
        xk = _dot(latb, wkn_ref[:, p * pair:(p + 1) * pair])
        xv = _dot(latb, wv_ref[:, p * pair:(p + 1) * pair])
        if pending is not None:
            finish(*pending)
        pending = (p, xq, xk, xv)
    finish(*pending)


def _mla_proj_kernel(h_ref, gattn_ref, win_ref, gqa_ref, gkva_ref, gkr_ref, tab_ref, wq_ref,
                     gqn_ref, gqr_ref, wkn_ref, wv_ref, gkn_ref, gsum_ref,
                     lat_ref, kr_ref, q_ref, k_ref, v_ref, *, heads, q_lora, kv_lora, scale):
    hn = _rms(h_ref[...], gattn_ref[...]).astype(BF16)
    a = _dot(hn, win_ref[...])
    c_q = _rms(a[:, :q_lora], gqa_ref[...]).astype(BF16)
    lat = _rms(a[:, q_lora:q_lora + kv_lora], gkva_ref[...])
    lat_ref[...] = lat
    tab = tab_ref[...]
    kr = _rope_dup(a[:, q_lora + kv_lora:], gkr_ref[...], tab)
    kr_ref[...] = kr[:, :MLA_ROPE]
    _mla_heads(c_q, lat.astype(BF16), kr[:, :MLA_ROPE].astype(BF16), jnp.concatenate([tab, tab], axis=1),
               wq_ref, wkn_ref, wv_ref, gqn_ref[...] * scale, gqr_ref[...] * scale, gkn_ref[...],
               gsum_ref, q_ref, k_ref, v_ref, heads)


def _mla_expand_kernel(lat_ref, kr_ref, wkn_ref, wv_ref, gkn_ref, gsum_ref, k_ref, v_ref, *, heads):
    _mla_heads(None, lat_ref[...].astype(BF16), kr_ref[...].astype(BF16), None, None, wkn_ref, wv_ref,
               None, None, gkn_ref[...], gsum_ref, None, k_ref, v_ref, heads)


def _mla_proj(h, w, tab, tm):
    n, d = h.shape
    heads = w['heads']
    q_lora, kv_lora = w['g_q_a'].shape[1], w['g_kv_a'].shape[1]
    qk = MLA_NOPE + MLA_ROPE
    nt = tab.shape[0] // tm
    row = lambda i: (i, 0)
    hrow = lambda i: (0, i, 0)
    kern = functools.partial(_mla_proj_kernel, heads=heads, q_lora=q_lora, kv_lora=kv_lora,
                             scale=qk ** -0.5 * LOG2E)
    weights = (w['g_attn'], w['w_in'], w['g_q_a'], w['g_kv_a'], w['g_k_rope'], tab, w['w_q'],
               w['g_q_nope'], w['g_q_rope'], w['w_kn'], w['w_v'], w['g_k_nope'], w['gsum'])
    return pl.pallas_call(
        kern,
        grid=(n // tm,),
        in_specs=[
            pl.BlockSpec((tm, d), row),
            _resident(w['g_attn'].shape), _resident(w['w_in'].shape),
            _resident(w['g_q_a'].shape), _resident(w['g_kv_a'].shape), _resident(w['g_k_rope'].shape),
            pl.BlockSpec((tm, LANES), lambda i: (i % nt, 0)),
            _resident(w['w_q'].shape), _resident(w['g_q_nope'].shape), _resident(w['g_q_rope'].shape),
            _resident(w['w_kn'].shape), _resident(w['w_v'].shape), _resident(w['g_k_nope'].shape),
            _resident(w['gsum'].shape),
        ],
        out_specs=[
            pl.BlockSpec((tm, kv_lora), row),
            pl.BlockSpec((tm, MLA_ROPE), row),
            pl.BlockSpec((heads, tm, qk), hrow),
            pl.BlockSpec((heads, tm, qk), hrow),
            pl.BlockSpec((heads, tm, MLA_V), hrow),
        ],
        out_shape=[
            jax.ShapeDtypeStruct((n, kv_lora), F32),
            jax.ShapeDtypeStruct((n, MLA_ROPE), F32),
            jax.ShapeDtypeStruct((heads, n, qk), BF16),
            jax.ShapeDtypeStruct((heads, n, qk), BF16),
            jax.ShapeDtypeStruct((heads, n, MLA_V), BF16),
        ],
        compiler_params=_params("parallel"),
        name="mla_proj",
    )(h, *weights)


def _mla_expand_call(lat, kr, layer, w, tm):
    _, n, kv_lora = lat.shape
    heads = w['heads']
    qk = MLA_NOPE + MLA_ROPE
    hrow = lambda i: (0, i, 0)
    return pl.pallas_call(
        functools.partial(_mla_expand_kernel, heads=heads),
        grid=(n // tm,),
        in_specs=[pl.BlockSpec((None, tm, kv_lora), lambda i: (layer, i, 0)),
                  pl.BlockSpec((None, tm, MLA_ROPE), lambda i: (layer, i, 0)),
                  _resident(w['w_kn'].shape), _resident(w['w_v'].shape),
                  _resident(w['g_k_nope'].shape), _resident(w['gsum'].shape)],
        out_specs=[pl.BlockSpec((heads, tm, qk), hrow), pl.BlockSpec((heads, tm, MLA_V), hrow)],
        out_shape=[jax.ShapeDtypeStruct((heads, n, qk), BF16),
                   jax.ShapeDtypeStruct((heads, n, MLA_V), BF16)],
        compiler_params=_params("parallel"),
        name="mla_expand",
    )(lat, kr, w['w_kn'], w['w_v'], w['g_k_nope'], w['gsum'])


def _mla_attn_kernel(q_ref, k_ref, v_ref, *rest, hb, tq, n_cast):
    cast_in, o_ref, cast_out, s_ref = rest[:n_cast], rest[n_cast], rest[n_cast + 1:-1], rest[-1]
    _cast_chunks(cast_in, cast_out)
    i = pl.program_id(2)
    kc = lax.broadcasted_iota(jnp.int32, (tq, tq), 0) // CHUNK
    qc = lax.broadcasted_iota(jnp.int32, (tq, tq), 1) // CHUNK
    diag_bias = jnp.where(kc <= qc, 0.0, NEG).astype(F32)
    qs = [q_ref[h] for h in range(hb)]

    def scores(j, slot):
        start = pl.multiple_of(j * tq, tq)
        mx = []
        for h in range(hb):
            s = _dot_t(k_ref[h, pl.ds(start, tq), :], qs[h])
            s_ref[slot, h] = s
            mx.append(jnp.max(s, axis=0, keepdims=True))
        return tuple(mx)

    def update(j, slot, carry, mx, bias=None):
        start = pl.multiple_of(j * tq, tq)
        ps, stats = [], []
        for h in range(hb):
            m, l, _ = carry[h]
            if bias is None:
                s, blk_max = s_ref[slot, h], mx[h]
            else:
                s = s_ref[slot, h] + bias
                blk_max = jnp.max(s, axis=0, keepdims=True)
            m_new = jnp.maximum(m, blk_max)
            alpha = jnp.exp2(m - m_new)
            p = jnp.exp2(s - m_new)
            stats.append((m_new, alpha * l + jnp.sum(p, axis=0, keepdims=True), alpha))
            ps.append(p.astype(BF16))
        out = []
        for h in range(hb):
            pv = lax.dot_general(v_ref[h, pl.ds(start, tq), :], ps[h],
                                 (((0,), (0,)), ((), ())), preferred_element_type=F32)
            m_new, l, alpha = stats[h]
            out.append((m_new, l, alpha * carry[h][2] + pv))
        return tuple(out)

    def finish(carry):
        for h in range(hb):
            _, l, acc = carry[h]
            o_ref[:, h * MLA_V:(h + 1) * MLA_V] = (acc / l).T.astype(BF16)

    def pair(t, state):
        carry, mx0 = state
        mx1 = scores(2 * t + 1, 1)
        carry = update(2 * t, 0, carry, mx0)
        mx0 = scores(2 * t + 2, 0)
        return update(2 * t + 1, 1, carry, mx1), mx0

    init = tuple((jnp.full((1, tq), NEG, F32), jnp.zeros((1, tq), F32), jnp.zeros((MLA_V, tq), F32))
                 for _ in range(hb))
    carry, mx0 = lax.fori_loop(0, i // 2, pair, (init, scores(0, 0)))

    @pl.when(i % 2 == 0)
    def _():
        finish(update(i, 0, carry, None, diag_bias))

    @pl.when(i % 2 == 1)
    def _():
        scores(i, 1)
        finish(update(i, 1, update(i - 1, 0, carry, mx0), None, diag_bias))


def _mla_attn_steps(batch, heads, seq, tq, hb):
    return batch * (heads // hb) * (seq // tq)


def _mla_attn(q, k, v, batch, seq, tq, hb, cast=()):
    heads, n, qk = q.shape
    nq = seq // tq
    ng = heads // hb
    kv_map = lambda b, g, i: (g, b, 0)
    cast_in, cast_out, cast_shapes = _cast_specs(cast, lambda b, g, i: (b * ng + g) * nq + i)
    out = pl.pallas_call(
        functools.partial(_mla_attn_kernel, hb=hb, tq=tq, n_cast=len(cast)),
        grid=(batch, ng, nq),
        in_specs=[pl.BlockSpec((hb, tq, qk), lambda b, g, i: (g, b * nq + i, 0)),
                  pl.BlockSpec((hb, seq, qk), kv_map),
                  pl.BlockSpec((hb, seq, MLA_V), kv_map)] + cast_in,
        out_specs=[pl.BlockSpec((tq, hb * MLA_V), lambda b, g, i: (b * nq + i, g))] + cast_out,
        out_shape=[jax.ShapeDtypeStruct((n, heads * MLA_V), BF16)] + cast_shapes,
        scratch_shapes=[pltpu.VMEM((2, hb, tq, tq), F32)],
        compiler_params=_params("parallel", "parallel", "arbitrary"),
        name="mla_attn",
    )(q, k, v, *[it['a'] for it in cast])
    return out[0], tuple(out[1:])


def _mla_attn_sample_kernel(q_ref, kc_ref, kn_ref, vc_ref, vn_ref, o_ref, *, hb):
    ss = [(_dot_t(kc_ref[h], q_ref[h]), _dot_t(kn_ref[h], q_ref[h])) for h in range(hb)]
    ps = []
    for s_c, s_n in ss:
        m = jnp.maximum(jnp.max(s_c, axis=0, keepdims=True), jnp.max(s_n, axis=0, keepdims=True))
        p_c = jnp.exp2(s_c - m)
        p_n = jnp.exp2(s_n - m)
        l = jnp.sum(p_c, axis=0, keepdims=True) + jnp.sum(p_n, axis=0, keepdims=True)
        ps.append((p_c.astype(BF16), p_n.astype(BF16), l))
    tn = (((0,), (0,)), ((), ()))
    for h, (p_c, p_n, l) in enumerate(ps):
        acc = (lax.dot_general(vc_ref[h], p_c, tn, preferred_element_type=F32)
               + lax.dot_general(vn_ref[h], p_n, tn, preferred_element_type=F32))
        o_ref[:, h * MLA_V:(h + 1) * MLA_V] = (acc / l).T.astype(BF16)


def _mla_attn_sample(q, k_cache, k_new, v_cache, v_new, batch, past, t, hb):
    heads, n, qk = q.shape
    q_pos = past + np.arange(t)
    k_pos = np.arange(past + t)
    assert np.all((k_pos // CHUNK)[None, :] <= (q_pos // CHUNK)[:, None])
    m3 = lambda b, g: (g, b, 0)
    return pl.pallas_call(
        functools.partial(_mla_attn_sample_kernel, hb=hb),
        grid=(batch, heads // hb),
        in_specs=[pl.BlockSpec((hb, t, qk), m3), pl.BlockSpec((hb, past, qk), m3),
                  pl.BlockSpec((hb, t, qk), m3), pl.BlockSpec((hb, past, MLA_V), m3),
                  pl.BlockSpec((hb, t, MLA_V), m3)],
        out_specs=pl.BlockSpec((t, hb * MLA_V), lambda b, g: (b, g)),
        out_shape=jax.ShapeDtypeStruct((n, heads * MLA_V), BF16),
        compiler_params=_params("parallel", "parallel"),
        name="mla_attn_sample",
    )(q, k_cache, k_new, v_cache, v_new)


def _swa_proj_kernel(h_ref, gattn_ref, w_ref, gq_ref, gk_ref, tc_ref, ts1_ref, ts2_ref, gmat_ref,
                     q_ref, kd_ref, vd_ref, kf_ref, vf_ref, *, q_cols, kv_heads, scale):
    hn = _rms(h_ref[...], gattn_ref[...]).astype(BF16)
    tc, ts1, ts2 = tc_ref[...], ts1_ref[...], ts2_ref[...]
    gmat = gmat_ref[...]

    def rope(x, ss, g):
        y = x * lax.rsqrt(ss * (1.0 / SWA_HEAD_DIM) + EPS) * g
        return (y * tc + pltpu.roll(y, LANES - SWA_ROT // 2, axis=1) * ts1
                + pltpu.roll(y, SWA_ROT // 2, axis=1) * ts2)

    gq = gq_ref[...] * scale
    gk = gk_ref[...]
    n_q = q_cols // 2
    n_kv = kv_heads // 4
    lo = lax.broadcasted_iota(jnp.int32, (1, LANES), 1) < SWA_HEAD_DIM

    def put_dup(col, first_head, dup_ref, flat_ref):
        swapped = pltpu.roll(col, SWA_HEAD_DIM, axis=1)
        for u, d in enumerate((jnp.where(lo, col, swapped), jnp.where(lo, swapped, col))):
            dup_ref[first_head + u] = d.astype(BF16)
            flat_ref[first_head + u] = d[:, :SWA_HEAD_DIM]

    def finish(c, x2):
        if c < n_q + n_kv:
            ss = _dot((x2 * x2).astype(BF16), gmat)
        for u in range(2):
            sl = slice(u * LANES, (u + 1) * LANES)
            if c < n_q:
                col = 2 * c + u
                q_ref[:, col * LANES:(col + 1) * LANES] = rope(x2[:, sl], ss[:, sl], gq).astype(BF16)
            elif c < n_q + n_kv:
                put_dup(rope(x2[:, sl], ss[:, sl], gk), 4 * (c - n_q) + 2 * u, kd_ref, kf_ref)
            else:
                put_dup(x2[:, sl], 4 * (c - n_q - n_kv) + 2 * u, vd_ref, vf_ref)

    pending = None
    for c in range(n_q + 2 * n_kv):
        x2 = _dot(hn, w_ref[:, c * MXU_WIDTH:(c + 1) * MXU_WIDTH])
        if pending is not None:
            finish(*pending)
        pending = (c, x2)
    finish(*pending)


def _swa_proj(h, w, tabs, tm):
    n, d = h.shape
    q_heads, kv_heads = w['q_heads'], w['kv_heads']
    q_cols = q_heads * SWA_HEAD_DIM // LANES
    nt = tabs[0].shape[0] // tm
    row = lambda i: (i, 0)
    hrow = lambda i: (0, i, 0)
    tspec = pl.BlockSpec((tm, LANES), lambda i: (i % nt, 0))
    kern = functools.partial(_swa_proj_kernel, q_cols=q_cols, kv_heads=kv_heads,
                             scale=SWA_HEAD_DIM ** -0.5 * LOG2E)
    return pl.pallas_call(
        kern,
        grid=(n // tm,),
        in_specs=[pl.BlockSpec((tm, d), row), _resident(w['g_attn'].shape),
                  _resident(w['w_qkv'].shape), _resident(w['g_q'].shape), _resident(w['g_k'].shape),
                  tspec, tspec, tspec, _resident(w['gmat'].shape)],
        out_specs=[pl.BlockSpec((tm, q_cols * LANES), row),
                   pl.BlockSpec((kv_heads, tm, LANES), hrow),
                   pl.BlockSpec((kv_heads, tm, LANES), hrow),
                   pl.BlockSpec((kv_heads, tm, SWA_HEAD_DIM), hrow),
                   pl.BlockSpec((kv_heads, tm, SWA_HEAD_DIM), hrow)],
        out_shape=[jax.ShapeDtypeStruct((n, q_cols * LANES), BF16),
                   jax.ShapeDtypeStruct((kv_heads, n, LANES), BF16),
                   jax.ShapeDtypeStruct((kv_heads, n, LANES), BF16),
                   jax.ShapeDtypeStruct((kv_heads, n, SWA_HEAD_DIM), F32),
                   jax.ShapeDtypeStruct((kv_heads, n, SWA_HEAD_DIM), F32)],
        compiler_params=_params("parallel"),
        name="swa_proj",
    )(h, w['g_attn'], w['w_qkv'], w['g_q'], w['g_k'], *tabs, w['gmat'])


def _swa_attn_kernel(sink_ref, q_ref, kp_ref, kc_ref, vp_ref, vc_ref, o_ref, *,
                     kv_heads, group, banded):
    tq = q_ref.shape[0]
    n_p, n_c = kp_ref.shape[1], kc_ref.shape[1]
    n_k = n_p + n_c
    lane = lax.broadcasted_iota(jnp.int32, (1, LANES), 1)
    row = lax.broadcasted_iota(jnp.int32, (LANES, 1), 0)
    if banded:
        assert n_p == WINDOW == LANES
        tiles = [(t * LANES, LANES, t * LANES, 2 * LANES) for t in range(tq // LANES)]
    else:
        tiles = [(0, tq, 0, n_k)]
    biases = []
    for q0, qn, k0, kn in tiles:
        if not banded:
            biases.append(None)
            continue
        win = WINDOW // CHUNK
        kc = (lax.broadcasted_iota(jnp.int32, (kn, qn), 0) + k0) // CHUNK - n_p // CHUNK
        qc = (lax.broadcasted_iota(jnp.int32, (kn, qn), 1) + q0) // CHUNK
        ok = (kc <= qc) & (kc >= qc - win) & ((kc >= 0) | (pl.program_id(1) > 0))
        biases.append(jnp.where(ok, 0.0, NEG).astype(F32))
    zero = jnp.zeros((), BF16)
    half = group // 2
    for kh in range(kv_heads):
        k_all = jnp.concatenate([kp_ref[kh], kc_ref[kh]], axis=0)
        v_t = jnp.concatenate([vp_ref[kh], vc_ref[kh]], axis=0).T
        ks = (jnp.where(lane < SWA_HEAD_DIM, k_all, zero), jnp.where(lane < SWA_HEAD_DIM, zero, k_all))
        vs = (jnp.where(row < SWA_HEAD_DIM, v_t, zero), jnp.where(row < SWA_HEAD_DIM, zero, v_t))
        cols = [q_ref[:, (kh * half + c) * LANES:(kh * half + c + 1) * LANES] for c in range(half)]
        for (q0, qn, k0, kn), bias in zip(tiles, biases):
            ss = [_dot_t(ks[u][k0:k0 + kn], cols[c][q0:q0 + qn])
                  for c in range(half) for u in range(2)]
            es, rdens = [], []
            for c in range(half):
                for u in range(2):
                    s = ss[2 * c + u]
                    if bias is not None:
                        s = s + bias
                    sink = sink_ref[2 * (kh * half + c) + u] * LOG2E
                    m = jnp.maximum(jnp.max(s, axis=0, keepdims=True), sink)
                    e = jnp.exp2(s - m)
                    rdens.append(1.0 / (jnp.sum(e, axis=0, keepdims=True) + jnp.exp2(sink - m)))
                    es.append(e.astype(BF16))
            for c in range(half):
                o_t = (_dot(vs[0][:, k0:k0 + kn], es[2 * c]) * rdens[2 * c]
                       + _dot(vs[1][:, k0:k0 + kn], es[2 * c + 1]) * rdens[2 * c + 1])
                col = kh * half + c
                o_ref[q0:q0 + qn, col * LANES:(col + 1) * LANES] = o_t.T.astype(BF16)


def _swa_attn(sinks, q, kd, vd, batch, seq, tq):
    n, dq = q.shape
    kv_heads = kd.shape[0]
    group = (dq // SWA_HEAD_DIM) // kv_heads
    nq = seq // tq
    per = tq // WINDOW
    prev = lambda b, i: (0, jnp.maximum(i * per - 1, 0) + b * nq * per, 0)
    cur = lambda b, i: (0, b * nq + i, 0)
    return pl.pallas_call(
        functools.partial(_swa_attn_kernel, kv_heads=kv_heads, group=group, banded=True),
        grid=(batch, nq),
        in_specs=[pl.BlockSpec(memory_space=pltpu.SMEM),
                  pl.BlockSpec((tq, dq), lambda b, i: (b * nq + i, 0)),
                  pl.BlockSpec((kv_heads, WINDOW, LANES), prev),
                  pl.BlockSpec((kv_heads, tq, LANES), cur),
                  pl.BlockSpec((kv_heads, WINDOW, LANES), prev),
                  pl.BlockSpec((kv_heads, tq, LANES), cur)],
        out_specs=pl.BlockSpec((tq, dq), lambda b, i: (b * nq + i, 0)),
        out_shape=jax.ShapeDtypeStruct((n, dq), BF16),
        compiler_params=_params("parallel", "arbitrary"),
        name="swa_attn",
    )(sinks, q, kd, kd, vd, vd)


def _swa_attn_sample(sinks, q, kd_cache, kd_new, vd_cache, vd_new, batch, keep, t):
    n, dq = q.shape
    kv_heads = kd_new.shape[0]
    group = (dq // SWA_HEAD_DIM) // kv_heads
    blk = lambda b: (0, b, 0)
    return pl.pallas_call(
        functools.partial(_swa_attn_kernel, kv_heads=kv_heads, group=group, banded=False),
        grid=(batch,),
        in_specs=[pl.BlockSpec(memory_space=pltpu.SMEM),
                  pl.BlockSpec((t, dq), lambda b: (b, 0)),
                  pl.BlockSpec((kv_heads, keep, LANES), blk),
                  pl.BlockSpec((kv_heads, t, LANES), blk),
                  pl.BlockSpec((kv_heads, keep, LANES), blk),
                  pl.BlockSpec((kv_heads, t, LANES), blk)],
        out_specs=pl.BlockSpec((t, dq), lambda b: (b, 0)),
        out_shape=jax.ShapeDtypeStruct((n, dq), BF16),
        compiler_params=_params("parallel"),
        name="swa_attn_sample",
    )(sinks, q, kd_cache, kd_new, vd_cache, vd_new)


def _oproj_kernel(h_ref, o_ref, w_ref, out_ref):
    out_ref[...] = h_ref[...] + _dot(o_ref[...], w_ref[...])


def _oproj(h, o, w, layer, tm):
    n, d = h.shape
    row = lambda i: (i, 0)
    return pl.pallas_call(
        _oproj_kernel,
        grid=(n // tm,),
        in_specs=[pl.BlockSpec((tm, d), row), pl.BlockSpec((tm, o.shape[1]), row),
                  _resident(w.shape, layer)],
        out_specs=pl.BlockSpec((tm, d), row),
        out_shape=jax.ShapeDtypeStruct((n, d), F32),
        compiler_params=_params("parallel"),
        name="attn_out_proj",
    )(h, o, w)


def _ffn_kernel(h_ref, g_ref, wg_ref, wu_ref, wd_ref, out_ref, hn_ref):
    @pl.when(pl.program_id(1) == 0)
    def _():
        x = h_ref[...]
        hn_ref[...] = _rms(x, g_ref[...]).astype(BF16)
        out_ref[...] = x

    hn = hn_ref[...]
    w = wg_ref.shape[1] // 2
    acts = []
    for c in range(2):
        a = _dot(hn, wg_ref[:, c * w:(c + 1) * w])
        b = _dot(hn, wu_ref[:, c * w:(c + 1) * w])
        acts.append((a * jax.nn.sigmoid(a) * b).astype(BF16))
    out_ref[...] += _dot(acts[0], wd_ref[0:w, :]) + _dot(acts[1], wd_ref[w:2 * w, :])


def _ffn(h, g, wg, wu, wd, layer, tm, tf):
    n, d = h.shape
    f = wg.shape[2]
    row = lambda i, j: (i, 0)
    return pl.pallas_call(
        _ffn_kernel,
        grid=(n // tm, f // tf),
        in_specs=[pl.BlockSpec((tm, d), row), pl.BlockSpec((None, 1, d), lambda i, j: (layer, 0, 0)),
                  pl.BlockSpec((None, d, tf), lambda i, j: (layer, 0, j)),
                  pl.BlockSpec((None, d, tf), lambda i, j: (layer, 0, j)),
                  pl.BlockSpec((None, tf, d), lambda i, j: (layer, j, 0))],
        out_specs=pl.BlockSpec((tm, d), row),
        out_shape=jax.ShapeDtypeStruct((n, d), F32),
        scratch_shapes=[pltpu.VMEM((tm, d), BF16)],
        compiler_params=_params("parallel", "arbitrary"),
        name="swiglu_ffn",
    )(h, g, wg, wu, wd)


def _ple_kernel(h_ref, g_ref, wgate_ref, p_ref, wproj_ref, gfin_ref, *rest, final, n_cast, stack):
    n_rows = sum(stack)
    cast_in, rows_in = rest[:n_cast], rest[n_cast:n_cast + n_rows]
    out_ref = rest[n_cast + n_rows]
    cast_out = rest[n_cast + n_rows + 1:2 * n_cast + n_rows + 1]
    stacked = rest[2 * n_cast + n_rows + 1:]
    first = 0
    for dst, count in zip(stacked, stack):
        for s in range(count):
            dst[s] = rows_in[first + s][...]
        first += count
    x = h_ref[...]
    gate = jax.nn.sigmoid(_dot(_rms(x, g_ref[...]).astype(BF16), wgate_ref[...]))
    y = x + gate * _dot(p_ref[...].astype(BF16), wproj_ref[...])
    if final:
        y = _rms(y, gfin_ref[...])
    out_ref[...] = y
    _cast_chunks(cast_in, cast_out)


def _ple(h, g, wgate, p, layer, wproj, gfin, final, tm, cast=(), cast_layer=0, stack=()):
    n, d = h.shape
    steps = n // tm
    row = lambda i: (i, 0)
    items = [_cast_item(a, steps, cast_layer) for a in cast]
    cast_specs, cast_out_specs, cast_shapes = _cast_specs(items, lambda i: i)
    rows = [a for group in stack for a in group]
    row_specs = [pl.BlockSpec((tm, a.shape[1]), row) for a in rows]
    stack_specs = [pl.BlockSpec((len(gr), tm, gr[0].shape[1]), lambda i: (0, i, 0)) for gr in stack]
    stack_shapes = [jax.ShapeDtypeStruct((len(gr), n, gr[0].shape[1]), gr[0].dtype) for gr in stack]
    out = pl.pallas_call(
        functools.partial(_ple_kernel, final=final, n_cast=len(cast), stack=tuple(len(gr) for gr in stack)),
        grid=(steps,),
        in_specs=[pl.BlockSpec((tm, d), row), _resident(g.shape, layer), _resident(wgate.shape, layer),
                  pl.BlockSpec((None, tm, p.shape[2]), lambda i: (layer, i, 0)),
                  _resident(wproj.shape, layer), _resident(gfin.shape)] + cast_specs + row_specs,
        out_specs=[pl.BlockSpec((tm, d), row)] + cast_out_specs + stack_specs,
        out_shape=[jax.ShapeDtypeStruct((n, d), F32)] + cast_shapes + stack_shapes,
        compiler_params=_params("parallel"),
        name="ple_embed",
    )(h, g, wgate, p, wproj, gfin, *cast, *rows)
    n_c = len(cast)
    return out[0], tuple(out[1:1 + n_c]), tuple(out[1 + n_c:])


def _angles(pos, r, theta):
    inv = np.power(np.float64(theta), -np.arange(0, r, 2, dtype=np.float64) / r)
    ang = pos.astype(np.float64)[:, None] * inv[None, :]
    return np.cos(ang).astype(np.float32), np.sin(ang).astype(np.float32)


def _mla_table(pos):
    cos, sin = _angles(pos, MLA_ROPE, MLA_THETA)
    return np.concatenate([cos, cos, -sin, sin], axis=-1)


def _swa_tables(pos):
    cos, sin = _angles(pos, SWA_ROT, SWA_THETA)
    s = cos.shape[0]
    rest = SWA_HEAD_DIM - SWA_ROT
    one = np.ones((s, rest), np.float32)
    zero = np.zeros((s, rest), np.float32)
    zh = np.zeros_like(sin)
    tc = np.concatenate([cos, cos, one], axis=-1)
    ts1 = np.concatenate([-sin, zh, zero], axis=-1)
    ts2 = np.concatenate([zh, sin, zero], axis=-1)
    return tuple(np.concatenate([t, t], axis=-1) for t in (tc, ts1, ts2))


def _tile_rows(tab, rows):
    tab = tab if tab.shape[0] >= rows else np.tile(tab, (rows // tab.shape[0], 1))
    return jnp.asarray(tab)


def _swap_halves(x):
    half = x.shape[-1] // 2
    return jnp.concatenate([x[..., half:], x[..., :half]], axis=-1)


def _prep_mla(j, g_attn_norm_i, w_mla_in, g_mla_q_a, w_mla_q_up, g_mla_kv_a, w_mla_kv_up, g_mla_q_nope,
              g_mla_q_rope, g_mla_k_nope, g_mla_k_rope):
    q_lora, kv_lora = g_mla_q_a.shape[1], g_mla_kv_a.shape[1]
    qk = MLA_NOPE + MLA_ROPE
    heads = w_mla_q_up.shape[2] // qk
    w_in = w_mla_in[j]
    k_r = w_in[:, q_lora + kv_lora:]
    w_in = jnp.concatenate([w_in, _swap_halves(k_r)], axis=-1)
    wq = w_mla_q_up[j].reshape(q_lora, heads, qk)
    rope = wq[:, :, MLA_NOPE:]
    wq = jnp.concatenate([wq[:, :, :MLA_NOPE].reshape(q_lora, heads // 2, 2 * MLA_NOPE),
                          jnp.concatenate([rope, _swap_halves(rope)], axis=-1).reshape(q_lora, heads // 2, 2 * LANES)],
                         axis=-1).reshape(q_lora, heads * 2 * LANES)
    wkv = w_mla_kv_up[j].reshape(kv_lora, heads, MLA_NOPE + MLA_V)
    dup = lambda g: jnp.concatenate([g, _swap_halves(g)], axis=-1)[None, :]
    two = lambda g: jnp.concatenate([g, g], axis=-1)
    gsum = np.kron(np.eye(2), np.ones((LANES, LANES)))
    return dict(
        heads=heads, gsum=jnp.asarray(gsum, BF16),
        g_attn=g_attn_norm_i[None, :],
        w_in=w_in.astype(BF16),
        g_q_a=g_mla_q_a[j][None, :], g_kv_a=g_mla_kv_a[j][None, :],
        g_k_rope=dup(g_mla_k_rope[j]), g_q_rope=two(dup(g_mla_q_rope[j])),
        g_q_nope=two(g_mla_q_nope[j][None, :]), g_k_nope=two(g_mla_k_nope[j][None, :]),
        w_q=wq.astype(BF16),
        w_kn=wkv[:, :, :MLA_NOPE].reshape(kv_lora, heads * MLA_NOPE).astype(BF16),
        w_v=wkv[:, :, MLA_NOPE:].reshape(kv_lora, heads * MLA_V).astype(BF16),
    )


def _prep_swa(j, g_attn_norm_i, w_swa_qkv, g_swa_q, g_swa_k, swa_sinks):
    q_heads = swa_sinks.shape[1]
    w = w_swa_qkv[j]
    kv_heads = (w.shape[1] - q_heads * SWA_HEAD_DIM) // (2 * SWA_HEAD_DIM)
    assert kv_heads % 4 == 0 and q_heads % 4 == 0
    gmat = np.kron(np.eye(2 * LANES // SWA_HEAD_DIM), np.ones((SWA_HEAD_DIM, SWA_HEAD_DIM)))
    dup = lambda g: jnp.concatenate([g, g], axis=-1)[None, :]
    return dict(
        q_heads=q_heads, kv_heads=kv_heads,
        g_attn=g_attn_norm_i[None, :],
        w_qkv=w.astype(BF16),
        g_q=dup(g_swa_q[j]), g_k=dup(g_swa_k[j]),
        gmat=jnp.asarray(gmat, BF16),
        sinks=swa_sinks[j],
    )


def _kv_dup(t):
    b, l, kv, d = t.shape
    t = jnp.transpose(t, (2, 0, 1, 3)).reshape(kv, b * l, d)
    return jnp.concatenate([t, t], axis=-1).astype(BF16)


def _first_casts(shared, steps):
    raw = shared['raw']

    def item(a, layer=None):
        split = next(s for s in (1, 2, 4, 8, 16)
                     if (a.shape[1] * s) % (steps * 16) == 0 and a.shape[2] % (s * LANES) == 0)
        return _cast_item(a, steps, layer, split)

    return ([item(a, 0) for a in raw['ffn']]
            + [item(raw[k]) for k in ('ple_gate', 'ple_proj', 'mla_out', 'swa_out')])


def _trunk(x, p, pos, caches, layers, shared, g_final, tm):
    batch, seq, d = x.shape
    n = batch * seq
    h = x.reshape(n, d)
    rows = max(tm, seq)
    mla_tab = _tile_rows(_mla_table(pos), rows)
    swa_tabs = tuple(_tile_rows(t, rows) for t in _swa_tables(pos))
    depth = len(layers)
    n_mla = sum(kind == 'mla' for kind, _ in layers)
    p_rows = p.reshape(depth, n, -1)
    lats, krs, sks, svs = [], [], [], []
    for i, (kind, w) in enumerate(layers):
        j = i // 2
        if kind == 'mla':
            lat, kr, q, k, v = _mla_proj(h, w, mla_tab, tm)
            lats.append(lat)
            krs.append(kr)
            if caches is None:
                tq, hb = MLA_Q, MLA_HEADS_PER_STEP
                first = 'ple' not in shared
                cast = _first_casts(shared, _mla_attn_steps(batch, w['heads'], seq, tq, hb)) if first else ()
                o, casted = _mla_attn(q, k, v, batch, seq, tq=tq, hb=hb, cast=cast)
                if first:
                    shared['ffn'] = {0: casted[:3]}
                    shared['ple'] = casted[3:5]
                    shared['out'] = dict(mla=casted[5], swa=casted[6])
            else:
                lat_c, kr_c = caches[0], caches[1]
                past = lat_c.shape[2]
                k_c, v_c = _mla_expand_call(lat_c.reshape(lat_c.shape[0], batch * past, -1),
                                            kr_c.reshape(kr_c.shape[0], batch * past, -1), j, w, tm=ROWS)
                o = _mla_attn_sample(q, k_c, k, v_c, v, batch, past, seq, hb=w['heads'])
        else:
            q, kd, vd, kf, vf = _swa_proj(h, w, swa_tabs, tm)
            kv_heads = w['kv_heads']
            unhead = lambda t, rows: jnp.transpose(
                t.reshape(kv_heads, batch, seq, SWA_HEAD_DIM)[:, :, seq - rows:], (1, 2, 0, 3))
            if caches is None:
                o = _swa_attn(w['sinks'], q, kd, vd, batch, seq, tq=SWA_Q)
                keep = min(WINDOW, seq)
                sks.append(unhead(kf, keep))
                svs.append(unhead(vf, keep))
            else:
                ck, cv = caches[2][j], caches[3][j]
                keep = ck.shape[1]
                o = _swa_attn_sample(w['sinks'], q, _kv_dup(ck), kd, _kv_dup(cv), vd, batch, keep, seq)
                sks.append(jnp.concatenate([ck, unhead(kf, seq)], axis=1)[:, -keep:])
                svs.append(jnp.concatenate([cv, unhead(vf, seq)], axis=1)[:, -keep:])
        h = _oproj(h, o, shared['out'][kind], j, tm)
        ffn_cast = shared['ffn']
        h = _ffn(h, shared['raw']['ffn_g'][i:i + 1], *ffn_cast[i], 0, min(n, ROWS_FFN), FFN_COLS)
        cast = shared['raw']['ffn'] if (i + 1 not in ffn_cast and i + 1 < depth) else ()
        last = i == depth - 1
        h, casted, stacked = _ple(h, shared['raw']['ple_g'], shared['ple'][0], p_rows, i, shared['ple'][1],
                                  g_final, last, tm, cast, i + 1, (lats, krs) if last else ())
        if casted:
            ffn_cast[i + 1] = casted
    lat, kr = (t.reshape(n_mla, batch, seq, -1) for t in stacked)
    return h.reshape(batch, seq, d), lat, kr, jnp.stack(sks), jnp.stack(svs)


def kernel(x_prompt, x_sample, p_prompt, p_sample, cache_mla_latent, cache_mla_krope, state_swa_k, state_swa_v, g_attn_norm, w_mla_in, g_mla_q_a, w_mla_q_up, g_mla_kv_a, w_mla_kv_up, g_mla_q_nope, g_mla_q_rope, g_mla_k_nope, g_mla_k_rope, w_mla_out, w_swa_qkv, g_swa_q, g_swa_k, swa_sinks, w_swa_out, g_ffn_norm, w_ffn_gate, w_ffn_up, w_ffn_down, g_ple_norm, w_ple_gate, w_ple_proj, g_final):
    depth = g_attn_norm.shape[0]
    layers = []
    for i in range(depth):
        j = i // 2
        if i % 2 == 0:
            layers.append(('mla', _prep_mla(j, g_attn_norm[i], w_mla_in, g_mla_q_a, w_mla_q_up,
                                            g_mla_kv_a, w_mla_kv_up, g_mla_q_nope, g_mla_q_rope,
                                            g_mla_k_nope, g_mla_k_rope)))
        else:
            layers.append(('swa', _prep_swa(j, g_attn_norm[i], w_swa_qkv, g_swa_q, g_swa_k, swa_sinks)))
    shared = dict(raw=dict(ffn=(w_ffn_gate, w_ffn_up, w_ffn_down), ffn_g=g_ffn_norm[:, None, :],
                           ple_gate=w_ple_gate, ple_proj=w_ple_proj, ple_g=g_ple_norm[:, None, :],
                           mla_out=w_mla_out, swa_out=w_swa_out))
    gfin = g_final[None, :]

    seq = x_prompt.shape[1]
    t = x_sample.shape[1]
    past = cache_mla_latent.shape[2]
    pos_p = np.arange(seq)
    pos_s = past + np.arange(t)
    y_p, lat_p, kr_p, sk_p, sv_p = _trunk(x_prompt, p_prompt, pos_p, None, layers, shared, gfin, tm=ROWS)
    caches = (cache_mla_latent, cache_mla_krope, state_swa_k, state_swa_v)
    n_s = x_sample.shape[0] * t
    y_s, lat_s, kr_s, sk_s, sv_s = _trunk(x_sample, p_sample, pos_s, caches, layers, shared, gfin,
                                          tm=min(n_s, ROWS))
    return (y_p, y_s, lat_p, kr_p, sk_p, sv_p, lat_s, kr_s, sk_s, sv_s)
```

```python
import functools

import numpy as np
import jax
import jax.numpy as jnp
from jax import lax
from jax.experimental import pallas as pl
from jax.experimental.pallas import tpu as pltpu

F32 = jnp.float32
BF16 = jnp.bfloat16

EPS = 1e-6
CHUNK = 64
WINDOW = 128
MLA_THETA = 10000.0
SWA_THETA = 500000.0
MLA_NOPE = 128
MLA_ROPE = 64
MLA_V = 128
SWA_HEAD_DIM = 64
SWA_ROT = SWA_HEAD_DIM // 4

LANES = 128
MXU_WIDTH = 256
VMEM_LIMIT = 56 * 1024 * 1024

ROWS = 512
ROWS_FFN = 1024
FFN_COLS = 512
MLA_Q = 512
MLA_HEADS_PER_STEP = 4
SWA_Q = 512
NEG = -1e30
LOG2E = 1.4426950408889634


def _params(*sem):
    return pltpu.CompilerParams(dimension_semantics=sem, vmem_limit_bytes=VMEM_LIMIT)


def _resident(shape, layer=None):
    if layer is None:
        zeros = (0,) * len(shape)
        return pl.BlockSpec(shape, lambda *_: zeros, pipeline_mode=pl.Buffered(1))
    index = (layer,) + (0,) * (len(shape) - 1)
    return pl.BlockSpec((None,) + tuple(shape[1:]), lambda *_: index, pipeline_mode=pl.Buffered(1))


def _cast_item(a, steps, layer=None, col_split=1):
    nl, r, c = a.shape
    assert (r * col_split) % (steps * 16) == 0 and c % (col_split * LANES) == 0, (a.shape, steps)
    lead, first = (nl, 0) if layer is None else (1, layer)
    return dict(a=a, block=(lead, r * col_split // steps, c // col_split),
                in_idx=lambda s: (first, s // col_split, s % col_split),
                out_idx=lambda s: (0, s // col_split, s % col_split),
                out_shape=jax.ShapeDtypeStruct((lead, r, c), BF16))


def _cast_specs(items, step_of):
    ins = [pl.BlockSpec(it['block'], lambda *g, it=it: it['in_idx'](step_of(*g))) for it in items]
    outs = [pl.BlockSpec(it['block'], lambda *g, it=it: it['out_idx'](step_of(*g))) for it in items]
    return ins, outs, [it['out_shape'] for it in items]


def _cast_chunks(srcs, dsts):
    for src, dst in zip(srcs, dsts):
        dst[...] = src[...].astype(BF16)


def _rms(x, g):
    return x * lax.rsqrt(jnp.mean(x * x, axis=-1, keepdims=True) + EPS) * g


def _dot(a, b):
    return jnp.dot(a, b, preferred_element_type=F32)


def _dot_t(a, b):
    return lax.dot_general(a, b, (((1,), (1,)), ((), ())), preferred_element_type=F32)


def _rope_dup(x, g, tab):
    ss = jnp.sum(x * x, axis=-1, keepdims=True)
    y = x * lax.rsqrt(ss * (1.0 / LANES) + EPS) * g
    t = y * tab
    return t + pltpu.roll(t, LANES // 2, axis=1)


def _mla_heads(c_q, latb, krb, tab, wq_ref, wkn_ref, wv_ref, gqn, gqr, gkn, gsum_ref,
               q_ref, k_ref, v_ref, heads):
    gsum = gsum_ref[...]
    pair = 2 * LANES

    def inv_rms(x):
        return lax.rsqrt(_dot((x * x).astype(BF16), gsum) * (1.0 / LANES) + EPS)

    def finish_kv(p, xk, xv):
        kn = jnp.concatenate([_rms(xk[:, :LANES], gkn[:, :LANES]),
                              _rms(xk[:, LANES:], gkn[:, LANES:])], axis=1)
        for u in range(2):
            h = 2 * p + u
            sl = slice(u * LANES, (u + 1) * LANES)
            k_ref[h, :, 0:MLA_NOPE] = kn[:, sl].astype(BF16)
            k_ref[h, :, MLA_NOPE:MLA_NOPE + MLA_ROPE] = krb
            v_ref[h] = xv[:, sl].astype(BF16)

    def finish_q(p, xq):
        qn = xq[:, :pair] * inv_rms(xq[:, :pair]) * gqn
        t = xq[:, pair:] * inv_rms(xq[:, pair:]) * gqr * tab
        for u in range(2):
            h = 2 * p + u
            sl = slice(u * LANES, (u + 1) * LANES)
            q_ref[h, :, 0:MLA_NOPE] = qn[:, sl].astype(BF16)
            tu = t[:, sl]
            qr = tu + pltpu.roll(tu, LANES // 2, axis=1)
            q_ref[h, :, MLA_NOPE:MLA_NOPE + MLA_ROPE] = qr[:, :MLA_ROPE].astype(BF16)

    n_pairs = heads // 2
    dot_q = lambda p: _dot(c_q, wq_ref[:, p * 2 * pair:(p + 1) * 2 * pair])
    xq = None if c_q is None else dot_q(0)
    pending = None
    for p in range(n_pairs):
        xk = _dot(latb, wkn_ref[:, p * pair:(p + 1) * pair])
        xv = _dot(latb, wv_ref[:, p * pair:(p + 1) * pair])
        xq_next = dot_q(p + 1) if (c_q is not None and p + 1 < n_pairs) else None
        if xq is not None:
            finish_q(p, xq)
        if pending is not None:
            finish_kv(*pending)
        pending, xq = (p, xk, xv), xq_next
    finish_kv(*pending)


def _mla_proj_kernel(h_ref, gattn_ref, win_ref, gqa_ref, gkva_ref, gkr_ref, tab_ref, wq_ref,
                     gqn_ref, gqr_ref, wkn_ref, wv_ref, gkn_ref, gsum_ref,
                     lat_ref, kr_ref, q_ref, k_ref, v_ref, *, heads, q_lora, kv_lora, scale):
    hn = _rms(h_ref[...], gattn_ref[...]).astype(BF16)
    a = _dot(hn, win_ref[...])
    c_q = _rms(a[:, :q_lora], gqa_ref[...]).astype(BF16)
    lat = _rms(a[:, q_lora:q_lora + kv_lora], gkva_ref[...])
    lat_ref[...] = lat
    tab = tab_ref[...]
    kr = _rope_dup(a[:, q_lora + kv_lora:], gkr_ref[...], tab)
    kr_ref[...] = kr[:, :MLA_ROPE]
    _mla_heads(c_q, lat.astype(BF16), kr[:, :MLA_ROPE].astype(BF16), jnp.concatenate([tab, tab], axis=1),
               wq_ref, wkn_ref, wv_ref, gqn_ref[...] * scale, gqr_ref[...] * scale, gkn_ref[...],
               gsum_ref, q_ref, k_ref, v_ref, heads)


def _mla_expand_kernel(lat_ref, kr_ref, wkn_ref, wv_ref, gkn_ref, gsum_ref, k_ref, v_ref, *, heads):
    _mla_heads(None, lat_ref[...].astype(BF16), kr_ref[...].astype(BF16), None, None, wkn_ref, wv_ref,
               None, None, gkn_ref[...], gsum_ref, None, k_ref, v_ref, heads)


def _mla_proj(h, w, tab, tm):
    n, d = h.shape
    heads = w['heads']
    q_lora, kv_lora = w['g_q_a'].shape[1], w['g_kv_a'].shape[1]
    qk = MLA_NOPE + MLA_ROPE
    nt = tab.shape[0] // tm
    row = lambda i: (i, 0)
    hrow = lambda i: (0, i, 0)
    kern = functools.partial(_mla_proj_kernel, heads=heads, q_lora=q_lora, kv_lora=kv_lora,
                             scale=qk ** -0.5 * LOG2E)
    weights = (w['g_attn'], w['w_in'], w['g_q_a'], w['g_kv_a'], w['g_k_rope'], tab, w['w_q'],
               w['g_q_nope'], w['g_q_rope'], w['w_kn'], w['w_v'], w['g_k_nope'], w['gsum'])
    return pl.pallas_call(
        kern,
        grid=(n // tm,),
        in_specs=[
            pl.BlockSpec((tm, d), row),
            _resident(w['g_attn'].shape), _resident(w['w_in'].shape),
            _resident(w['g_q_a'].shape), _resident(w['g_kv_a'].shape), _resident(w['g_k_rope'].shape),
            pl.BlockSpec((tm, LANES), lambda i: (i % nt, 0)),
            _resident(w['w_q'].shape), _resident(w['g_q_nope'].shape), _resident(w['g_q_rope'].shape),
            _resident(w['w_kn'].shape), _resident(w['w_v'].shape), _resident(w['g_k_nope'].shape),
            _resident(w['gsum'].shape),
        ],
        out_specs=[
            pl.BlockSpec((tm, kv_lora), row),
            pl.BlockSpec((tm, MLA_ROPE), row),
            pl.BlockSpec((heads, tm, qk), hrow),
            pl.BlockSpec((heads, tm, qk), hrow),
            pl.BlockSpec((heads, tm, MLA_V), hrow),
        ],
        out_shape=[
            jax.ShapeDtypeStruct((n, kv_lora), F32),
            jax.ShapeDtypeStruct((n, MLA_ROPE), F32),
            jax.ShapeDtypeStruct((heads, n, qk), BF16),
            jax.ShapeDtypeStruct((heads, n, qk), BF16),
            jax.ShapeDtypeStruct((heads, n, MLA_V), BF16),
        ],
        compiler_params=_params("parallel"),
        name="mla_proj",
    )(h, *weights)


def _mla_expand_call(lat, kr, layer, w, tm):
    _, n, kv_lora = lat.shape
    heads = w['heads']
    qk = MLA_NOPE + MLA_ROPE
    hrow = lambda i: (0, i, 0)
    return pl.pallas_call(
        functools.partial(_mla_expand_kernel, heads=heads),
        grid=(n // tm,),
        in_specs=[pl.BlockSpec((None, tm, kv_lora), lambda i: (layer, i, 0)),
                  pl.BlockSpec((None, tm, MLA_ROPE), lambda i: (layer, i, 0)),
                  _resident(w['w_kn'].shape), _resident(w['w_v'].shape),
                  _resident(w['g_k_nope'].shape), _resident(w['gsum'].shape)],
        out_specs=[pl.BlockSpec((heads, tm, qk), hrow), pl.BlockSpec((heads, tm, MLA_V), hrow)],
        out_shape=[jax.ShapeDtypeStruct((heads, n, qk), BF16),
                   jax.ShapeDtypeStruct((heads, n, MLA_V), BF16)],
        compiler_params=_params("parallel"),
        name="mla_expand",
    )(lat, kr, w['w_kn'], w['w_v'], w['g_k_nope'], w['gsum'])


def _mla_attn_kernel(q_ref, k_ref, v_ref, *rest, hb, tq, n_cast):
    cast_in, o_ref, cast_out, s_ref = rest[:n_cast], rest[n_cast], rest[n_cast + 1:-1], rest[-1]
    _cast_chunks(cast_in, cast_out)
    i = pl.program_id(2)
    kc = lax.broadcasted_iota(jnp.int32, (tq, tq), 0) // CHUNK
    qc = lax.broadcasted_iota(jnp.int32, (tq, tq), 1) // CHUNK
    diag_bias = jnp.where(kc <= qc, 0.0, NEG).astype(F32)
    qs = [q_ref[h] for h in range(hb)]

    def scores(j, slot):
        start = pl.multiple_of(j * tq, tq)
        mx = []
        for h in range(hb):
            s = _dot_t(k_ref[h, pl.ds(start, tq), :], qs[h])
            s_ref[slot, h] = s
            mx.append(jnp.max(s, axis=0, keepdims=True))
        return tuple(mx)

    def update(j, slot, carry, mx, bias=None):
        start = pl.multiple_of(j * tq, tq)
        ps, stats = [], []
        for h in range(hb):
            m, l, _ = carry[h]
            if bias is None:
                s, blk_max = s_ref[slot, h], mx[h]
            else:
                s = s_ref[slot, h] + bias
                blk_max = jnp.max(s, axis=0, keepdims=True)
            m_new = jnp.maximum(m, blk_max)
            alpha = jnp.exp2(m - m_new)
            p = jnp.exp2(s - m_new)
            stats.append((m_new, alpha * l + jnp.sum(p, axis=0, keepdims=True), alpha))
            ps.append(p.astype(BF16))
        out = []
        for h in range(hb):
            pv = lax.dot_general(v_ref[h, pl.ds(start, tq), :], ps[h],
                                 (((0,), (0,)), ((), ())), preferred_element_type=F32)
            m_new, l, alpha = stats[h]
            out.append((m_new, l, alpha * carry[h][2] + pv))
        return tuple(out)

    def finish(carry):
        for h in range(hb):
            _, l, acc = carry[h]
            o_ref[:, h * MLA_V:(h + 1) * MLA_V] = (acc / l).T.astype(BF16)

    def pair(t, state):
        carry, mx0 = state
        mx1 = scores(2 * t + 1, 1)
        carry = update(2 * t, 0, carry, mx0)
        mx0 = scores(2 * t + 2, 0)
        return update(2 * t + 1, 1, carry, mx1), mx0

    init = tuple((jnp.full((1, tq), NEG, F32), jnp.zeros((1, tq), F32), jnp.zeros((MLA_V, tq), F32))
                 for _ in range(hb))
    carry, mx0 = lax.fori_loop(0, i // 2, pair, (init, scores(0, 0)))

    @pl.when(i % 2 == 0)
    def _():
        finish(update(i, 0, carry, None, diag_bias))

    @pl.when(i % 2 == 1)
    def _():
        scores(i, 1)
        finish(update(i, 1, update(i - 1, 0, carry, mx0), None, diag_bias))


def _mla_attn_steps(batch, heads, seq, tq, hb):
    return batch * (heads // hb) * (seq // tq)


def _mla_attn(q, k, v, batch, seq, tq, hb, cast=()):
    heads, n, qk = q.shape
    nq = seq // tq
    ng = heads // hb
    kv_map = lambda b, g, i: (g, b, 0)
    cast_in, cast_out, cast_shapes = _cast_specs(cast, lambda b, g, i: (b * ng + g) * nq + i)
    out = pl.pallas_call(
        functools.partial(_mla_attn_kernel, hb=hb, tq=tq, n_cast=len(cast)),
        grid=(batch, ng, nq),
        in_specs=[pl.BlockSpec((hb, tq, qk), lambda b, g, i: (g, b * nq + i, 0)),
                  pl.BlockSpec((hb, seq, qk), kv_map),
                  pl.BlockSpec((hb, seq, MLA_V), kv_map)] + cast_in,
        out_specs=[pl.BlockSpec((tq, hb * MLA_V), lambda b, g, i: (b * nq + i, g))] + cast_out,
        out_shape=[jax.ShapeDtypeStruct((n, heads * MLA_V), BF16)] + cast_shapes,
        scratch_shapes=[pltpu.VMEM((2, hb, tq, tq), F32)],
        compiler_params=_params("parallel", "parallel", "arbitrary"),
        name="mla_attn",
    )(q, k, v, *[it['a'] for it in cast])
    return out[0], tuple(out[1:])


def _mla_attn_sample_kernel(q_ref, kc_ref, kn_ref, vc_ref, vn_ref, o_ref, *, hb):
    ss = [(_dot_t(kc_ref[h], q_ref[h]), _dot_t(kn_ref[h], q_ref[h])) for h in range(hb)]
    ps = []
    for s_c, s_n in ss:
        m = jnp.maximum(jnp.max(s_c, axis=0, keepdims=True), jnp.max(s_n, axis=0, keepdims=True))
        p_c = jnp.exp2(s_c - m)
        p_n = jnp.exp2(s_n - m)
        l = jnp.sum(p_c, axis=0, keepdims=True) + jnp.sum(p_n, axis=0, keepdims=True)
        ps.append((p_c.astype(BF16), p_n.astype(BF16), l))
    tn = (((0,), (0,)), ((), ()))
    for h, (p_c, p_n, l) in enumerate(ps):
        acc = (lax.dot_general(vc_ref[h], p_c, tn, preferred_element_type=F32)
               + lax.dot_general(vn_ref[h], p_n, tn, preferred_element_type=F32))
        o_ref[:, h * MLA_V:(h + 1) * MLA_V] = (acc / l).T.astype(BF16)


def _mla_attn_sample(q, k_cache, k_new, v_cache, v_new, batch, past, t, hb):
    heads, n, qk = q.shape
    q_pos = past + np.arange(t)
    k_pos = np.arange(past + t)
    assert np.all((k_pos // CHUNK)[None, :] <= (q_pos // CHUNK)[:, None])
    m3 = lambda b, g: (g, b, 0)
    return pl.pallas_call(
        functools.partial(_mla_attn_sample_kernel, hb=hb),
        grid=(batch, heads // hb),
        in_specs=[pl.BlockSpec((hb, t, qk), m3), pl.BlockSpec((hb, past, qk), m3),
                  pl.BlockSpec((hb, t, qk), m3), pl.BlockSpec((hb, past, MLA_V), m3),
                  pl.BlockSpec((hb, t, MLA_V), m3)],
        out_specs=pl.BlockSpec((t, hb * MLA_V), lambda b, g: (b, g)),
        out_shape=jax.ShapeDtypeStruct((n, heads * MLA_V), BF16),
        compiler_params=_params("parallel", "parallel"),
        name="mla_attn_sample",
    )(q, k_cache, k_new, v_cache, v_new)


def _swa_proj_kernel(h_ref, gattn_ref, w_ref, gq_ref, gk_ref, tc_ref, ts1_ref, ts2_ref, gmat_ref,
                     q_ref, kd_ref, vd_ref, kf_ref, vf_ref, *, q_cols, kv_heads, scale):
    hn = _rms(h_ref[...], gattn_ref[...]).astype(BF16)
    tc, ts1, ts2 = tc_ref[...], ts1_ref[...], ts2_ref[...]
    gmat = gmat_ref[...]

    def rope(x, ss, g):
        y = x * lax.rsqrt(ss * (1.0 / SWA_HEAD_DIM) + EPS) * g
        return (y * tc + pltpu.roll(y, LANES - SWA_ROT // 2, axis=1) * ts1
                + pltpu.roll(y, SWA_ROT // 2, axis=1) * ts2)

    gq = gq_ref[...] * scale
    gk = gk_ref[...]
    n_q = q_cols // 2
    n_kv = kv_heads // 4
    lo = lax.broadcasted_iota(jnp.int32, (1, LANES), 1) < SWA_HEAD_DIM

    def put_dup(col, first_head, dup_ref, flat_ref):
        swapped = pltpu.roll(col, SWA_HEAD_DIM, axis=1)
        for u, d in enumerate((jnp.where(lo, col, swapped), jnp.where(lo, swapped, col))):
            dup_ref[first_head + u] = d.astype(BF16)
            flat_ref[first_head + u] = d[:, :SWA_HEAD_DIM]

    def finish(c, x2):
        if c < n_q + n_kv:
            ss = _dot((x2 * x2).astype(BF16), gmat)
        for u in range(2):
            sl = slice(u * LANES, (u + 1) * LANES)
            if c < n_q:
                col = 2 * c + u
                q_ref[:, col * LANES:(col + 1) * LANES] = rope(x2[:, sl], ss[:, sl], gq).astype(BF16)
            elif c < n_q + n_kv:
                put_dup(rope(x2[:, sl], ss[:, sl], gk), 4 * (c - n_q) + 2 * u, kd_ref, kf_ref)
            else:
                put_dup(x2[:, sl], 4 * (c - n_q - n_kv) + 2 * u, vd_ref, vf_ref)

    pending = None
    for c in range(n_q + 2 * n_kv):
        x2 = _dot(hn, w_ref[:, c * MXU_WIDTH:(c + 1) * MXU_WIDTH])
        if pending is not None:
            finish(*pending)
        pending = (c, x2)
    finish(*pending)


def _swa_proj(h, w, tabs, tm):
    n, d = h.shape
    q_heads, kv_heads = w['q_heads'], w['kv_heads']
    q_cols = q_heads * SWA_HEAD_DIM // LANES
    nt = tabs[0].shape[0] // tm
    row = lambda i: (i, 0)
    hrow = lambda i: (0, i, 0)
    tspec = pl.BlockSpec((tm, LANES), lambda i: (i % nt, 0))
    kern = functools.partial(_swa_proj_kernel, q_cols=q_cols, kv_heads=kv_heads,
                             scale=SWA_HEAD_DIM ** -0.5 * LOG2E)
    return pl.pallas_call(
        kern,
        grid=(n // tm,),
        in_specs=[pl.BlockSpec((tm, d), row), _resident(w['g_attn'].shape),
                  _resident(w['w_qkv'].shape), _resident(w['g_q'].shape), _resident(w['g_k'].shape),
                  tspec, tspec, tspec, _resident(w['gmat'].shape)],
        out_specs=[pl.BlockSpec((tm, q_cols * LANES), row),
                   pl.BlockSpec((kv_heads, tm, LANES), hrow),
                   pl.BlockSpec((kv_heads, tm, LANES), hrow),
                   pl.BlockSpec((kv_heads, tm, SWA_HEAD_DIM), hrow),
                   pl.BlockSpec((kv_heads, tm, SWA_HEAD_DIM), hrow)],
        out_shape=[jax.ShapeDtypeStruct((n, q_cols * LANES), BF16),
                   jax.ShapeDtypeStruct((kv_heads, n, LANES), BF16),
                   jax.ShapeDtypeStruct((kv_heads, n, LANES), BF16),
                   jax.ShapeDtypeStruct((kv_heads, n, SWA_HEAD_DIM), F32),
                   jax.ShapeDtypeStruct((kv_heads, n, SWA_HEAD_DIM), F32)],
        compiler_params=_params("parallel"),
        name="swa_proj",
    )(h, w['g_attn'], w['w_qkv'], w['g_q'], w['g_k'], *tabs, w['gmat'])


def _swa_attn_kernel(sink_ref, q_ref, kp_ref, kc_ref, vp_ref, vc_ref, o_ref, *,
                     kv_heads, group, banded):
    tq = q_ref.shape[0]
    n_p, n_c = kp_ref.shape[1], kc_ref.shape[1]
    n_k = n_p + n_c
    lane = lax.broadcasted_iota(jnp.int32, (1, LANES), 1)
    row = lax.broadcasted_iota(jnp.int32, (LANES, 1), 0)
    if banded:
        assert n_p == WINDOW == LANES
        tiles = [(t * LANES, LANES, t * LANES, 2 * LANES) for t in range(tq // LANES)]
    else:
        tiles = [(0, tq, 0, n_k)]
    biases = []
    for q0, qn, k0, kn in tiles:
        if not banded:
            biases.append(None)
            continue
        win = WINDOW // CHUNK
        kc = (lax.broadcasted_iota(jnp.int32, (kn, qn), 0) + k0) // CHUNK - n_p // CHUNK
        qc = (lax.broadcasted_iota(jnp.int32, (kn, qn), 1) + q0) // CHUNK
        ok = (kc <= qc) & (kc >= qc - win) & ((kc >= 0) | (pl.program_id(1) > 0))
        biases.append(jnp.where(ok, 0.0, NEG).astype(F32))
    zero = jnp.zeros((), BF16)
    half = group // 2
    for kh in range(kv_heads):
        k_all = jnp.concatenate([kp_ref[kh], kc_ref[kh]], axis=0)
        v_t = jnp.concatenate([vp_ref[kh], vc_ref[kh]], axis=0).T
        ks = (jnp.where(lane < SWA_HEAD_DIM, k_all, zero), jnp.where(lane < SWA_HEAD_DIM, zero, k_all))
        vs = (jnp.where(row < SWA_HEAD_DIM, v_t, zero), jnp.where(row < SWA_HEAD_DIM, zero, v_t))
        cols = [q_ref[:, (kh * half + c) * LANES:(kh * half + c + 1) * LANES] for c in range(half)]
        for (q0, qn, k0, kn), bias in zip(tiles, biases):
            ss = [_dot_t(ks[u][k0:k0 + kn], cols[c][q0:q0 + qn])
                  for c in range(half) for u in range(2)]
            es, rdens = [], []
            for c in range(half):
                for u in range(2):
                    s = ss[2 * c + u]
                    if bias is not None:
                        s = s + bias
                    sink = sink_ref[2 * (kh * half + c) + u] * LOG2E
                    m = jnp.maximum(jnp.max(s, axis=0, keepdims=True), sink)
                    e = jnp.exp2(s - m)
                    rdens.append(1.0 / (jnp.sum(e, axis=0, keepdims=True) + jnp.exp2(sink - m)))
                    es.append(e.astype(BF16))
            for c in range(half):
                o_t = (_dot(vs[0][:, k0:k0 + kn], es[2 * c]) * rdens[2 * c]
                       + _dot(vs[1][:, k0:k0 + kn], es[2 * c + 1]) * rdens[2 * c + 1])
                col = kh * half + c
                o_ref[q0:q0 + qn, col * LANES:(col + 1) * LANES] = o_t.T.astype(BF16)


def _swa_attn(sinks, q, kd, vd, batch, seq, tq):
    n, dq = q.shape
    kv_heads = kd.shape[0]
    group = (dq // SWA_HEAD_DIM) // kv_heads
    nq = seq // tq
    per = tq // WINDOW
    prev = lambda b, i: (0, jnp.maximum(i * per - 1, 0) + b * nq * per, 0)
    cur = lambda b, i: (0, b * nq + i, 0)
    return pl.pallas_call(
        functools.partial(_swa_attn_kernel, kv_heads=kv_heads, group=group, banded=True),
        grid=(batch, nq),
        in_specs=[pl.BlockSpec(memory_space=pltpu.SMEM),
                  pl.BlockSpec((tq, dq), lambda b, i: (b * nq + i, 0)),
                  pl.BlockSpec((kv_heads, WINDOW, LANES), prev),
                  pl.BlockSpec((kv_heads, tq, LANES), cur),
                  pl.BlockSpec((kv_heads, WINDOW, LANES), prev),
                  pl.BlockSpec((kv_heads, tq, LANES), cur)],
        out_specs=pl.BlockSpec((tq, dq), lambda b, i: (b * nq + i, 0)),
        out_shape=jax.ShapeDtypeStruct((n, dq), BF16),
        compiler_params=_params("parallel", "arbitrary"),
        name="swa_attn",
    )(sinks, q, kd, kd, vd, vd)


def _swa_attn_sample(sinks, q, kd_cache, kd_new, vd_cache, vd_new, batch, keep, t):
    n, dq = q.shape
    kv_heads = kd_new.shape[0]
    group = (dq // SWA_HEAD_DIM) // kv_heads
    blk = lambda b: (0, b, 0)
    return pl.pallas_call(
        functools.partial(_swa_attn_kernel, kv_heads=kv_heads, group=group, banded=False),
        grid=(batch,),
        in_specs=[pl.BlockSpec(memory_space=pltpu.SMEM),
                  pl.BlockSpec((t, dq), lambda b: (b, 0)),
                  pl.BlockSpec((kv_heads, keep, LANES), blk),
                  pl.BlockSpec((kv_heads, t, LANES), blk),
                  pl.BlockSpec((kv_heads, keep, LANES), blk),
                  pl.BlockSpec((kv_heads, t, LANES), blk)],
        out_specs=pl.BlockSpec((t, dq), lambda b: (b, 0)),
        out_shape=jax.ShapeDtypeStruct((n, dq), BF16),
        compiler_params=_params("parallel"),
        name="swa_attn_sample",
    )(sinks, q, kd_cache, kd_new, vd_cache, vd_new)


def _oproj_kernel(h_ref, o_ref, w_ref, out_ref):
    out_ref[...] = h_ref[...] + _dot(o_ref[...], w_ref[...])


def _oproj(h, o, w, layer, tm):
    n, d = h.shape
    row = lambda i: (i, 0)
    return pl.pallas_call(
        _oproj_kernel,
        grid=(n // tm,),
        in_specs=[pl.BlockSpec((tm, d), row), pl.BlockSpec((tm, o.shape[1]), row),
                  _resident(w.shape, layer)],
        out_specs=pl.BlockSpec((tm, d), row),
        out_shape=jax.ShapeDtypeStruct((n, d), F32),
        compiler_params=_params("parallel"),
        name="attn_out_proj",
    )(h, o, w)


def _ffn_kernel(h_ref, g_ref, wg_ref, wu_ref, wd_ref, out_ref, hn_ref):
    @pl.when(pl.program_id(1) == 0)
    def _():
        x = h_ref[...]
        hn_ref[...] = _rms(x, g_ref[...]).astype(BF16)
        out_ref[...] = x

    hn = hn_ref[...]
    w = wg_ref.shape[1] // 2
    acts = []
    for c in range(2):
        a = _dot(hn, wg_ref[:, c * w:(c + 1) * w])
        b = _dot(hn, wu_ref[:, c * w:(c + 1) * w])
        acts.append((a * jax.nn.sigmoid(a) * b).astype(BF16))
    out_ref[...] += _dot(acts[0], wd_ref[0:w, :]) + _dot(acts[1], wd_ref[w:2 * w, :])


def _ffn(h, g, wg, wu, wd, layer, tm, tf):
    n, d = h.shape
    f = wg.shape[2]
    row = lambda i, j: (i, 0)
    return pl.pallas_call(
        _ffn_kernel,
        grid=(n // tm, f // tf),
        in_specs=[pl.BlockSpec((tm, d), row), pl.BlockSpec((None, 1, d), lambda i, j: (layer, 0, 0)),
                  pl.BlockSpec((None, d, tf), lambda i, j: (layer, 0, j)),
                  pl.BlockSpec((None, d, tf), lambda i, j: (layer, 0, j)),
                  pl.BlockSpec((None, tf, d), lambda i, j: (layer, j, 0))],
        out_specs=pl.BlockSpec((tm, d), row),
        out_shape=jax.ShapeDtypeStruct((n, d), F32),
        scratch_shapes=[pltpu.VMEM((tm, d), BF16)],
        compiler_params=_params("parallel", "arbitrary"),
        name="swiglu_ffn",
    )(h, g, wg, wu, wd)


def _ple_kernel(h_ref, g_ref, wgate_ref, p_ref, wproj_ref, gfin_ref, *rest, final, n_cast, stack):
    n_rows = sum(stack)
    cast_in, rows_in = rest[:n_cast], rest[n_cast:n_cast + n_rows]
    out_ref = rest[n_cast + n_rows]
    cast_out = rest[n_cast + n_rows + 1:2 * n_cast + n_rows + 1]
    stacked = rest[2 * n_cast + n_rows + 1:]
    first = 0
    for dst, count in zip(stacked, stack):
        for s in range(count):
            dst[s] = rows_in[first + s][...]
        first += count
    x = h_ref[...]
    gate = jax.nn.sigmoid(_dot(_rms(x, g_ref[...]).astype(BF16), wgate_ref[...]))
    y = x + gate * _dot(p_ref[...].astype(BF16), wproj_ref[...])
    if final:
        y = _rms(y, gfin_ref[...])
    out_ref[...] = y
    _cast_chunks(cast_in, cast_out)


def _ple(h, g, wgate, p, layer, wproj, gfin, final, tm, cast=(), cast_layer=0, stack=()):
    n, d = h.shape
    steps = n // tm
    row = lambda i: (i, 0)
    items = [_cast_item(a, steps, cast_layer) for a in cast]
    cast_specs, cast_out_specs, cast_shapes = _cast_specs(items, lambda i: i)
    rows = [a for group in stack for a in group]
    row_specs = [pl.BlockSpec((tm, a.shape[1]), row) for a in rows]
    stack_specs = [pl.BlockSpec((len(gr), tm, gr[0].shape[1]), lambda i: (0, i, 0)) for gr in stack]
    stack_shapes = [jax.ShapeDtypeStruct((len(gr), n, gr[0].shape[1]), gr[0].dtype) for gr in stack]
    out = pl.pallas_call(
        functools.partial(_ple_kernel, final=final, n_cast=len(cast), stack=tuple(len(gr) for gr in stack)),
        grid=(steps,),
        in_specs=[pl.BlockSpec((tm, d), row), _resident(g.shape, layer), _resident(wgate.shape, layer),
                  pl.BlockSpec((None, tm, p.shape[2]), lambda i: (layer, i, 0)),
                  _resident(wproj.shape, layer), _resident(gfin.shape)] + cast_specs + row_specs,
        out_specs=[pl.BlockSpec((tm, d), row)] + cast_out_specs + stack_specs,
        out_shape=[jax.ShapeDtypeStruct((n, d), F32)] + cast_shapes + stack_shapes,
        compiler_params=_params("parallel"),
        name="ple_embed",
    )(h, g, wgate, p, wproj, gfin, *cast, *rows)
    n_c = len(cast)
    return out[0], tuple(out[1:1 + n_c]), tuple(out[1 + n_c:])


def _angles(pos, r, theta):
    inv = np.power(np.float64(theta), -np.arange(0, r, 2, dtype=np.float64) / r)
    ang = pos.astype(np.float64)[:, None] * inv[None, :]
    return np.cos(ang).astype(np.float32), np.sin(ang).astype(np.float32)


def _mla_table(pos):
    cos, sin = _angles(pos, MLA_ROPE, MLA_THETA)
    return np.concatenate([cos, cos, -sin, sin], axis=-1)


def _swa_tables(pos):
    cos, sin = _angles(pos, SWA_ROT, SWA_THETA)
    s = cos.shape[0]
    rest = SWA_HEAD_DIM - SWA_ROT
    one = np.ones((s, rest), np.float32)
    zero = np.zeros((s, rest), np.float32)
    zh = np.zeros_like(sin)
    tc = np.concatenate([cos, cos, one], axis=-1)
    ts1 = np.concatenate([-sin, zh, zero], axis=-1)
    ts2 = np.concatenate([zh, sin, zero], axis=-1)
    return tuple(np.concatenate([t, t], axis=-1) for t in (tc, ts1, ts2))


def _tile_rows(tab, rows):
    tab = tab if tab.shape[0] >= rows else np.tile(tab, (rows // tab.shape[0], 1))
    return jnp.asarray(tab)


def _swap_halves(x):
    half = x.shape[-1] // 2
    return jnp.concatenate([x[..., half:], x[..., :half]], axis=-1)


def _prep_mla(j, g_attn_norm_i, w_mla_in, g_mla_q_a, w_mla_q_up, g_mla_kv_a, w_mla_kv_up, g_mla_q_nope,
              g_mla_q_rope, g_mla_k_nope, g_mla_k_rope):
    q_lora, kv_lora = g_mla_q_a.shape[1], g_mla_kv_a.shape[1]
    qk = MLA_NOPE + MLA_ROPE
    heads = w_mla_q_up.shape[2] // qk
    w_in = w_mla_in[j]
    k_r = w_in[:, q_lora + kv_lora:]
    w_in = jnp.concatenate([w_in, _swap_halves(k_r)], axis=-1)
    wq = w_mla_q_up[j].reshape(q_lora, heads, qk)
    rope = wq[:, :, MLA_NOPE:]
    wq = jnp.concatenate([wq[:, :, :MLA_NOPE].reshape(q_lora, heads // 2, 2 * MLA_NOPE),
                          jnp.concatenate([rope, _swap_halves(rope)], axis=-1).reshape(q_lora, heads // 2, 2 * LANES)],
                         axis=-1).reshape(q_lora, heads * 2 * LANES)
    wkv = w_mla_kv_up[j].reshape(kv_lora, heads, MLA_NOPE + MLA_V)
    dup = lambda g: jnp.concatenate([g, _swap_halves(g)], axis=-1)[None, :]
    two = lambda g: jnp.concatenate([g, g], axis=-1)
    gsum = np.kron(np.eye(2), np.ones((LANES, LANES)))
    return dict(
        heads=heads, gsum=jnp.asarray(gsum, BF16),
        g_attn=g_attn_norm_i[None, :],
        w_in=w_in.astype(BF16),
        g_q_a=g_mla_q_a[j][None, :], g_kv_a=g_mla_kv_a[j][None, :],
        g_k_rope=dup(g_mla_k_rope[j]), g_q_rope=two(dup(g_mla_q_rope[j])),
        g_q_nope=two(g_mla_q_nope[j][None, :]), g_k_nope=two(g_mla_k_nope[j][None, :]),
        w_q=wq.astype(BF16),
        w_kn=wkv[:, :, :MLA_NOPE].reshape(kv_lora, heads * MLA_NOPE).astype(BF16),
        w_v=wkv[:, :, MLA_NOPE:].reshape(kv_lora, heads * MLA_V).astype(BF16),
    )


def _prep_swa(j, g_attn_norm_i, w_swa_qkv, g_swa_q, g_swa_k, swa_sinks):
    q_heads = swa_sinks.shape[1]
    w = w_swa_qkv[j]
    kv_heads = (w.shape[1] - q_heads * SWA_HEAD_DIM) // (2 * SWA_HEAD_DIM)
    assert kv_heads % 4 == 0 and q_heads % 4 == 0
    gmat = np.kron(np.eye(2 * LANES // SWA_HEAD_DIM), np.ones((SWA_HEAD_DIM, SWA_HEAD_DIM)))
    dup = lambda g: jnp.concatenate([g, g], axis=-1)[None, :]
    return dict(
        q_heads=q_heads, kv_heads=kv_heads,
        g_attn=g_attn_norm_i[None, :],
        w_qkv=w.astype(BF16),
        g_q=dup(g_swa_q[j]), g_k=dup(g_swa_k[j]),
        gmat=jnp.asarray(gmat, BF16),
        sinks=swa_sinks[j],
    )


def _kv_dup(t):
    b, l, kv, d = t.shape
    t = jnp.transpose(t, (2, 0, 1, 3)).reshape(kv, b * l, d)
    return jnp.concatenate([t, t], axis=-1).astype(BF16)


def _first_casts(shared, steps):
    raw = shared['raw']

    def item(a, layer=None):
        split = next(s for s in (1, 2, 4, 8, 16)
                     if (a.shape[1] * s) % (steps * 16) == 0 and a.shape[2] % (s * LANES) == 0)
        return _cast_item(a, steps, layer, split)

    return ([item(a, 0) for a in raw['ffn']]
            + [item(raw[k]) for k in ('ple_gate', 'ple_proj', 'mla_out', 'swa_out')])


def _trunk(x, p, pos, caches, layers, shared, g_final, tm):
    batch, seq, d = x.shape
    n = batch * seq
    h = x.reshape(n, d)
    rows = max(tm, seq)
    mla_tab = _tile_rows(_mla_table(pos), rows)
    swa_tabs = tuple(_tile_rows(t, rows) for t in _swa_tables(pos))
    depth = len(layers)
    n_mla = sum(kind == 'mla' for kind, _ in layers)
    p_rows = p.reshape(depth, n, -1)
    lats, krs, sks, svs = [], [], [], []
    for i, (kind, w) in enumerate(layers):
        j = i // 2
        if kind == 'mla':
            lat, kr, q, k, v = _mla_proj(h, w, mla_tab, tm)
            lats.append(lat)
            krs.append(kr)
            if caches is None:
                tq, hb = MLA_Q, MLA_HEADS_PER_STEP
                first = 'ple' not in shared
                cast = _first_casts(shared, _mla_attn_steps(batch, w['heads'], seq, tq, hb)) if first else ()
                o, casted = _mla_attn(q, k, v, batch, seq, tq=tq, hb=hb, cast=cast)
                if first:
                    shared['ffn'] = {0: casted[:3]}
                    shared['ple'] = casted[3:5]
                    shared['out'] = dict(mla=casted[5], swa=casted[6])
            else:
                lat_c, kr_c = caches[0], caches[1]
                past = lat_c.shape[2]
                k_c, v_c = _mla_expand_call(lat_c.reshape(lat_c.shape[0], batch * past, -1),
                                            kr_c.reshape(kr_c.shape[0], batch * past, -1), j, w, tm=ROWS)
                o = _mla_attn_sample(q, k_c, k, v_c, v, batch, past, seq, hb=w['heads'])
        else:
            q, kd, vd, kf, vf = _swa_proj(h, w, swa_tabs, tm)
            kv_heads = w['kv_heads']
            unhead = lambda t, rows: jnp.transpose(
                t.reshape(kv_heads, batch, seq, SWA_HEAD_DIM)[:, :, seq - rows:], (1, 2, 0, 3))
            if caches is None:
                o = _swa_attn(w['sinks'], q, kd, vd, batch, seq, tq=SWA_Q)
                keep = min(WINDOW, seq)
                sks.append(unhead(kf, keep))
                svs.append(unhead(vf, keep))
            else:
                ck, cv = caches[2][j], caches[3][j]
                keep = ck.shape[1]
                o = _swa_attn_sample(w['sinks'], q, _kv_dup(ck), kd, _kv_dup(cv), vd, batch, keep, seq)
                sks.append(jnp.concatenate([ck, unhead(kf, seq)], axis=1)[:, -keep:])
                svs.append(jnp.concatenate([cv, unhead(vf, seq)], axis=1)[:, -keep:])
        h = _oproj(h, o, shared['out'][kind], j, tm)
        ffn_cast = shared['ffn']
        h = _ffn(h, shared['raw']['ffn_g'][i:i + 1], *ffn_cast[i], 0, min(n, ROWS_FFN), FFN_COLS)
        cast = shared['raw']['ffn'] if (i + 1 not in ffn_cast and i + 1 < depth) else ()
        last = i == depth - 1
        h, casted, stacked = _ple(h, shared['raw']['ple_g'], shared['ple'][0], p_rows, i, shared['ple'][1],
                                  g_final, last, tm, cast, i + 1, (lats, krs) if last else ())
        if casted:
            ffn_cast[i + 1] = casted
    lat, kr = (t.reshape(n_mla, batch, seq, -1) for t in stacked)
    return h.reshape(batch, seq, d), lat, kr, jnp.stack(sks), jnp.stack(svs)


def kernel(x_prompt, x_sample, p_prompt, p_sample, cache_mla_latent, cache_mla_krope, state_swa_k, state_swa_v, g_attn_norm, w_mla_in, g_mla_q_a, w_mla_q_up, g_mla_kv_a, w_mla_kv_up, g_mla_q_nope, g_mla_q_rope, g_mla_k_nope, g_mla_k_rope, w_mla_out, w_swa_qkv, g_swa_q, g_swa_k, swa_sinks, w_swa_out, g_ffn_norm, w_ffn_gate, w_ffn_up, w_ffn_down, g_ple_norm, w_ple_gate, w_ple_proj, g_final):
    depth = g_attn_norm.shape[0]
    layers = []
    for i in range(depth):
        j = i // 2
        if i % 2 == 0:
            layers.append(('mla', _prep_mla(j, g_attn_norm[i], w_mla_in, g_mla_q_a, w_mla_q_up,
                                            g_mla_kv_a, w_mla_kv_up, g_mla_q_nope, g_mla_q_rope,
                                            g_mla_k_nope, g_mla_k_rope)))
        else:
            layers.append(('swa', _prep_swa(j, g_attn_norm[i], w_swa_qkv, g_swa_q, g_swa_k, swa_sinks)))
    shared = dict(raw=dict(ffn=(w_ffn_gate, w_ffn_up, w_ffn_down), ffn_g=g_ffn_norm[:, None, :],
                           ple_gate=w_ple_gate, ple_proj=w_ple_proj, ple_g=g_ple_norm[:, None, :],
                           mla_out=w_mla_out, swa_out=w_swa_out))
    gfin = g_final[None, :]

    seq = x_prompt.shape[1]
    t = x_sample.shape[1]
    past = cache_mla_latent.shape[2]
    pos_p = np.arange(seq)
    pos_s = past + np.arange(t)
    y_p, lat_p, kr_p, sk_p, sv_p = _trunk(x_prompt, p_prompt, pos_p, None, layers, shared, gfin, tm=ROWS)
    caches = (cache_mla_latent, cache_mla_krope, state_swa_k, state_swa_v)
    n_s = x_sample.shape[0] * t
    y_s, lat_s, kr_s, sk_s, sv_s = _trunk(x_sample, p_sample, pos_s, caches, layers, shared, gfin,
                                          tm=min(n_s, ROWS))
    return (y_p, y_s, lat_p, kr_p, sk_p, sv_p, lat_s, kr_s, sk_s, sv_s)
```

```python
import functools

import numpy as np
import jax
import jax.numpy as jnp
from jax import lax
from jax.experimental import pallas as pl
from jax.experimental.pallas import tpu as pltpu

F32 = jnp.float32
BF16 = jnp.bfloat16

EPS = 1e-6
CHUNK = 64
WINDOW = 128
MLA_THETA = 10000.0
SWA_THETA = 500000.0
MLA_NOPE = 128
MLA_ROPE = 64
MLA_V = 128
SWA_HEAD_DIM = 64
SWA_ROT = SWA_HEAD_DIM // 4

LANES = 128
MXU_WIDTH = 256
VMEM_LIMIT = 56 * 1024 * 1024

ROWS = 512
ROWS_FFN = 1024
FFN_COLS = 512
MLA_Q = 512
MLA_HEADS_PER_STEP = 4
SWA_Q = 512
NEG = -1e30
LOG2E = 1.4426950408889634


def _params(*sem):
    return pltpu.CompilerParams(dimension_semantics=sem, vmem_limit_bytes=VMEM_LIMIT)


def _resident(shape, layer=None):
    if layer is None:
        zeros = (0,) * len(shape)
        return pl.BlockSpec(shape, lambda *_: zeros, pipeline_mode=pl.Buffered(1))
    index = (layer,) + (0,) * (len(shape) - 1)
    return pl.BlockSpec((None,) + tuple(shape[1:]), lambda *_: index, pipeline_mode=pl.Buffered(1))


def _cast_item(a, steps, layer=None, col_split=1):
    nl, r, c = a.shape
    assert (r * col_split) % (steps * 16) == 0 and c % (col_split * LANES) == 0, (a.shape, steps)
    lead, first = (nl, 0) if layer is None else (1, layer)
    return dict(a=a, block=(lead, r * col_split // steps, c // col_split),
                in_idx=lambda s: (first, s // col_split, s % col_split),
                out_idx=lambda s: (0, s // col_split, s % col_split),
                out_shape=jax.ShapeDtypeStruct((lead, r, c), BF16))


def _cast_specs(items, step_of):
    ins = [pl.BlockSpec(it['block'], lambda *g, it=it: it['in_idx'](step_of(*g))) for it in items]
    outs = [pl.BlockSpec(it['block'], lambda *g, it=it: it['out_idx'](step_of(*g))) for it in items]
    return ins, outs, [it['out_shape'] for it in items]


def _cast_chunks(srcs, dsts):
    for src, dst in zip(srcs, dsts):
        dst[...] = src[...].astype(BF16)


def _rms(x, g):
    return x * lax.rsqrt(jnp.mean(x * x, axis=-1, keepdims=True) + EPS) * g


def _dot(a, b):
    return jnp.dot(a, b, preferred_element_type=F32)


def _dot_t(a, b):
    return lax.dot_general(a, b, (((1,), (1,)), ((), ())), preferred_element_type=F32)


def _rope_dup(x, g, tab):
    ss = jnp.sum(x * x, axis=-1, keepdims=True)
    y = x * lax.rsqrt(ss * (1.0 / LANES) + EPS) * g
    t = y * tab
    return t + pltpu.roll(t, LANES // 2, axis=1)


def _mla_heads(c_q, latb, krb, tab, wq_ref, wkn_ref, wv_ref, gqn, gqr, gkn, gsum_ref,
               q_ref, k_ref, v_ref, heads):
    gsum = gsum_ref[...]
    pair = 2 * LANES

    def inv_rms(x):
        return lax.rsqrt(_dot((x * x).astype(BF16), gsum) * (1.0 / LANES) + EPS)

    def finish_kv(p, xk, xv):
        kn = jnp.concatenate([_rms(xk[:, :LANES], gkn[:, :LANES]),
                              _rms(xk[:, LANES:], gkn[:, LANES:])], axis=1)
        for u in range(2):
            h = 2 * p + u
            sl = slice(u * LANES, (u + 1) * LANES)
            k_ref[h, :, 0:MLA_NOPE] = kn[:, sl].astype(BF16)
            k_ref[h, :, MLA_NOPE:MLA_NOPE + MLA_ROPE] = krb
            v_ref[h] = xv[:, sl].astype(BF16)

    def finish_q(p, xq):
        qn = xq[:, :pair] * inv_rms(xq[:, :pair]) * gqn
        t = xq[:, pair:] * inv_rms(xq[:, pair:]) * gqr * tab
        for u in range(2):
            h = 2 * p + u
            sl = slice(u * LANES, (u + 1) * LANES)
            q_ref[h, :, 0:MLA_NOPE] = qn[:, sl].astype(BF16)
            tu = t[:, sl]
            qr = tu + pltpu.roll(tu, LANES // 2, axis=1)
            q_ref[h, :, MLA_NOPE:MLA_NOPE + MLA_ROPE] = qr[:, :MLA_ROPE].astype(BF16)

    n_pairs = heads // 2
    dot_q = lambda p: _dot(c_q, wq_ref[:, p * 2 * pair:(p + 1) * 2 * pair])
    xq = None if c_q is None else dot_q(0)
    pending = None
    for p in range(n_pairs):
        xk = _dot(latb, wkn_ref[:, p * pair:(p + 1) * pair])
        xv = _dot(latb, wv_ref[:, p * pair:(p + 1) * pair])
        xq_next = dot_q(p + 1) if (c_q is not None and p + 1 < n_pairs) else None
        if xq is not None:
            finish_q(p, xq)
        if pending is not None:
            finish_kv(*pending)
        pending, xq = (p, xk, xv), xq_next
    finish_kv(*pending)


def _mla_proj_kernel(h_ref, gattn_ref, win_ref, gqa_ref, gkva_ref, gkr_ref, tab_ref, wq_ref,
                     gqn_ref, gqr_ref, wkn_ref, wv_ref, gkn_ref, gsum_ref,
                     lat_ref, kr_ref, q_ref, k_ref, v_ref, *, heads, q_lora, kv_lora, scale):
    hn = _rms(h_ref[...], gattn_ref[...]).astype(BF16)
    a = _dot(hn, win_ref[...])
    c_q = _rms(a[:, :q_lora], gqa_ref[...]).astype(BF16)
    lat = _rms(a[:, q_lora:q_lora + kv_lora], gkva_ref[...])
    lat_ref[...] = lat
    tab = tab_ref[...]
    kr = _rope_dup(a[:, q_lora + kv_lora:], gkr_ref[...], tab)
    kr_ref[...] = kr[:, :MLA_ROPE]
    _mla_heads(c_q, lat.astype(BF16), kr[:, :MLA_ROPE].astype(BF16), jnp.concatenate([tab, tab], axis=1),
               wq_ref, wkn_ref, wv_ref, gqn_ref[...] * scale, gqr_ref[...] * scale, gkn_ref[...],
               gsum_ref, q_ref, k_ref, v_ref, heads)


def _mla_expand_kernel(lat_ref, kr_ref, wkn_ref, wv_ref, gkn_ref, gsum_ref, k_ref, v_ref, *, heads):
    _mla_heads(None, lat_ref[...].astype(BF16), kr_ref[...].astype(BF16), None, None, wkn_ref, wv_ref,
               None, None, gkn_ref[...], gsum_ref, None, k_ref, v_ref, heads)


def _mla_proj(h, w, tab, tm):
    n, d = h.shape
    heads = w['heads']
    q_lora, kv_lora = w['g_q_a'].shape[1], w['g_kv_a'].shape[1]
    qk = MLA_NOPE + MLA_ROPE
    nt = tab.shape[0] // tm
    row = lambda i: (i, 0)
    hrow = lambda i: (0, i, 0)
    kern = functools.partial(_mla_proj_kernel, heads=heads, q_lora=q_lora, kv_lora=kv_lora,
                             scale=qk ** -0.5 * LOG2E)
    weights = (w['g_attn'], w['w_in'], w['g_q_a'], w['g_kv_a'], w['g_k_rope'], tab, w['w_q'],
               w['g_q_nope'], w['g_q_rope'], w['w_kn'], w['w_v'], w['g_k_nope'], w['gsum'])
    return pl.pallas_call(
        kern,
        grid=(n // tm,),
        in_specs=[
            pl.BlockSpec((tm, d), row),
            _resident(w['g_attn'].shape), _resident(w['w_in'].shape),
            _resident(w['g_q_a'].shape), _resident(w['g_kv_a'].shape), _resident(w['g_k_rope'].shape),
            pl.BlockSpec((tm, LANES), lambda i: (i % nt, 0)),
            _resident(w['w_q'].shape), _resident(w['g_q_nope'].shape), _resident(w['g_q_rope'].shape),
            _resident(w['w_kn'].shape), _resident(w['w_v'].shape), _resident(w['g_k_nope'].shape),
            _resident(w['gsum'].shape),
        ],
        out_specs=[
            pl.BlockSpec((tm, kv_lora), row),
            pl.BlockSpec((tm, MLA_ROPE), row),
            pl.BlockSpec((heads, tm, qk), hrow),
            pl.BlockSpec((heads, tm, qk), hrow),
            pl.BlockSpec((heads, tm, MLA_V), hrow),
        ],
        out_shape=[
            jax.ShapeDtypeStruct((n, kv_lora), F32),
            jax.ShapeDtypeStruct((n, MLA_ROPE), F32),
            jax.ShapeDtypeStruct((heads, n, qk), BF16),
            jax.ShapeDtypeStruct((heads, n, qk), BF16),
            jax.ShapeDtypeStruct((heads, n, MLA_V), BF16),
        ],
        compiler_params=_params("parallel"),
        name="mla_proj",
    )(h, *weights)


def _mla_expand_call(lat, kr, layer, w, tm):
    _, n, kv_lora = lat.shape
    heads = w['heads']
    qk = MLA_NOPE + MLA_ROPE
    hrow = lambda i: (0, i, 0)
    return pl.pallas_call(
        functools.partial(_mla_expand_kernel, heads=heads),
        grid=(n // tm,),
        in_specs=[pl.BlockSpec((None, tm, kv_lora), lambda i: (layer, i, 0)),
                  pl.BlockSpec((None, tm, MLA_ROPE), lambda i: (layer, i, 0)),
                  _resident(w['w_kn'].shape), _resident(w['w_v'].shape),
                  _resident(w['g_k_nope'].shape), _resident(w['gsum'].shape)],
        out_specs=[pl.BlockSpec((heads, tm, qk), hrow), pl.BlockSpec((heads, tm, MLA_V), hrow)],
        out_shape=[jax.ShapeDtypeStruct((heads, n, qk), BF16),
                   jax.ShapeDtypeStruct((heads, n, MLA_V), BF16)],
        compiler_params=_params("parallel"),
        name="mla_expand",
    )(lat, kr, w['w_kn'], w['w_v'], w['g_k_nope'], w['gsum'])


def _mla_attn_kernel(q_ref, k_ref, v_ref, *rest, hb, tq, n_cast):
    cast_in, o_ref, cast_out, s_ref = rest[:n_cast], rest[n_cast], rest[n_cast + 1:-1], rest[-1]
    _cast_chunks(cast_in, cast_out)
    i = pl.program_id(2)
    kc = lax.broadcasted_iota(jnp.int32, (tq, tq), 0) // CHUNK
    qc = lax.broadcasted_iota(jnp.int32, (tq, tq), 1) // CHUNK
    diag_bias = jnp.where(kc <= qc, 0.0, NEG).astype(F32)
    qs = [q_ref[h] for h in range(hb)]

    def scores(j, slot):
        start = pl.multiple_of(j * tq, tq)
        mx = []
        for h in range(hb):
            s = _dot_t(k_ref[h, pl.ds(start, tq), :], qs[h])
            s_ref[slot, h] = s
            mx.append(jnp.max(s, axis=0, keepdims=True))
        return tuple(mx)

    def update(j, slot, carry, mx):
        start = pl.multiple_of(j * tq, tq)
        ps, stats = [], []
        for h in range(hb):
            m, l, _ = carry[h]
            s = s_ref[slot, h]
            m_new = jnp.maximum(m, mx[h])
            alpha = jnp.exp2(m - m_new)
            p = jnp.exp2(s - m_new)
            stats.append((m_new, alpha * l + jnp.sum(p, axis=0, keepdims=True), alpha))
            ps.append(p.astype(BF16))
        out = []
        for h in range(hb):
            pv = lax.dot_general(v_ref[h, pl.ds(start, tq), :], ps[h],
                                 (((0,), (0,)), ((), ())), preferred_element_type=F32)
            m_new, l, alpha = stats[h]
            out.append((m_new, l, alpha * carry[h][2] + pv))
        return tuple(out)

    def update_diag(j, slot, carry):
        start = pl.multiple_of(j * tq, tq)
        half = tq // 2
        spans = ((slice(0, half), half), (slice(half, tq), tq))
        ps, stats = [], []
        for h in range(hb):
            m, l, _ = carry[h]
            m_new, l_new, alpha, p_h = [], [], [], []
            for cols, keys in spans:
                s = s_ref[slot, h, 0:keys, cols] + diag_bias[0:keys, cols]
                mc = jnp.maximum(m[:, cols], jnp.max(s, axis=0, keepdims=True))
                a = jnp.exp2(m[:, cols] - mc)
                p = jnp.exp2(s - mc)
                m_new.append(mc)
                alpha.append(a)
                l_new.append(a * l[:, cols] + jnp.sum(p, axis=0, keepdims=True))
                p_h.append(p.astype(BF16))
            stats.append((jnp.concatenate(m_new, axis=1), jnp.concatenate(l_new, axis=1), alpha))
            ps.append(p_h)
        out = []
        for h in range(hb):
            m_new, l_new, alpha = stats[h]
            acc = carry[h][2]
            parts = []
            for (cols, keys), a, p in zip(spans, alpha, ps[h]):
                pv = lax.dot_general(v_ref[h, pl.ds(start, keys), :], p,
                                     (((0,), (0,)), ((), ())), preferred_element_type=F32)
                parts.append(a * acc[:, cols] + pv)
            out.append((m_new, l_new, jnp.concatenate(parts, axis=1)))
        return tuple(out)

    def finish(carry):
        for h in range(hb):
            _, l, acc = carry[h]
            o_ref[:, h * MLA_V:(h + 1) * MLA_V] = (acc / l).T.astype(BF16)

    def pair(t, state):
        carry, mx0 = state
        mx1 = scores(2 * t + 1, 1)
        carry = update(2 * t, 0, carry, mx0)
        mx0 = scores(2 * t + 2, 0)
        return update(2 * t + 1, 1, carry, mx1), mx0

    init = tuple((jnp.full((1, tq), NEG, F32), jnp.zeros((1, tq), F32), jnp.zeros((MLA_V, tq), F32))
                 for _ in range(hb))
    carry, mx0 = lax.fori_loop(0, i // 2, pair, (init, scores(0, 0)))

    @pl.when(i % 2 == 0)
    def _():
        finish(update_diag(i, 0, carry))

    @pl.when(i % 2 == 1)
    def _():
        scores(i, 1)
        finish(update_diag(i, 1, update(i - 1, 0, carry, mx0)))


def _mla_attn_steps(batch, heads, seq, tq, hb):
    return batch * (heads // hb) * (seq // tq)


def _mla_attn(q, k, v, batch, seq, tq, hb, cast=()):
    heads, n, qk = q.shape
    nq = seq // tq
    ng = heads // hb
    kv_map = lambda b, g, i: (g, b, 0)
    cast_in, cast_out, cast_shapes = _cast_specs(cast, lambda b, g, i: (b * ng + g) * nq + i)
    out = pl.pallas_call(
        functools.partial(_mla_attn_kernel, hb=hb, tq=tq, n_cast=len(cast)),
        grid=(batch, ng, nq),
        in_specs=[pl.BlockSpec((hb, tq, qk), lambda b, g, i: (g, b * nq + i, 0)),
                  pl.BlockSpec((hb, seq, qk), kv_map),
                  pl.BlockSpec((hb, seq, MLA_V), kv_map)] + cast_in,
        out_specs=[pl.BlockSpec((tq, hb * MLA_V), lambda b, g, i: (b * nq + i, g))] + cast_out,
        out_shape=[jax.ShapeDtypeStruct((n, heads * MLA_V), BF16)] + cast_shapes,
        scratch_shapes=[pltpu.VMEM((2, hb, tq, tq), F32)],
        compiler_params=_params("parallel", "parallel", "arbitrary"),
        name="mla_attn",
    )(q, k, v, *[it['a'] for it in cast])
    return out[0], tuple(out[1:])


def _mla_attn_sample_kernel(q_ref, kc_ref, kn_ref, vc_ref, vn_ref, o_ref, *, hb):
    ss = [(_dot_t(kc_ref[h], q_ref[h]), _dot_t(kn_ref[h], q_ref[h])) for h in range(hb)]
    ps = []
    for s_c, s_n in ss:
        m = jnp.maximum(jnp.max(s_c, axis=0, keepdims=True), jnp.max(s_n, axis=0, keepdims=True))
        p_c = jnp.exp2(s_c - m)
        p_n = jnp.exp2(s_n - m)
        l = jnp.sum(p_c, axis=0, keepdims=True) + jnp.sum(p_n, axis=0, keepdims=True)
        ps.append((p_c.astype(BF16), p_n.astype(BF16), l))
    tn = (((0,), (0,)), ((), ()))
    for h, (p_c, p_n, l) in enumerate(ps):
        acc = (lax.dot_general(vc_ref[h], p_c, tn, preferred_element_type=F32)
               + lax.dot_general(vn_ref[h], p_n, tn, preferred_element_type=F32))
        o_ref[:, h * MLA_V:(h + 1) * MLA_V] = (acc / l).T.astype(BF16)


def _mla_attn_sample(q, k_cache, k_new, v_cache, v_new, batch, past, t, hb):
    heads, n, qk = q.shape
    q_pos = past + np.arange(t)
    k_pos = np.arange(past + t)
    assert np.all((k_pos // CHUNK)[None, :] <= (q_pos // CHUNK)[:, None])
    m3 = lambda b, g: (g, b, 0)
    return pl.pallas_call(
        functools.partial(_mla_attn_sample_kernel, hb=hb),
        grid=(batch, heads // hb),
        in_specs=[pl.BlockSpec((hb, t, qk), m3), pl.BlockSpec((hb, past, qk), m3),
                  pl.BlockSpec((hb, t, qk), m3), pl.BlockSpec((hb, past, MLA_V), m3),
                  pl.BlockSpec((hb, t, MLA_V), m3)],
        out_specs=pl.BlockSpec((t, hb * MLA_V), lambda b, g: (b, g)),
        out_shape=jax.ShapeDtypeStruct((n, heads * MLA_V), BF16),
        compiler_params=_params("parallel", "parallel"),
        name="mla_attn_sample",
    )(q, k_cache, k_new, v_cache, v_new)


def _swa_proj_kernel(h_ref, gattn_ref, w_ref, gq_ref, gk_ref, tc_ref, ts1_ref, ts2_ref, gmat_ref,
                     q_ref, kd_ref, vd_ref, kf_ref, vf_ref, *, q_cols, kv_heads, scale):
    hn = _rms(h_ref[...], gattn_ref[...]).astype(BF16)
    tc, ts1, ts2 = tc_ref[...], ts1_ref[...], ts2_ref[...]
    gmat = gmat_ref[...]

    def rope(x, ss, g):
        y = x * lax.rsqrt(ss * (1.0 / SWA_HEAD_DIM) + EPS) * g
        return (y * tc + pltpu.roll(y, LANES - SWA_ROT // 2, axis=1) * ts1
                + pltpu.roll(y, SWA_ROT // 2, axis=1) * ts2)

    gq = gq_ref[...] * scale
    gk = gk_ref[...]
    n_q = q_cols // 2
    n_kv = kv_heads // 4
    lo = lax.broadcasted_iota(jnp.int32, (1, LANES), 1) < SWA_HEAD_DIM

    def put_dup(col, first_head, dup_ref, flat_ref):
        swapped = pltpu.roll(col, SWA_HEAD_DIM, axis=1)
        for u, d in enumerate((jnp.where(lo, col, swapped), jnp.where(lo, swapped, col))):
            dup_ref[first_head + u] = d.astype(BF16)
            flat_ref[first_head + u] = d[:, :SWA_HEAD_DIM]

    def finish(c, x2):
        if c < n_q + n_kv:
            ss = _dot((x2 * x2).astype(BF16), gmat)
        for u in range(2):
            sl = slice(u * LANES, (u + 1) * LANES)
            if c < n_q:
                col = 2 * c + u
                q_ref[:, col * LANES:(col + 1) * LANES] = rope(x2[:, sl], ss[:, sl], gq).astype(BF16)
            elif c < n_q + n_kv:
                put_dup(rope(x2[:, sl], ss[:, sl], gk), 4 * (c - n_q) + 2 * u, kd_ref, kf_ref)
            else:
                put_dup(x2[:, sl], 4 * (c - n_q - n_kv) + 2 * u, vd_ref, vf_ref)

    pending = None
    for c in range(n_q + 2 * n_kv):
        x2 = _dot(hn, w_ref[:, c * MXU_WIDTH:(c + 1) * MXU_WIDTH])
        if pending is not None:
            finish(*pending)
        pending = (c, x2)
    finish(*pending)


def _swa_proj(h, w, tabs, tm):
    n, d = h.shape
    q_heads, kv_heads = w['q_heads'], w['kv_heads']
    q_cols = q_heads * SWA_HEAD_DIM // LANES
    nt = tabs[0].shape[0] // tm
    row = lambda i: (i, 0)
    hrow = lambda i: (0, i, 0)
    tspec = pl.BlockSpec((tm, LANES), lambda i: (i % nt, 0))
    kern = functools.partial(_swa_proj_kernel, q_cols=q_cols, kv_heads=kv_heads,
                             scale=SWA_HEAD_DIM ** -0.5 * LOG2E)
    return pl.pallas_call(
        kern,
        grid=(n // tm,),
        in_specs=[pl.BlockSpec((tm, d), row), _resident(w['g_attn'].shape),
                  _resident(w['w_qkv'].shape), _resident(w['g_q'].shape), _resident(w['g_k'].shape),
                  tspec, tspec, tspec, _resident(w['gmat'].shape)],
        out_specs=[pl.BlockSpec((tm, q_cols * LANES), row),
                   pl.BlockSpec((kv_heads, tm, LANES), hrow),
                   pl.BlockSpec((kv_heads, tm, LANES), hrow),
                   pl.BlockSpec((kv_heads, tm, SWA_HEAD_DIM), hrow),
                   pl.BlockSpec((kv_heads, tm, SWA_HEAD_DIM), hrow)],
        out_shape=[jax.ShapeDtypeStruct((n, q_cols * LANES), BF16),
                   jax.ShapeDtypeStruct((kv_heads, n, LANES), BF16),
                   jax.ShapeDtypeStruct((kv_heads, n, LANES), BF16),
                   jax.ShapeDtypeStruct((kv_heads, n, SWA_HEAD_DIM), F32),
                   jax.ShapeDtypeStruct((kv_heads, n, SWA_HEAD_DIM), F32)],
        compiler_params=_params("parallel"),
        name="swa_proj",
    )(h, w['g_attn'], w['w_qkv'], w['g_q'], w['g_k'], *tabs, w['gmat'])


def _swa_attn_kernel(sink_ref, q_ref, kp_ref, kc_ref, vp_ref, vc_ref, o_ref, *,
                     kv_heads, group, banded):
    tq = q_ref.shape[0]
    n_p, n_c = kp_ref.shape[1], kc_ref.shape[1]
    n_k = n_p + n_c
    lane = lax.broadcasted_iota(jnp.int32, (1, LANES), 1)
    row = lax.broadcasted_iota(jnp.int32, (LANES, 1), 0)
    if banded:
        assert n_p == WINDOW == LANES
        tiles = [(t * LANES, LANES, t * LANES, 2 * LANES) for t in range(tq // LANES)]
    else:
        tiles = [(0, tq, 0, n_k)]
    biases = []
    for q0, qn, k0, kn in tiles:
        if not banded:
            biases.append(None)
            continue
        win = WINDOW // CHUNK
        kc = (lax.broadcasted_iota(jnp.int32, (kn, qn), 0) + k0) // CHUNK - n_p // CHUNK
        qc = (lax.broadcasted_iota(jnp.int32, (kn, qn), 1) + q0) // CHUNK
        ok = (kc <= qc) & (kc >= qc - win) & ((kc >= 0) | (pl.program_id(1) > 0))
        biases.append(jnp.where(ok, 0.0, NEG).astype(F32))
    zero = jnp.zeros((), BF16)
    half = group // 2
    for kh in range(kv_heads):
        k_all = jnp.concatenate([kp_ref[kh], kc_ref[kh]], axis=0)
        v_t = jnp.concatenate([vp_ref[kh], vc_ref[kh]], axis=0).T
        ks = (jnp.where(lane < SWA_HEAD_DIM, k_all, zero), jnp.where(lane < SWA_HEAD_DIM, zero, k_all))
        vs = (jnp.where(row < SWA_HEAD_DIM, v_t, zero), jnp.where(row < SWA_HEAD_DIM, zero, v_t))
        cols = [q_ref[:, (kh * half + c) * LANES:(kh * half + c + 1) * LANES] for c in range(half)]
        for (q0, qn, k0, kn), bias in zip(tiles, biases):
            ss = [_dot_t(ks[u][k0:k0 + kn], cols[c][q0:q0 + qn])
                  for c in range(half) for u in range(2)]
            es, rdens = [], []
            for c in range(half):
                for u in range(2):
                    s = ss[2 * c + u]
                    if bias is not None:
                        s = s + bias
                    sink = sink_ref[2 * (kh * half + c) + u] * LOG2E
                    m = jnp.maximum(jnp.max(s, axis=0, keepdims=True), sink)
                    e = jnp.exp2(s - m)
                    rdens.append(1.0 / (jnp.sum(e, axis=0, keepdims=True) + jnp.exp2(sink - m)))
                    es.append(e.astype(BF16))
            for c in range(half):
                o_t = (_dot(vs[0][:, k0:k0 + kn], es[2 * c]) * rdens[2 * c]
                       + _dot(vs[1][:, k0:k0 + kn], es[2 * c + 1]) * rdens[2 * c + 1])
                col = kh * half + c
                o_ref[q0:q0 + qn, col * LANES:(col + 1) * LANES] = o_t.T.astype(BF16)


def _swa_attn(sinks, q, kd, vd, batch, seq, tq):
    n, dq = q.shape
    kv_heads = kd.shape[0]
    group = (dq // SWA_HEAD_DIM) // kv_heads
    nq = seq // tq
    per = tq // WINDOW
    prev = lambda b, i: (0, jnp.maximum(i * per - 1, 0) + b * nq * per, 0)
    cur = lambda b, i: (0, b * nq + i, 0)
    return pl.pallas_call(
        functools.partial(_swa_attn_kernel, kv_heads=kv_heads, group=group, banded=True),
        grid=(batch, nq),
        in_specs=[pl.BlockSpec(memory_space=pltpu.SMEM),
                  pl.BlockSpec((tq, dq), lambda b, i: (b * nq + i, 0)),
                  pl.BlockSpec((kv_heads, WINDOW, LANES), prev),
                  pl.BlockSpec((kv_heads, tq, LANES), cur),
                  pl.BlockSpec((kv_heads, WINDOW, LANES), prev),
                  pl.BlockSpec((kv_heads, tq, LANES), cur)],
        out_specs=pl.BlockSpec((tq, dq), lambda b, i: (b * nq + i, 0)),
        out_shape=jax.ShapeDtypeStruct((n, dq), BF16),
        compiler_params=_params("parallel", "arbitrary"),
        name="swa_attn",
    )(sinks, q, kd, kd, vd, vd)


def _swa_attn_sample(sinks, q, kd_cache, kd_new, vd_cache, vd_new, batch, keep, t):
    n, dq = q.shape
    kv_heads = kd_new.shape[0]
    group = (dq // SWA_HEAD_DIM) // kv_heads
    blk = lambda b: (0, b, 0)
    return pl.pallas_call(
        functools.partial(_swa_attn_kernel, kv_heads=kv_heads, group=group, banded=False),
        grid=(batch,),
        in_specs=[pl.BlockSpec(memory_space=pltpu.SMEM),
                  pl.BlockSpec((t, dq), lambda b: (b, 0)),
                  pl.BlockSpec((kv_heads, keep, LANES), blk),
                  pl.BlockSpec((kv_heads, t, LANES), blk),
                  pl.BlockSpec((kv_heads, keep, LANES), blk),
                  pl.BlockSpec((kv_heads, t, LANES), blk)],
        out_specs=pl.BlockSpec((t, dq), lambda b: (b, 0)),
        out_shape=jax.ShapeDtypeStruct((n, dq), BF16),
        compiler_params=_params("parallel"),
        name="swa_attn_sample",
    )(sinks, q, kd_cache, kd_new, vd_cache, vd_new)


def _oproj_kernel(h_ref, o_ref, w_ref, out_ref):
    out_ref[...] = h_ref[...] + _dot(o_ref[...], w_ref[...])


def _oproj(h, o, w, layer, tm):
    n, d = h.shape
    row = lambda i: (i, 0)
    return pl.pallas_call(
        _oproj_kernel,
        grid=(n // tm,),
        in_specs=[pl.BlockSpec((tm, d), row), pl.BlockSpec((tm, o.shape[1]), row),
                  _resident(w.shape, layer)],
        out_specs=pl.BlockSpec((tm, d), row),
        out_shape=jax.ShapeDtypeStruct((n, d), F32),
        compiler_params=_params("parallel"),
        name="attn_out_proj",
    )(h, o, w)


def _ffn_kernel(h_ref, g_ref, wg_ref, wu_ref, wd_ref, out_ref, hn_ref):
    @pl.when(pl.program_id(1) == 0)
    def _():
        x = h_ref[...]
        hn_ref[...] = _rms(x, g_ref[...]).astype(BF16)
        out_ref[...] = x

    hn = hn_ref[...]
    w = wg_ref.shape[1] // 2
    acts = []
    for c in range(2):
        a = _dot(hn, wg_ref[:, c * w:(c + 1) * w])
        b = _dot(hn, wu_ref[:, c * w:(c + 1) * w])
        acts.append((a * jax.nn.sigmoid(a) * b).astype(BF16))
    out_ref[...] += _dot(acts[0], wd_ref[0:w, :]) + _dot(acts[1], wd_ref[w:2 * w, :])


def _ffn(h, g, wg, wu, wd, layer, tm, tf):
    n, d = h.shape
    f = wg.shape[2]
    row = lambda i, j: (i, 0)
    return pl.pallas_call(
        _ffn_kernel,
        grid=(n // tm, f // tf),
        in_specs=[pl.BlockSpec((tm, d), row), pl.BlockSpec((None, 1, d), lambda i, j: (layer, 0, 0)),
                  pl.BlockSpec((None, d, tf), lambda i, j: (layer, 0, j)),
                  pl.BlockSpec((None, d, tf), lambda i, j: (layer, 0, j)),
                  pl.BlockSpec((None, tf, d), lambda i, j: (layer, j, 0))],
        out_specs=pl.BlockSpec((tm, d), row),
        out_shape=jax.ShapeDtypeStruct((n, d), F32),
        scratch_shapes=[pltpu.VMEM((tm, d), BF16)],
        compiler_params=_params("parallel", "arbitrary"),
        name="swiglu_ffn",
    )(h, g, wg, wu, wd)


def _ple_kernel(h_ref, g_ref, wgate_ref, p_ref, wproj_ref, gfin_ref, *rest, final, n_cast, stack):
    n_rows = sum(stack)
    cast_in, rows_in = rest[:n_cast], rest[n_cast:n_cast + n_rows]
    out_ref = rest[n_cast + n_rows]
    cast_out = rest[n_cast + n_rows + 1:2 * n_cast + n_rows + 1]
    stacked = rest[2 * n_cast + n_rows + 1:]
    first = 0
    for dst, count in zip(stacked, stack):
        for s in range(count):
            dst[s] = rows_in[first + s][...]
        first += count
    x = h_ref[...]
    gate = jax.nn.sigmoid(_dot(_rms(x, g_ref[...]).astype(BF16), wgate_ref[...]))
    y = x + gate * _dot(p_ref[...].astype(BF16), wproj_ref[...])
    if final:
        y = _rms(y, gfin_ref[...])
    out_ref[...] = y
    _cast_chunks(cast_in, cast_out)


def _ple(h, g, wgate, p, layer, wproj, gfin, final, tm, cast=(), cast_layer=0, stack=()):
    n, d = h.shape
    steps = n // tm
    row = lambda i: (i, 0)
    items = [_cast_item(a, steps, cast_layer) for a in cast]
    cast_specs, cast_out_specs, cast_shapes = _cast_specs(items, lambda i: i)
    rows = [a for group in stack for a in group]
    row_specs = [pl.BlockSpec((tm, a.shape[1]), row) for a in rows]
    stack_specs = [pl.BlockSpec((len(gr), tm, gr[0].shape[1]), lambda i: (0, i, 0)) for gr in stack]
    stack_shapes = [jax.ShapeDtypeStruct((len(gr), n, gr[0].shape[1]), gr[0].dtype) for gr in stack]
    out = pl.pallas_call(
        functools.partial(_ple_kernel, final=final, n_cast=len(cast), stack=tuple(len(gr) for gr in stack)),
        grid=(steps,),
        in_specs=[pl.BlockSpec((tm, d), row), _resident(g.shape, layer), _resident(wgate.shape, layer),
                  pl.BlockSpec((None, tm, p.shape[2]), lambda i: (layer, i, 0)),
                  _resident(wproj.shape, layer), _resident(gfin.shape)] + cast_specs + row_specs,
        out_specs=[pl.BlockSpec((tm, d), row)] + cast_out_specs + stack_specs,
        out_shape=[jax.ShapeDtypeStruct((n, d), F32)] + cast_shapes + stack_shapes,
        compiler_params=_params("parallel"),
        name="ple_embed",
    )(h, g, wgate, p, wproj, gfin, *cast, *rows)
    n_c = len(cast)
    return out[0], tuple(out[1:1 + n_c]), tuple(out[1 + n_c:])


def _angles(pos, r, theta):
    inv = np.power(np.float64(theta), -np.arange(0, r, 2, dtype=np.float64) / r)
    ang = pos.astype(np.float64)[:, None] * inv[None, :]
    return np.cos(ang).astype(np.float32), np.sin(ang).astype(np.float32)


def _mla_table(pos):
    cos, sin = _angles(pos, MLA_ROPE, MLA_THETA)
    return np.concatenate([cos, cos, -sin, sin], axis=-1)


def _swa_tables(pos):
    cos, sin = _angles(pos, SWA_ROT, SWA_THETA)
    s = cos.shape[0]
    rest = SWA_HEAD_DIM - SWA_ROT
    one = np.ones((s, rest), np.float32)
    zero = np.zeros((s, rest), np.float32)
    zh = np.zeros_like(sin)
    tc = np.concatenate([cos, cos, one], axis=-1)
    ts1 = np.concatenate([-sin, zh, zero], axis=-1)
    ts2 = np.concatenate([zh, sin, zero], axis=-1)
    return tuple(np.concatenate([t, t], axis=-1) for t in (tc, ts1, ts2))


def _tile_rows(tab, rows):
    tab = tab if tab.shape[0] >= rows else np.tile(tab, (rows // tab.shape[0], 1))
    return jnp.asarray(tab)


def _swap_halves(x):
    half = x.shape[-1] // 2
    return jnp.concatenate([x[..., half:], x[..., :half]], axis=-1)


def _prep_mla(j, g_attn_norm_i, w_mla_in, g_mla_q_a, w_mla_q_up, g_mla_kv_a, w_mla_kv_up, g_mla_q_nope,
              g_mla_q_rope, g_mla_k_nope, g_mla_k_rope):
    q_lora, kv_lora = g_mla_q_a.shape[1], g_mla_kv_a.shape[1]
    qk = MLA_NOPE + MLA_ROPE
    heads = w_mla_q_up.shape[2] // qk
    w_in = w_mla_in[j]
    k_r = w_in[:, q_lora + kv_lora:]
    w_in = jnp.concatenate([w_in, _swap_halves(k_r)], axis=-1)
    wq = w_mla_q_up[j].reshape(q_lora, heads, qk)
    rope = wq[:, :, MLA_NOPE:]
    wq = jnp.concatenate([wq[:, :, :MLA_NOPE].reshape(q_lora, heads // 2, 2 * MLA_NOPE),
                          jnp.concatenate([rope, _swap_halves(rope)], axis=-1).reshape(q_lora, heads // 2, 2 * LANES)],
                         axis=-1).reshape(q_lora, heads * 2 * LANES)
    wkv = w_mla_kv_up[j].reshape(kv_lora, heads, MLA_NOPE + MLA_V)
    dup = lambda g: jnp.concatenate([g, _swap_halves(g)], axis=-1)[None, :]
    two = lambda g: jnp.concatenate([g, g], axis=-1)
    gsum = np.kron(np.eye(2), np.ones((LANES, LANES)))
    return dict(
        heads=heads, gsum=jnp.asarray(gsum, BF16),
        g_attn=g_attn_norm_i[None, :],
        w_in=w_in.astype(BF16),
        g_q_a=g_mla_q_a[j][None, :], g_kv_a=g_mla_kv_a[j][None, :],
        g_k_rope=dup(g_mla_k_rope[j]), g_q_rope=two(dup(g_mla_q_rope[j])),
        g_q_nope=two(g_mla_q_nope[j][None, :]), g_k_nope=two(g_mla_k_nope[j][None, :]),
        w_q=wq.astype(BF16),
        w_kn=wkv[:, :, :MLA_NOPE].reshape(kv_lora, heads * MLA_NOPE).astype(BF16),
        w_v=wkv[:, :, MLA_NOPE:].reshape(kv_lora, heads * MLA_V).astype(BF16),
    )


def _prep_swa(j, g_attn_norm_i, w_swa_qkv, g_swa_q, g_swa_k, swa_sinks):
    q_heads = swa_sinks.shape[1]
    w = w_swa_qkv[j]
    kv_heads = (w.shape[1] - q_heads * SWA_HEAD_DIM) // (2 * SWA_HEAD_DIM)
    assert kv_heads % 4 == 0 and q_heads % 4 == 0
    gmat = np.kron(np.eye(2 * LANES // SWA_HEAD_DIM), np.ones((SWA_HEAD_DIM, SWA_HEAD_DIM)))
    dup = lambda g: jnp.concatenate([g, g], axis=-1)[None, :]
    return dict(
        q_heads=q_heads, kv_heads=kv_heads,
        g_attn=g_attn_norm_i[None, :],
        w_qkv=w.astype(BF16),
        g_q=dup(g_swa_q[j]), g_k=dup(g_swa_k[j]),
        gmat=jnp.asarray(gmat, BF16),
        sinks=swa_sinks[j],
    )


def _kv_dup(t):
    b, l, kv, d = t.shape
    t = jnp.transpose(t, (2, 0, 1, 3)).reshape(kv, b * l, d)
    return jnp.concatenate([t, t], axis=-1).astype(BF16)


def _first_casts(shared, steps):
    raw = shared['raw']

    def item(a, layer=None):
        split = next(s for s in (1, 2, 4, 8, 16)
                     if (a.shape[1] * s) % (steps * 16) == 0 and a.shape[2] % (s * LANES) == 0)
        return _cast_item(a, steps, layer, split)

    return ([item(a, 0) for a in raw['ffn']]
            + [item(raw[k]) for k in ('ple_gate', 'ple_proj', 'mla_out', 'swa_out')])


def _trunk(x, p, pos, caches, layers, shared, g_final, tm):
    batch, seq, d = x.shape
    n = batch * seq
    h = x.reshape(n, d)
    rows = max(tm, seq)
    mla_tab = _tile_rows(_mla_table(pos), rows)
    swa_tabs = tuple(_tile_rows(t, rows) for t in _swa_tables(pos))
    depth = len(layers)
    n_mla = sum(kind == 'mla' for kind, _ in layers)
    p_rows = p.reshape(depth, n, -1)
    lats, krs, sks, svs = [], [], [], []
    for i, (kind, w) in enumerate(layers):
        j = i // 2
        if kind == 'mla':
            lat, kr, q, k, v = _mla_proj(h, w, mla_tab, tm)
            lats.append(lat)
            krs.append(kr)
            if caches is None:
                tq, hb = MLA_Q, MLA_HEADS_PER_STEP
                first = 'ple' not in shared
                cast = _first_casts(shared, _mla_attn_steps(batch, w['heads'], seq, tq, hb)) if first else ()
                o, casted = _mla_attn(q, k, v, batch, seq, tq=tq, hb=hb, cast=cast)
                if first:
                    shared['ffn'] = {0: casted[:3]}
                    shared['ple'] = casted[3:5]
                    shared['out'] = dict(mla=casted[5], swa=casted[6])
            else:
                lat_c, kr_c = caches[0], caches[1]
                past = lat_c.shape[2]
                k_c, v_c = _mla_expand_call(lat_c.reshape(lat_c.shape[0], batch * past, -1),
                                            kr_c.reshape(kr_c.shape[0], batch * past, -1), j, w, tm=2 * ROWS)
                o = _mla_attn_sample(q, k_c, k, v_c, v, batch, past, seq, hb=w['heads'])
        else:
            q, kd, vd, kf, vf = _swa_proj(h, w, swa_tabs, tm)
            kv_heads = w['kv_heads']
            unhead = lambda t, rows: jnp.transpose(
                t.reshape(kv_heads, batch, seq, SWA_HEAD_DIM)[:, :, seq - rows:], (1, 2, 0, 3))
            if caches is None:
                o = _swa_attn(w['sinks'], q, kd, vd, batch, seq, tq=SWA_Q)
                keep = min(WINDOW, seq)
                sks.append(unhead(kf, keep))
                svs.append(unhead(vf, keep))
            else:
                ck, cv = caches[2][j], caches[3][j]
                keep = ck.shape[1]
                o = _swa_attn_sample(w['sinks'], q, _kv_dup(ck), kd, _kv_dup(cv), vd, batch, keep, seq)
                sks.append(jnp.concatenate([ck, unhead(kf, seq)], axis=1)[:, -keep:])
                svs.append(jnp.concatenate([cv, unhead(vf, seq)], axis=1)[:, -keep:])
        h = _oproj(h, o, shared['out'][kind], j, tm)
        ffn_cast = shared['ffn']
        h = _ffn(h, shared['raw']['ffn_g'][i:i + 1], *ffn_cast[i], 0, min(n, ROWS_FFN), FFN_COLS)
        cast = shared['raw']['ffn'] if (i + 1 not in ffn_cast and i + 1 < depth) else ()
        last = i == depth - 1
        h, casted, stacked = _ple(h, shared['raw']['ple_g'], shared['ple'][0], p_rows, i, shared['ple'][1],
                                  g_final, last, tm, cast, i + 1, (lats, krs) if last else ())
        if casted:
            ffn_cast[i + 1] = casted
    lat, kr = (t.reshape(n_mla, batch, seq, -1) for t in stacked)
    return h.reshape(batch, seq, d), lat, kr, jnp.stack(sks), jnp.stack(svs)


def kernel(x_prompt, x_sample, p_prompt, p_sample, cache_mla_latent, cache_mla_krope, state_swa_k, state_swa_v, g_attn_norm, w_mla_in, g_mla_q_a, w_mla_q_up, g_mla_kv_a, w_mla_kv_up, g_mla_q_nope, g_mla_q_rope, g_mla_k_nope, g_mla_k_rope, w_mla_out, w_swa_qkv, g_swa_q, g_swa_k, swa_sinks, w_swa_out, g_ffn_norm, w_ffn_gate, w_ffn_up, w_ffn_down, g_ple_norm, w_ple_gate, w_ple_proj, g_final):
    depth = g_attn_norm.shape[0]
    layers = []
    for i in range(depth):
        j = i // 2
        if i % 2 == 0:
            layers.append(('mla', _prep_mla(j, g_attn_norm[i], w_mla_in, g_mla_q_a, w_mla_q_up,
                                            g_mla_kv_a, w_mla_kv_up, g_mla_q_nope, g_mla_q_rope,
                                            g_mla_k_nope, g_mla_k_rope)))
        else:
            layers.append(('swa', _prep_swa(j, g_attn_norm[i], w_swa_qkv, g_swa_q, g_swa_k, swa_sinks)))
    shared = dict(raw=dict(ffn=(w_ffn_gate, w_ffn_up, w_ffn_down), ffn_g=g_ffn_norm[:, None, :],
                           ple_gate=w_ple_gate, ple_proj=w_ple_proj, ple_g=g_ple_norm[:, None, :],
                           mla_out=w_mla_out, swa_out=w_swa_out))
    gfin = g_final[None, :]

    seq = x_prompt.shape[1]
    t = x_sample.shape[1]
    past = cache_mla_latent.shape[2]
    pos_p = np.arange(seq)
    pos_s = past + np.arange(t)
    y_p, lat_p, kr_p, sk_p, sv_p = _trunk(x_prompt, p_prompt, pos_p, None, layers, shared, gfin, tm=ROWS)
    caches = (cache_mla_latent, cache_mla_krope, state_swa_k, state_swa_v)
    n_s = x_sample.shape[0] * t
    y_s, lat_s, kr_s, sk_s, sv_s = _trunk(x_sample, p_sample, pos_s, caches, layers, shared, gfin,
                                          tm=min(n_s, ROWS))
    return (y_p, y_s, lat_p, kr_p, sk_p, sv_p, lat_s, kr_s, sk_s, sv_s)
```

```python
import functools

import numpy as np
import jax
import jax.numpy as jnp
from jax import lax
from jax.experimental import pallas as pl
from jax.experimental.pallas import tpu as pltpu

F32 = jnp.float32
BF16 = jnp.bfloat16

EPS = 1e-6
CHUNK = 64
WINDOW = 128
MLA_THETA = 10000.0
SWA_THETA = 500000.0
MLA_NOPE = 128
MLA_ROPE = 64
MLA_V = 128
SWA_HEAD_DIM = 64
SWA_ROT = SWA_HEAD_DIM // 4

LANES = 128
MXU_WIDTH = 256
VMEM_LIMIT = 56 * 1024 * 1024

ROWS = 512
ROWS_FFN = 1024
FFN_COLS = 512
MLA_Q = 512
MLA_HEADS_PER_STEP = 4
SWA_Q = 512
NEG = -1e30
LOG2E = 1.4426950408889634


def _params(*sem):
    return pltpu.CompilerParams(dimension_semantics=sem, vmem_limit_bytes=VMEM_LIMIT)


def _resident(shape, layer=None):
    if layer is None:
        zeros = (0,) * len(shape)
        return pl.BlockSpec(shape, lambda *_: zeros, pipeline_mode=pl.Buffered(1))
    index = (layer,) + (0,) * (len(shape) - 1)
    return pl.BlockSpec((None,) + tuple(shape[1:]), lambda *_: index, pipeline_mode=pl.Buffered(1))


def _cast_item(a, steps, layer=None, col_split=1):
    nl, r, c = a.shape
    assert (r * col_split) % (steps * 16) == 0 and c % (col_split * LANES) == 0, (a.shape, steps)
    lead, first = (nl, 0) if layer is None else (1, layer)
    return dict(a=a, block=(lead, r * col_split // steps, c // col_split),
                in_idx=lambda s: (first, s // col_split, s % col_split),
                out_idx=lambda s: (0, s // col_split, s % col_split),
                out_shape=jax.ShapeDtypeStruct((lead, r, c), BF16))


def _cast_specs(items, step_of):
    ins = [pl.BlockSpec(it['block'], lambda *g, it=it: it['in_idx'](step_of(*g))) for it in items]
    outs = [pl.BlockSpec(it['block'], lambda *g, it=it: it['out_idx'](step_of(*g))) for it in items]
    return ins, outs, [it['out_shape'] for it in items]


def _cast_chunks(srcs, dsts):
    for src, dst in zip(srcs, dsts):
        dst[...] = src[...].astype(BF16)


def _rms(x, g):
    return x * lax.rsqrt(jnp.mean(x * x, axis=-1, keepdims=True) + EPS) * g


def _dot(a, b):
    return jnp.dot(a, b, preferred_element_type=F32)


def _dot_t(a, b):
    return lax.dot_general(a, b, (((1,), (1,)), ((), ())), preferred_element_type=F32)


def _rope_dup(x, g, tab):
    ss = jnp.sum(x * x, axis=-1, keepdims=True)
    y = x * lax.rsqrt(ss * (1.0 / LANES) + EPS) * g
    t = y * tab
    return t + pltpu.roll(t, LANES // 2, axis=1)


def _mla_heads(c_q, latb, krb, tab, wq_ref, wkn_ref, wv_ref, gqn, gqr, gkn, gsum_ref,
               q_ref, k_ref, v_ref, heads):
    gsum = gsum_ref[...]
    pair = 2 * LANES

    def inv_rms(x):
        return lax.rsqrt(_dot((x * x).astype(BF16), gsum) * (1.0 / LANES) + EPS)

    def finish_kv(p, xk, xv):
        kn = jnp.concatenate([_rms(xk[:, :LANES], gkn[:, :LANES]),
                              _rms(xk[:, LANES:], gkn[:, LANES:])], axis=1)
        for u in range(2):
            h = 2 * p + u
            sl = slice(u * LANES, (u + 1) * LANES)
            k_ref[h, :, 0:MLA_NOPE] = kn[:, sl].astype(BF16)
            k_ref[h, :, MLA_NOPE:MLA_NOPE + MLA_ROPE] = krb
            v_ref[h] = xv[:, sl].astype(BF16)

    def finish_q(p, xq):
        qn = xq[:, :pair] * inv_rms(xq[:, :pair]) * gqn
        t = xq[:, pair:] * inv_rms(xq[:, pair:]) * gqr * tab
        for u in range(2):
            h = 2 * p + u
            sl = slice(u * LANES, (u + 1) * LANES)
            q_ref[h, :, 0:MLA_NOPE] = qn[:, sl].astype(BF16)
            tu = t[:, sl]
            qr = tu + pltpu.roll(tu, LANES // 2, axis=1)
            q_ref[h, :, MLA_NOPE:MLA_NOPE + MLA_ROPE] = qr[:, :MLA_ROPE].astype(BF16)

    n_pairs = heads // 2
    dot_q = lambda p: _dot(c_q, wq_ref[:, p * 2 * pair:(p + 1) * 2 * pair])
    xq = None if c_q is None else dot_q(0)
    pending = None
    for p in range(n_pairs):
        xk = _dot(latb, wkn_ref[:, p * pair:(p + 1) * pair])
        xv = _dot(latb, wv_ref[:, p * pair:(p + 1) * pair])
        xq_next = dot_q(p + 1) if (c_q is not None and p + 1 < n_pairs) else None
        if xq is not None:
            finish_q(p, xq)
        if pending is not None:
            finish_kv(*pending)
        pending, xq = (p, xk, xv), xq_next
    finish_kv(*pending)


def _mla_proj_kernel(h_ref, gattn_ref, win_ref, gqa_ref, gkva_ref, gkr_ref, tab_ref, wq_ref,
                     gqn_ref, gqr_ref, wkn_ref, wv_ref, gkn_ref, gsum_ref,
                     lat_ref, kr_ref, q_ref, k_ref, v_ref, *, heads, q_lora, kv_lora, scale):
    hn = _rms(h_ref[...], gattn_ref[...]).astype(BF16)
    a = _dot(hn, win_ref[...])
    c_q = _rms(a[:, :q_lora], gqa_ref[...]).astype(BF16)
    lat = _rms(a[:, q_lora:q_lora + kv_lora], gkva_ref[...])
    lat_ref[...] = lat
    tab = tab_ref[...]
    kr = _rope_dup(a[:, q_lora + kv_lora:], gkr_ref[...], tab)
    kr_ref[...] = kr[:, :MLA_ROPE]
    _mla_heads(c_q, lat.astype(BF16), kr[:, :MLA_ROPE].astype(BF16), jnp.concatenate([tab, tab], axis=1),
               wq_ref, wkn_ref, wv_ref, gqn_ref[...] * scale, gqr_ref[...] * scale, gkn_ref[...],
               gsum_ref, q_ref, k_ref, v_ref, heads)


def _mla_expand_kernel(lat_ref, kr_ref, wkn_ref, wv_ref, gkn_ref, gsum_ref, k_ref, v_ref, *, heads):
    _mla_heads(None, lat_ref[...].astype(BF16), kr_ref[...].astype(BF16), None, None, wkn_ref, wv_ref,
               None, None, gkn_ref[...], gsum_ref, None, k_ref, v_ref, heads)


def _mla_proj(h, w, tab, tm):
    n, d = h.shape
    heads = w['heads']
    q_lora, kv_lora = w['g_q_a'].shape[1], w['g_kv_a'].shape[1]
    qk = MLA_NOPE + MLA_ROPE
    nt = tab.shape[0] // tm
    row = lambda i: (i, 0)
    hrow = lambda i: (0, i, 0)
    kern = functools.partial(_mla_proj_kernel, heads=heads, q_lora=q_lora, kv_lora=kv_lora,
                             scale=qk ** -0.5 * LOG2E)
    weights = (w['g_attn'], w['w_in'], w['g_q_a'], w['g_kv_a'], w['g_k_rope'], tab, w['w_q'],
               w['g_q_nope'], w['g_q_rope'], w['w_kn'], w['w_v'], w['g_k_nope'], w['gsum'])
    return pl.pallas_call(
        kern,
        grid=(n // tm,),
        in_specs=[
            pl.BlockSpec((tm, d), row),
            _resident(w['g_attn'].shape), _resident(w['w_in'].shape),
            _resident(w['g_q_a'].shape), _resident(w['g_kv_a'].shape), _resident(w['g_k_rope'].shape),
            pl.BlockSpec((tm, LANES), lambda i: (i % nt, 0)),
            _resident(w['w_q'].shape), _resident(w['g_q_nope'].shape), _resident(w['g_q_rope'].shape),
            _resident(w['w_kn'].shape), _resident(w['w_v'].shape), _resident(w['g_k_nope'].shape),
            _resident(w['gsum'].shape),
        ],
        out_specs=[
            pl.BlockSpec((tm, kv_lora), row),
            pl.BlockSpec((tm, MLA_ROPE), row),
            pl.BlockSpec((heads, tm, qk), hrow),
            pl.BlockSpec((heads, tm, qk), hrow),
            pl.BlockSpec((heads, tm, MLA_V), hrow),
        ],
        out_shape=[
            jax.ShapeDtypeStruct((n, kv_lora), F32),
            jax.ShapeDtypeStruct((n, MLA_ROPE), F32),
            jax.ShapeDtypeStruct((heads, n, qk), BF16),
            jax.ShapeDtypeStruct((heads, n, qk), BF16),
            jax.ShapeDtypeStruct((heads, n, MLA_V), BF16),
        ],
        compiler_params=_params("parallel"),
        name="mla_proj",
    )(h, *weights)


def _mla_expand_call(lat, kr, layer, w, tm):
    _, n, kv_lora = lat.shape
    heads = w['heads']
    qk = MLA_NOPE + MLA_ROPE
    hrow = lambda i: (0, i, 0)
    return pl.pallas_call(
        functools.partial(_mla_expand_kernel, heads=heads),
        grid=(n // tm,),
        in_specs=[pl.BlockSpec((None, tm, kv_lora), lambda i: (layer, i, 0)),
                  pl.BlockSpec((None, tm, MLA_ROPE), lambda i: (layer, i, 0)),
                  _resident(w['w_kn'].shape), _resident(w['w_v'].shape),
                  _resident(w['g_k_nope'].shape), _resident(w['gsum'].shape)],
        out_specs=[pl.BlockSpec((heads, tm, qk), hrow), pl.BlockSpec((heads, tm, MLA_V), hrow)],
        out_shape=[jax.ShapeDtypeStruct((heads, n, qk), BF16),
                   jax.ShapeDtypeStruct((heads, n, MLA_V), BF16)],
        compiler_params=_params("parallel"),
        name="mla_expand",
    )(lat, kr, w['w_kn'], w['w_v'], w['g_k_nope'], w['gsum'])


def _mla_attn_kernel(q_ref, k_ref, v_ref, *rest, hb, tq, n_cast):
    cast_in, o_ref, cast_out, s_ref = rest[:n_cast], rest[n_cast], rest[n_cast + 1:-1], rest[-1]
    _cast_chunks(cast_in, cast_out)
    i = pl.program_id(2)
    kc = lax.broadcasted_iota(jnp.int32, (tq, tq), 0) // CHUNK
    qc = lax.broadcasted_iota(jnp.int32, (tq, tq), 1) // CHUNK
    diag_bias = jnp.where(kc <= qc, 0.0, NEG).astype(F32)
    qs = [q_ref[h] for h in range(hb)]

    def scores(j, slot):
        start = pl.multiple_of(j * tq, tq)
        mx = []
        for h in range(hb):
            s = _dot_t(k_ref[h, pl.ds(start, tq), :], qs[h])
            s_ref[slot, h] = s
            mx.append(jnp.max(s, axis=0, keepdims=True))
        return tuple(mx)

    def update(j, slot, carry, mx):
        start = pl.multiple_of(j * tq, tq)
        ps, stats = [], []
        for h in range(hb):
            m, l, _ = carry[h]
            s = s_ref[slot, h]
            m_new = jnp.maximum(m, mx[h])
            alpha = jnp.exp2(m - m_new)
            p = jnp.exp2(s - m_new)
            stats.append((m_new, alpha * l + jnp.sum(p, axis=0, keepdims=True), alpha))
            ps.append(p.astype(BF16))
        out = []
        for h in range(hb):
            pv = lax.dot_general(v_ref[h, pl.ds(start, tq), :], ps[h],
                                 (((0,), (0,)), ((), ())), preferred_element_type=F32)
            m_new, l, alpha = stats[h]
            out.append((m_new, l, alpha * carry[h][2] + pv))
        return tuple(out)

    def update_diag(j, slot, carry):
        start = pl.multiple_of(j * tq, tq)
        half = tq // 2
        spans = ((slice(0, half), half), (slice(half, tq), tq))
        ps, stats = [], []
        for h in range(hb):
            m, l, _ = carry[h]
            m_new, l_new, alpha, p_h = [], [], [], []
            for cols, keys in spans:
                s = s_ref[slot, h, 0:keys, cols] + diag_bias[0:keys, cols]
                mc = jnp.maximum(m[:, cols], jnp.max(s, axis=0, keepdims=True))
                a = jnp.exp2(m[:, cols] - mc)
                p = jnp.exp2(s - mc)
                m_new.append(mc)
                alpha.append(a)
                l_new.append(a * l[:, cols] + jnp.sum(p, axis=0, keepdims=True))
                p_h.append(p.astype(BF16))
            stats.append((jnp.concatenate(m_new, axis=1), jnp.concatenate(l_new, axis=1), alpha))
            ps.append(p_h)
        out = []
        for h in range(hb):
            m_new, l_new, alpha = stats[h]
            acc = carry[h][2]
            parts = []
            for (cols, keys), a, p in zip(spans, alpha, ps[h]):
                pv = lax.dot_general(v_ref[h, pl.ds(start, keys), :], p,
                                     (((0,), (0,)), ((), ())), preferred_element_type=F32)
                parts.append(a * acc[:, cols] + pv)
            out.append((m_new, l_new, jnp.concatenate(parts, axis=1)))
        return tuple(out)

    def finish(carry):
        for h in range(hb):
            _, l, acc = carry[h]
            o_ref[:, h * MLA_V:(h + 1) * MLA_V] = (acc / l).T.astype(BF16)

    def pair(t, state):
        carry, mx0 = state
        mx1 = scores(2 * t + 1, 1)
        carry = update(2 * t, 0, carry, mx0)
        mx0 = scores(2 * t + 2, 0)
        return update(2 * t + 1, 1, carry, mx1), mx0

    init = tuple((jnp.full((1, tq), NEG, F32), jnp.zeros((1, tq), F32), jnp.zeros((MLA_V, tq), F32))
                 for _ in range(hb))
    carry, mx0 = lax.fori_loop(0, i // 2, pair, (init, scores(0, 0)))

    @pl.when(i % 2 == 0)
    def _():
        finish(update_diag(i, 0, carry))

    @pl.when(i % 2 == 1)
    def _():
        scores(i, 1)
        finish(update_diag(i, 1, update(i - 1, 0, carry, mx0)))


def _mla_attn_steps(batch, heads, seq, tq, hb):
    return batch * (heads // hb) * (seq // tq)


def _mla_attn(q, k, v, batch, seq, tq, hb, cast=()):
    heads, n, qk = q.shape
    nq = seq // tq
    ng = heads // hb
    kv_map = lambda b, g, i: (g, b, 0)
    cast_in, cast_out, cast_shapes = _cast_specs(cast, lambda b, g, i: (b * ng + g) * nq + i)
    out = pl.pallas_call(
        functools.partial(_mla_attn_kernel, hb=hb, tq=tq, n_cast=len(cast)),
        grid=(batch, ng, nq),
        in_specs=[pl.BlockSpec((hb, tq, qk), lambda b, g, i: (g, b * nq + i, 0)),
                  pl.BlockSpec((hb, seq, qk), kv_map),
                  pl.BlockSpec((hb, seq, MLA_V), kv_map)] + cast_in,
        out_specs=[pl.BlockSpec((tq, hb * MLA_V), lambda b, g, i: (b * nq + i, g))] + cast_out,
        out_shape=[jax.ShapeDtypeStruct((n, heads * MLA_V), BF16)] + cast_shapes,
        scratch_shapes=[pltpu.VMEM((2, hb, tq, tq), F32)],
        compiler_params=_params("parallel", "parallel", "arbitrary"),
        name="mla_attn",
    )(q, k, v, *[it['a'] for it in cast])
    return out[0], tuple(out[1:])


def _mla_attn_sample_kernel(q_ref, kc_ref, kn_ref, vc_ref, vn_ref, o_ref, *, hb):
    ss = [(_dot_t(kc_ref[h], q_ref[h]), _dot_t(kn_ref[h], q_ref[h])) for h in range(hb)]
    ps = []
    for s_c, s_n in ss:
        m = jnp.maximum(jnp.max(s_c, axis=0, keepdims=True), jnp.max(s_n, axis=0, keepdims=True))
        p_c = jnp.exp2(s_c - m)
        p_n = jnp.exp2(s_n - m)
        l = jnp.sum(p_c, axis=0, keepdims=True) + jnp.sum(p_n, axis=0, keepdims=True)
        ps.append((p_c.astype(BF16), p_n.astype(BF16), l))
    tn = (((0,), (0,)), ((), ()))
    for h, (p_c, p_n, l) in enumerate(ps):
        acc = (lax.dot_general(vc_ref[h], p_c, tn, preferred_element_type=F32)
               + lax.dot_general(vn_ref[h], p_n, tn, preferred_element_type=F32))
        o_ref[:, h * MLA_V:(h + 1) * MLA_V] = (acc / l).T.astype(BF16)


def _mla_attn_sample(q, k_cache, k_new, v_cache, v_new, batch, past, t, hb):
    heads, n, qk = q.shape
    q_pos = past + np.arange(t)
    k_pos = np.arange(past + t)
    assert np.all((k_pos // CHUNK)[None, :] <= (q_pos // CHUNK)[:, None])
    m3 = lambda b, g: (g, b, 0)
    return pl.pallas_call(
        functools.partial(_mla_attn_sample_kernel, hb=hb),
        grid=(batch, heads // hb),
        in_specs=[pl.BlockSpec((hb, t, qk), m3), pl.BlockSpec((hb, past, qk), m3),
                  pl.BlockSpec((hb, t, qk), m3), pl.BlockSpec((hb, past, MLA_V), m3),
                  pl.BlockSpec((hb, t, MLA_V), m3)],
        out_specs=pl.BlockSpec((t, hb * MLA_V), lambda b, g: (b, g)),
        out_shape=jax.ShapeDtypeStruct((n, heads * MLA_V), BF16),
        compiler_params=_params("parallel", "parallel"),
        name="mla_attn_sample",
    )(q, k_cache, k_new, v_cache, v_new)


def _swa_proj_kernel(h_ref, gattn_ref, w_ref, gq_ref, gk_ref, tc_ref, ts1_ref, ts2_ref, gmat_ref,
                     q_ref, kd_ref, vd_ref, kf_ref, vf_ref, *, q_cols, kv_heads, scale):
    hn = _rms(h_ref[...], gattn_ref[...]).astype(BF16)
    tc, ts1, ts2 = tc_ref[...], ts1_ref[...], ts2_ref[...]
    gmat = gmat_ref[...]

    def rope(x, ss, g):
        y = x * lax.rsqrt(ss * (1.0 / SWA_HEAD_DIM) + EPS) * g
        return (y * tc + pltpu.roll(y, LANES - SWA_ROT // 2, axis=1) * ts1
                + pltpu.roll(y, SWA_ROT // 2, axis=1) * ts2)

    gq = gq_ref[...] * scale
    gk = gk_ref[...]
    n_q = q_cols // 2
    n_kv = kv_heads // 4
    lo = lax.broadcasted_iota(jnp.int32, (1, LANES), 1) < SWA_HEAD_DIM

    def put_dup(col, first_head, dup_ref, flat_ref):
        swapped = pltpu.roll(col, SWA_HEAD_DIM, axis=1)
        for u, d in enumerate((jnp.where(lo, col, swapped), jnp.where(lo, swapped, col))):
            dup_ref[first_head + u] = d.astype(BF16)
            flat_ref[first_head + u] = d[:, :SWA_HEAD_DIM]

    def finish(c, x2):
        if c < n_q + n_kv:
            ss = _dot((x2 * x2).astype(BF16), gmat)
        for u in range(2):
            sl = slice(u * LANES, (u + 1) * LANES)
            if c < n_q:
                col = 2 * c + u
                q_ref[:, col * LANES:(col + 1) * LANES] = rope(x2[:, sl], ss[:, sl], gq).astype(BF16)
            elif c < n_q + n_kv:
                put_dup(rope(x2[:, sl], ss[:, sl], gk), 4 * (c - n_q) + 2 * u, kd_ref, kf_ref)
            else:
                put_dup(x2[:, sl], 4 * (c - n_q - n_kv) + 2 * u, vd_ref, vf_ref)

    pending = None
    for c in range(n_q + 2 * n_kv):
        x2 = _dot(hn, w_ref[:, c * MXU_WIDTH:(c + 1) * MXU_WIDTH])
        if pending is not None:
            finish(*pending)
        pending = (c, x2)
    finish(*pending)


def _swa_proj(h, w, tabs, tm):
    n, d = h.shape
    q_heads, kv_heads = w['q_heads'], w['kv_heads']
    q_cols = q_heads * SWA_HEAD_DIM // LANES
    nt = tabs[0].shape[0] // tm
    row = lambda i: (i, 0)
    hrow = lambda i: (0, i, 0)
    tspec = pl.BlockSpec((tm, LANES), lambda i: (i % nt, 0))
    kern = functools.partial(_swa_proj_kernel, q_cols=q_cols, kv_heads=kv_heads,
                             scale=SWA_HEAD_DIM ** -0.5 * LOG2E)
    return pl.pallas_call(
        kern,
        grid=(n // tm,),
        in_specs=[pl.BlockSpec((tm, d), row), _resident(w['g_attn'].shape),
                  _resident(w['w_qkv'].shape), _resident(w['g_q'].shape), _resident(w['g_k'].shape),
                  tspec, tspec, tspec, _resident(w['gmat'].shape)],
        out_specs=[pl.BlockSpec((tm, q_cols * LANES), row),
                   pl.BlockSpec((kv_heads, tm, LANES), hrow),
                   pl.BlockSpec((kv_heads, tm, LANES), hrow),
                   pl.BlockSpec((kv_heads, tm, SWA_HEAD_DIM), hrow),
                   pl.BlockSpec((kv_heads, tm, SWA_HEAD_DIM), hrow)],
        out_shape=[jax.ShapeDtypeStruct((n, q_cols * LANES), BF16),
                   jax.ShapeDtypeStruct((kv_heads, n, LANES), BF16),
                   jax.ShapeDtypeStruct((kv_heads, n, LANES), BF16),
                   jax.ShapeDtypeStruct((kv_heads, n, SWA_HEAD_DIM), F32),
                   jax.ShapeDtypeStruct((kv_heads, n, SWA_HEAD_DIM), F32)],
        compiler_params=_params("parallel"),
        name="swa_proj",
    )(h, w['g_attn'], w['w_qkv'], w['g_q'], w['g_k'], *tabs, w['gmat'])


def _swa_attn_kernel(sink_ref, q_ref, kp_ref, kc_ref, vp_ref, vc_ref, o_ref, *,
                     kv_heads, group, banded):
    tq = q_ref.shape[0]
    n_p, n_c = kp_ref.shape[1], kc_ref.shape[1]
    n_k = n_p + n_c
    lane = lax.broadcasted_iota(jnp.int32, (1, LANES), 1)
    row = lax.broadcasted_iota(jnp.int32, (LANES, 1), 0)
    if banded:
        assert n_p == WINDOW == LANES
        tiles = [(t * LANES, LANES, t * LANES, 2 * LANES) for t in range(tq // LANES)]
    else:
        tiles = [(0, tq, 0, n_k)]
    biases = []
    for q0, qn, k0, kn in tiles:
        if not banded:
            biases.append(None)
            continue
        win = WINDOW // CHUNK
        kc = (lax.broadcasted_iota(jnp.int32, (kn, qn), 0) + k0) // CHUNK - n_p // CHUNK
        qc = (lax.broadcasted_iota(jnp.int32, (kn, qn), 1) + q0) // CHUNK
        ok = (kc <= qc) & (kc >= qc - win) & ((kc >= 0) | (pl.program_id(1) > 0))
        biases.append(jnp.where(ok, 0.0, NEG).astype(F32))
    zero = jnp.zeros((), BF16)
    half = group // 2
    for kh in range(kv_heads):
        k_all = jnp.concatenate([kp_ref[kh], kc_ref[kh]], axis=0)
        v_t = jnp.concatenate([vp_ref[kh], vc_ref[kh]], axis=0).T
        ks = (jnp.where(lane < SWA_HEAD_DIM, k_all, zero), jnp.where(lane < SWA_HEAD_DIM, zero, k_all))
        vs = (jnp.where(row < SWA_HEAD_DIM, v_t, zero), jnp.where(row < SWA_HEAD_DIM, zero, v_t))
        cols = [q_ref[:, (kh * half + c) * LANES:(kh * half + c + 1) * LANES] for c in range(half)]
        for (q0, qn, k0, kn), bias in zip(tiles, biases):
            ss = [_dot_t(ks[u][k0:k0 + kn], cols[c][q0:q0 + qn])
                  for c in range(half) for u in range(2)]
            es, rdens = [], []
            for c in range(half):
                for u in range(2):
                    s = ss[2 * c + u]
                    if bias is not None:
                        s = s + bias
                    sink = sink_ref[2 * (kh * half + c) + u] * LOG2E
                    m = jnp.maximum(jnp.max(s, axis=0, keepdims=True), sink)
                    e = jnp.exp2(s - m)
                    rdens.append(1.0 / (jnp.sum(e, axis=0, keepdims=True) + jnp.exp2(sink - m)))
                    es.append(e.astype(BF16))
            for c in range(half):
                o_t = (_dot(vs[0][:, k0:k0 + kn], es[2 * c]) * rdens[2 * c]
                       + _dot(vs[1][:, k0:k0 + kn], es[2 * c + 1]) * rdens[2 * c + 1])
                col = kh * half + c
                o_ref[q0:q0 + qn, col * LANES:(col + 1) * LANES] = o_t.T.astype(BF16)


def _swa_attn(sinks, q, kd, vd, batch, seq, tq):
    n, dq = q.shape
    kv_heads = kd.shape[0]
    group = (dq // SWA_HEAD_DIM) // kv_heads
    nq = seq // tq
    per = tq // WINDOW
    prev = lambda b, i: (0, jnp.maximum(i * per - 1, 0) + b * nq * per, 0)
    cur = lambda b, i: (0, b * nq + i, 0)
    return pl.pallas_call(
        functools.partial(_swa_attn_kernel, kv_heads=kv_heads, group=group, banded=True),
        grid=(batch, nq),
        in_specs=[pl.BlockSpec(memory_space=pltpu.SMEM),
                  pl.BlockSpec((tq, dq), lambda b, i: (b * nq + i, 0)),
                  pl.BlockSpec((kv_heads, WINDOW, LANES), prev),
                  pl.BlockSpec((kv_heads, tq, LANES), cur),
                  pl.BlockSpec((kv_heads, WINDOW, LANES), prev),
                  pl.BlockSpec((kv_heads, tq, LANES), cur)],
        out_specs=pl.BlockSpec((tq, dq), lambda b, i: (b * nq + i, 0)),
        out_shape=jax.ShapeDtypeStruct((n, dq), BF16),
        compiler_params=_params("parallel", "arbitrary"),
        name="swa_attn",
    )(sinks, q, kd, kd, vd, vd)


def _swa_attn_sample(sinks, q, kd_cache, kd_new, vd_cache, vd_new, batch, keep, t):
    n, dq = q.shape
    kv_heads = kd_new.shape[0]
    group = (dq // SWA_HEAD_DIM) // kv_heads
    blk = lambda b: (0, b, 0)
    return pl.pallas_call(
        functools.partial(_swa_attn_kernel, kv_heads=kv_heads, group=group, banded=False),
        grid=(batch,),
        in_specs=[pl.BlockSpec(memory_space=pltpu.SMEM),
                  pl.BlockSpec((t, dq), lambda b: (b, 0)),
                  pl.BlockSpec((kv_heads, keep, LANES), blk),
                  pl.BlockSpec((kv_heads, t, LANES), blk),
                  pl.BlockSpec((kv_heads, keep, LANES), blk),
                  pl.BlockSpec((kv_heads, t, LANES), blk)],
        out_specs=pl.BlockSpec((t, dq), lambda b: (b, 0)),
        out_shape=jax.ShapeDtypeStruct((n, dq), BF16),
        compiler_params=_params("parallel"),
        name="swa_attn_sample",
    )(sinks, q, kd_cache, kd_new, vd_cache, vd_new)


def _oproj_kernel(h_ref, o_ref, w_ref, out_ref):
    out_ref[...] = h_ref[...] + _dot(o_ref[...], w_ref[...])


def _oproj(h, o, w, layer, tm):
    n, d = h.shape
    row = lambda i: (i, 0)
    return pl.pallas_call(
        _oproj_kernel,
        grid=(n // tm,),
        in_specs=[pl.BlockSpec((tm, d), row), pl.BlockSpec((tm, o.shape[1]), row),
                  _resident(w.shape, layer)],
        out_specs=pl.BlockSpec((tm, d), row),
        out_shape=jax.ShapeDtypeStruct((n, d), F32),
        compiler_params=_params("parallel"),
        name="attn_out_proj",
    )(h, o, w)


def _ffn_kernel(h_ref, g_ref, wg_ref, wu_ref, wd_ref, out_ref, hn_ref):
    @pl.when(pl.program_id(1) == 0)
    def _():
        x = h_ref[...]
        hn_ref[...] = _rms(x, g_ref[...]).astype(BF16)
        out_ref[...] = x

    hn = hn_ref[...]
    w = wg_ref.shape[1] // 2
    acts = []
    for c in range(2):
        a = _dot(hn, wg_ref[:, c * w:(c + 1) * w])
        b = _dot(hn, wu_ref[:, c * w:(c + 1) * w])
        acts.append((a * jax.nn.sigmoid(a) * b).astype(BF16))
    out_ref[...] += _dot(acts[0], wd_ref[0:w, :]) + _dot(acts[1], wd_ref[w:2 * w, :])


def _ffn(h, g, wg, wu, wd, layer, tm, tf):
    n, d = h.shape
    f = wg.shape[2]
    row = lambda i, j: (i, 0)
    return pl.pallas_call(
        _ffn_kernel,
        grid=(n // tm, f // tf),
        in_specs=[pl.BlockSpec((tm, d), row), pl.BlockSpec((None, 1, d), lambda i, j: (layer, 0, 0)),
                  pl.BlockSpec((None, d, tf), lambda i, j: (layer, 0, j)),
                  pl.BlockSpec((None, d, tf), lambda i, j: (layer, 0, j)),
                  pl.BlockSpec((None, tf, d), lambda i, j: (layer, j, 0))],
        out_specs=pl.BlockSpec((tm, d), row),
        out_shape=jax.ShapeDtypeStruct((n, d), F32),
        scratch_shapes=[pltpu.VMEM((tm, d), BF16)],
        compiler_params=_params("parallel", "arbitrary"),
        name="swiglu_ffn",
    )(h, g, wg, wu, wd)


FFN_RING = 3


def _ffn_stream_kernel(h_ref, g_ref, wg_hbm, wu_hbm, wd_hbm, out_ref, wg_buf, wu_buf, wd_buf, sems,
                       *, layer, tf, n_tiles):
    def copies(j):
        slot = j % FFN_RING
        cols = pl.ds(j * tf, tf)
        return (pltpu.make_async_copy(wg_hbm.at[layer, :, cols], wg_buf.at[slot], sems.at[0, slot]),
                pltpu.make_async_copy(wu_hbm.at[layer, :, cols], wu_buf.at[slot], sems.at[1, slot]),
                pltpu.make_async_copy(wd_hbm.at[layer, cols, :], wd_buf.at[slot], sems.at[2, slot]))

    for j in range(min(FFN_RING - 1, n_tiles)):
        for c in copies(j):
            c.start()
    x = h_ref[...]
    hn = _rms(x, g_ref[...]).astype(BF16)
    acc = x
    for j in range(n_tiles):
        nxt = j + FFN_RING - 1
        if nxt < n_tiles:
            for c in copies(nxt):
                c.start()
        for c in copies(j):
            c.wait()
        slot = j % FFN_RING
        a = _dot(hn, wg_buf[slot])
        b = _dot(hn, wu_buf[slot])
        acc = acc + _dot((a * jax.nn.sigmoid(a) * b).astype(BF16), wd_buf[slot])
    out_ref[...] = acc


def _ffn_stream(h, g, wg, wu, wd, layer, tf):
    n, d = h.shape
    f = wg.shape[2]
    kern = functools.partial(_ffn_stream_kernel, layer=layer, tf=tf, n_tiles=f // tf)
    return pl.pallas_call(
        kern,
        in_specs=[pl.BlockSpec((n, d), lambda: (0, 0)),
                  pl.BlockSpec((None, 1, d), lambda: (layer, 0, 0)),
                  pl.BlockSpec(memory_space=pl.ANY), pl.BlockSpec(memory_space=pl.ANY),
                  pl.BlockSpec(memory_space=pl.ANY)],
        out_specs=pl.BlockSpec((n, d), lambda: (0, 0)),
        out_shape=jax.ShapeDtypeStruct((n, d), F32),
        scratch_shapes=[pltpu.VMEM((FFN_RING, d, tf), BF16), pltpu.VMEM((FFN_RING, d, tf), BF16),
                        pltpu.VMEM((FFN_RING, tf, d), BF16), pltpu.SemaphoreType.DMA((3, FFN_RING))],
        compiler_params=pltpu.CompilerParams(vmem_limit_bytes=VMEM_LIMIT),
        name="swiglu_ffn_stream",
    )(h, g, wg, wu, wd)


def _ple_kernel(h_ref, g_ref, wgate_ref, p_ref, wproj_ref, gfin_ref, *rest, final, n_cast, stack):
    n_rows = sum(stack)
    cast_in, rows_in = rest[:n_cast], rest[n_cast:n_cast + n_rows]
    out_ref = rest[n_cast + n_rows]
    cast_out = rest[n_cast + n_rows + 1:2 * n_cast + n_rows + 1]
    stacked = rest[2 * n_cast + n_rows + 1:]
    first = 0
    for dst, count in zip(stacked, stack):
        for s in range(count):
            dst[s] = rows_in[first + s][...]
        first += count
    x = h_ref[...]
    gate = jax.nn.sigmoid(_dot(_rms(x, g_ref[...]).astype(BF16), wgate_ref[...]))
    y = x + gate * _dot(p_ref[...].astype(BF16), wproj_ref[...])
    if final:
        y = _rms(y, gfin_ref[...])
    out_ref[...] = y
    _cast_chunks(cast_in, cast_out)


def _ple(h, g, wgate, p, layer, wproj, gfin, final, tm, cast=(), cast_layer=0, stack=()):
    n, d = h.shape
    steps = n // tm
    row = lambda i: (i, 0)
    items = [_cast_item(a, steps, cast_layer) for a in cast]
    cast_specs, cast_out_specs, cast_shapes = _cast_specs(items, lambda i: i)
    rows = [a for group in stack for a in group]
    row_specs = [pl.BlockSpec((tm, a.shape[1]), row) for a in rows]
    stack_specs = [pl.BlockSpec((len(gr), tm, gr[0].shape[1]), lambda i: (0, i, 0)) for gr in stack]
    stack_shapes = [jax.ShapeDtypeStruct((len(gr), n, gr[0].shape[1]), gr[0].dtype) for gr in stack]
    out = pl.pallas_call(
        functools.partial(_ple_kernel, final=final, n_cast=len(cast), stack=tuple(len(gr) for gr in stack)),
        grid=(steps,),
        in_specs=[pl.BlockSpec((tm, d), row), _resident(g.shape, layer), _resident(wgate.shape, layer),
                  pl.BlockSpec((None, tm, p.shape[2]), lambda i: (layer, i, 0)),
                  _resident(wproj.shape, layer), _resident(gfin.shape)] + cast_specs + row_specs,
        out_specs=[pl.BlockSpec((tm, d), row)] + cast_out_specs + stack_specs,
        out_shape=[jax.ShapeDtypeStruct((n, d), F32)] + cast_shapes + stack_shapes,
        compiler_params=_params("parallel"),
        name="ple_embed",
    )(h, g, wgate, p, wproj, gfin, *cast, *rows)
    n_c = len(cast)
    return out[0], tuple(out[1:1 + n_c]), tuple(out[1 + n_c:])


def _angles(pos, r, theta):
    inv = np.power(np.float64(theta), -np.arange(0, r, 2, dtype=np.float64) / r)
    ang = pos.astype(np.float64)[:, None] * inv[None, :]
    return np.cos(ang).astype(np.float32), np.sin(ang).astype(np.float32)


def _mla_table(pos):
    cos, sin = _angles(pos, MLA_ROPE, MLA_THETA)
    return np.concatenate([cos, cos, -sin, sin], axis=-1)


def _swa_tables(pos):
    cos, sin = _angles(pos, SWA_ROT, SWA_THETA)
    s = cos.shape[0]
    rest = SWA_HEAD_DIM - SWA_ROT
    one = np.ones((s, rest), np.float32)
    zero = np.zeros((s, rest), np.float32)
    zh = np.zeros_like(sin)
    tc = np.concatenate([cos, cos, one], axis=-1)
    ts1 = np.concatenate([-sin, zh, zero], axis=-1)
    ts2 = np.concatenate([zh, sin, zero], axis=-1)
    return tuple(np.concatenate([t, t], axis=-1) for t in (tc, ts1, ts2))


def _tile_rows(tab, rows):
    tab = tab if tab.shape[0] >= rows else np.tile(tab, (rows // tab.shape[0], 1))
    return jnp.asarray(tab)


def _swap_halves(x):
    half = x.shape[-1] // 2
    return jnp.concatenate([x[..., half:], x[..., :half]], axis=-1)


def _prep_mla(j, g_attn_norm_i, w_mla_in, g_mla_q_a, w_mla_q_up, g_mla_kv_a, w_mla_kv_up, g_mla_q_nope,
              g_mla_q_rope, g_mla_k_nope, g_mla_k_rope):
    q_lora, kv_lora = g_mla_q_a.shape[1], g_mla_kv_a.shape[1]
    qk = MLA_NOPE + MLA_ROPE
    heads = w_mla_q_up.shape[2] // qk
    w_in = w_mla_in[j]
    k_r = w_in[:, q_lora + kv_lora:]
    w_in = jnp.concatenate([w_in, _swap_halves(k_r)], axis=-1)
    wq = w_mla_q_up[j].reshape(q_lora, heads, qk)
    rope = wq[:, :, MLA_NOPE:]
    wq = jnp.concatenate([wq[:, :, :MLA_NOPE].reshape(q_lora, heads // 2, 2 * MLA_NOPE),
                          jnp.concatenate([rope, _swap_halves(rope)], axis=-1).reshape(q_lora, heads // 2, 2 * LANES)],
                         axis=-1).reshape(q_lora, heads * 2 * LANES)
    wkv = w_mla_kv_up[j].reshape(kv_lora, heads, MLA_NOPE + MLA_V)
    dup = lambda g: jnp.concatenate([g, _swap_halves(g)], axis=-1)[None, :]
    two = lambda g: jnp.concatenate([g, g], axis=-1)
    gsum = np.kron(np.eye(2), np.ones((LANES, LANES)))
    return dict(
        heads=heads, gsum=jnp.asarray(gsum, BF16),
        g_attn=g_attn_norm_i[None, :],
        w_in=w_in.astype(BF16),
        g_q_a=g_mla_q_a[j][None, :], g_kv_a=g_mla_kv_a[j][None, :],
        g_k_rope=dup(g_mla_k_rope[j]), g_q_rope=two(dup(g_mla_q_rope[j])),
        g_q_nope=two(g_mla_q_nope[j][None, :]), g_k_nope=two(g_mla_k_nope[j][None, :]),
        w_q=wq.astype(BF16),
        w_kn=wkv[:, :, :MLA_NOPE].reshape(kv_lora, heads * MLA_NOPE).astype(BF16),
        w_v=wkv[:, :, MLA_NOPE:].reshape(kv_lora, heads * MLA_V).astype(BF16),
    )


def _prep_swa(j, g_attn_norm_i, w_swa_qkv, g_swa_q, g_swa_k, swa_sinks):
    q_heads = swa_sinks.shape[1]
    w = w_swa_qkv[j]
    kv_heads = (w.shape[1] - q_heads * SWA_HEAD_DIM) // (2 * SWA_HEAD_DIM)
    assert kv_heads % 4 == 0 and q_heads % 4 == 0
    gmat = np.kron(np.eye(2 * LANES // SWA_HEAD_DIM), np.ones((SWA_HEAD_DIM, SWA_HEAD_DIM)))
    dup = lambda g: jnp.concatenate([g, g], axis=-1)[None, :]
    return dict(
        q_heads=q_heads, kv_heads=kv_heads,
        g_attn=g_attn_norm_i[None, :],
        w_qkv=w.astype(BF16),
        g_q=dup(g_swa_q[j]), g_k=dup(g_swa_k[j]),
        gmat=jnp.asarray(gmat, BF16),
        sinks=swa_sinks[j],
    )


def _kv_dup(t):
    b, l, kv, d = t.shape
    t = jnp.transpose(t, (2, 0, 1, 3)).reshape(kv, b * l, d)
    return jnp.concatenate([t, t], axis=-1).astype(BF16)


def _first_casts(shared, steps):
    raw = shared['raw']

    def item(a, layer=None):
        split = next(s for s in (1, 2, 4, 8, 16)
                     if (a.shape[1] * s) % (steps * 16) == 0 and a.shape[2] % (s * LANES) == 0)
        return _cast_item(a, steps, layer, split)

    return ([item(a, 0) for a in raw['ffn']]
            + [item(raw[k]) for k in ('ple_gate', 'ple_proj', 'mla_out', 'swa_out')])


def _trunk(x, p, pos, caches, layers, shared, g_final, tm):
    batch, seq, d = x.shape
    n = batch * seq
    h = x.reshape(n, d)
    rows = max(tm, seq)
    mla_tab = _tile_rows(_mla_table(pos), rows)
    swa_tabs = tuple(_tile_rows(t, rows) for t in _swa_tables(pos))
    depth = len(layers)
    n_mla = sum(kind == 'mla' for kind, _ in layers)
    p_rows = p.reshape(depth, n, -1)
    lats, krs, sks, svs = [], [], [], []
    for i, (kind, w) in enumerate(layers):
        j = i // 2
        if kind == 'mla':
            lat, kr, q, k, v = _mla_proj(h, w, mla_tab, tm)
            lats.append(lat)
            krs.append(kr)
            if caches is None:
                tq, hb = MLA_Q, MLA_HEADS_PER_STEP
                first = 'ple' not in shared
                cast = _first_casts(shared, _mla_attn_steps(batch, w['heads'], seq, tq, hb)) if first else ()
                o, casted = _mla_attn(q, k, v, batch, seq, tq=tq, hb=hb, cast=cast)
                if first:
                    shared['ffn'] = {0: casted[:3]}
                    shared['ple'] = casted[3:5]
                    shared['out'] = dict(mla=casted[5], swa=casted[6])
            else:
                lat_c, kr_c = caches[0], caches[1]
                past = lat_c.shape[2]
                k_c, v_c = _mla_expand_call(lat_c.reshape(lat_c.shape[0], batch * past, -1),
                                            kr_c.reshape(kr_c.shape[0], batch * past, -1), j, w, tm=2 * ROWS)
                o = _mla_attn_sample(q, k_c, k, v_c, v, batch, past, seq, hb=w['heads'])
        else:
            q, kd, vd, kf, vf = _swa_proj(h, w, swa_tabs, tm)
            kv_heads = w['kv_heads']
            unhead = lambda t, rows: jnp.transpose(
                t.reshape(kv_heads, batch, seq, SWA_HEAD_DIM)[:, :, seq - rows:], (1, 2, 0, 3))
            if caches is None:
                o = _swa_attn(w['sinks'], q, kd, vd, batch, seq, tq=SWA_Q)
                keep = min(WINDOW, seq)
                sks.append(unhead(kf, keep))
                svs.append(unhead(vf, keep))
            else:
                ck, cv = caches[2][j], caches[3][j]
                keep = ck.shape[1]
                o = _swa_attn_sample(w['sinks'], q, _kv_dup(ck), kd, _kv_dup(cv), vd, batch, keep, seq)
                sks.append(jnp.concatenate([ck, unhead(kf, seq)], axis=1)[:, -keep:])
                svs.append(jnp.concatenate([cv, unhead(vf, seq)], axis=1)[:, -keep:])
        h = _oproj(h, o, shared['out'][kind], j, tm)
        ffn_cast = shared['ffn']
        ffn_args = (h, shared['raw']['ffn_g'][i:i + 1], *ffn_cast[i], 0)
        h = _ffn_stream(*ffn_args, FFN_COLS) if n <= ROWS else _ffn(*ffn_args, ROWS_FFN, FFN_COLS)
        cast = shared['raw']['ffn'] if (i + 1 not in ffn_cast and i + 1 < depth) else ()
        last = i == depth - 1
        h, casted, stacked = _ple(h, shared['raw']['ple_g'], shared['ple'][0], p_rows, i, shared['ple'][1],
                                  g_final, last, tm, cast, i + 1, (lats, krs) if last else ())
        if casted:
            ffn_cast[i + 1] = casted
    lat, kr = (t.reshape(n_mla, batch, seq, -1) for t in stacked)
    return h.reshape(batch, seq, d), lat, kr, jnp.stack(sks), jnp.stack(svs)


def kernel(x_prompt, x_sample, p_prompt, p_sample, cache_mla_latent, cache_mla_krope, state_swa_k, state_swa_v, g_attn_norm, w_mla_in, g_mla_q_a, w_mla_q_up, g_mla_kv_a, w_mla_kv_up, g_mla_q_nope, g_mla_q_rope, g_mla_k_nope, g_mla_k_rope, w_mla_out, w_swa_qkv, g_swa_q, g_swa_k, swa_sinks, w_swa_out, g_ffn_norm, w_ffn_gate, w_ffn_up, w_ffn_down, g_ple_norm, w_ple_gate, w_ple_proj, g_final):
    depth = g_attn_norm.shape[0]
    layers = []
    for i in range(depth):
        j = i // 2
        if i % 2 == 0:
            layers.append(('mla', _prep_mla(j, g_attn_norm[i], w_mla_in, g_mla_q_a, w_mla_q_up,
                                            g_mla_kv_a, w_mla_kv_up, g_mla_q_nope, g_mla_q_rope,
                                            g_mla_k_nope, g_mla_k_rope)))
        else:
            layers.append(('swa', _prep_swa(j, g_attn_norm[i], w_swa_qkv, g_swa_q, g_swa_k, swa_sinks)))
    shared = dict(raw=dict(ffn=(w_ffn_gate, w_ffn_up, w_ffn_down), ffn_g=g_ffn_norm[:, None, :],
                           ple_gate=w_ple_gate, ple_proj=w_ple_proj, ple_g=g_ple_norm[:, None, :],
                           mla_out=w_mla_out, swa_out=w_swa_out))
    gfin = g_final[None, :]

    seq = x_prompt.shape[1]
    t = x_sample.shape[1]
    past = cache_mla_latent.shape[2]
    pos_p = np.arange(seq)
    pos_s = past + np.arange(t)
    y_p, lat_p, kr_p, sk_p, sv_p = _trunk(x_prompt, p_prompt, pos_p, None, layers, shared, gfin, tm=ROWS)
    caches = (cache_mla_latent, cache_mla_krope, state_swa_k, state_swa_v)
    n_s = x_sample.shape[0] * t
    y_s, lat_s, kr_s, sk_s, sv_s = _trunk(x_sample, p_sample, pos_s, caches, layers, shared, gfin,
                                          tm=min(n_s, ROWS))
    return (y_p, y_s, lat_p, kr_p, sk_p, sv_p, lat_s, kr_s, sk_s, sv_s)
```
